```python
import jax, jax.numpy as jnp
from jax import lax
import numpy as np

D_MODEL = 2048
BATCH = 2
SEQ = 8192
DEPTH = 1
DEC_BATCH = 32
DEC_SEQ = 16
PAST_LEN = 4096

CHUNK = 64
N_LEFT_CHUNKS = 8
BAND_LEFT = N_LEFT_CHUNKS * CHUNK
BAND = BAND_LEFT + CHUNK
H_A = 8
DH_A = D_MODEL // 16
W_A = H_A * DH_A
REL_CLIP = 128
H_B = 4
DK_B = D_MODEL // 16
DV_B = D_MODEL // 8
KW_B = H_B * DK_B
VW_B = H_B * DV_B
ALPHA_RANK = 16
GATE_TAU = 16.0
N_BRANCH = 2
IN_WIDTHS = (W_A, W_A, W_A, KW_B, KW_B, VW_B, VW_B, ALPHA_RANK, N_BRANCH * D_MODEL)
IN_WIDTH = 3 * W_A + 2 * KW_B + 2 * VW_B + ALPHA_RANK + N_BRANCH * D_MODEL
PEER_HEADS = 8
PEER_DK = 256
N_KEYS = 128
N_EXPERTS = N_KEYS * N_KEYS
PEER_TOPK = 16
PEER_BLOCK = 128
EPS = 1e-6
NEG_INF = -1e30

kernel_name = "hybrid_chunk_stream_encoder_step"


def rmsnorm(x, g):
    xf = x.astype(jnp.float32)
    y = xf * lax.rsqrt(jnp.mean(xf * xf, axis=-1, keepdims=True) + EPS)
    return (y * g.astype(jnp.float32)).astype(x.dtype)


def _in_offsets():
    offs, acc = [], 0
    for w in IN_WIDTHS[:-1]:
        acc += w
        offs.append(acc)
    return offs


def band_attention(q, k_band, v_band, q_pos, k_pos, rel_bias):
    s = jnp.einsum('bqhd,bkhd->bhqk', q, k_band).astype(jnp.float32) * (DH_A ** -0.5)
    rel = jnp.clip(q_pos[:, None] - k_pos[None, :], -REL_CLIP, REL_CLIP) + REL_CLIP
    s = s + rel_bias.astype(jnp.float32)[:, rel][None]
    qc = q_pos // CHUNK
    kc = k_pos // CHUNK
    ok = (k_pos[None, :] >= 0) & (kc[None, :] <= qc[:, None]) & (kc[None, :] >= qc[:, None] - N_LEFT_CHUNKS)
    s = jnp.where(ok[None, None], s, NEG_INF)
    p = jax.nn.softmax(s, axis=-1).astype(v_band.dtype)
    return jnp.einsum('bhqk,bkhd->bqhd', p, v_band)


def chunk_band_prompt(q, k, v, rel_bias):
    B, T, H, D = q.shape
    nc = T // CHUNK
    kp = jnp.pad(k, ((0, 0), (BAND_LEFT, 0), (0, 0), (0, 0)))
    vp = jnp.pad(v, ((0, 0), (BAND_LEFT, 0), (0, 0), (0, 0)))

    def one(j):
        start = j * CHUNK
        qj = lax.dynamic_slice_in_dim(q, start, CHUNK, axis=1)
        kj = lax.dynamic_slice_in_dim(kp, start, BAND, axis=1)
        vj = lax.dynamic_slice_in_dim(vp, start, BAND, axis=1)
        q_pos = start + jnp.arange(CHUNK, dtype=jnp.int32)
        k_pos = start - BAND_LEFT + jnp.arange(BAND, dtype=jnp.int32)
        return band_attention(qj, kj, vj, q_pos, k_pos, rel_bias)

    out = lax.map(one, jnp.arange(nc, dtype=jnp.int32))
    return jnp.moveaxis(out, 0, 1).reshape(B, T, H * D)


def chunk_band_sample(q, k, v, ck, cv, rel_bias):
    B, S, H, D = q.shape
    L = ck.shape[1]
    kb = jnp.concatenate([ck.astype(k.dtype), k], axis=1)
    vb = jnp.concatenate([cv.astype(v.dtype), v], axis=1)
    q_pos = PAST_LEN + jnp.arange(S, dtype=jnp.int32)
    k_pos = jnp.concatenate([PAST_LEN - L + jnp.arange(L, dtype=jnp.int32), q_pos])
    return band_attention(q, kb, vb, q_pos, k_pos, rel_bias).reshape(B, S, H * D)


def gla_chunked(q, k, v, g, s0):
    B, T, H, DK = q.shape
    DV = v.shape[-1]
    c = min(CHUNK, T)
    nc = T // c
    f32 = jnp.float32
    q = (q.astype(f32) * (DK ** -0.5)).reshape(B, nc, c, H, DK)
    k = (k.astype(f32) * (DK ** -0.5)).reshape(B, nc, c, H, DK)
    v = v.astype(f32).reshape(B, nc, c, H, DV)
    b = jnp.cumsum(g.astype(f32).reshape(B, nc, c, H, DK), axis=2)
    mid = b[:, :, c // 2:c // 2 + 1]
    a = jnp.einsum('bnthd,bnshd->bnhts', q * jnp.exp(b - mid), k * jnp.exp(mid - b))
    causal = jnp.tril(jnp.ones((c, c), dtype=bool))
    a = jnp.where(causal, a, 0.0)
    o_intra = jnp.einsum('bnhts,bnshv->bnthv', a, v)
    blast = b[:, :, -1:]
    q_dec = jnp.moveaxis(q * jnp.exp(b), 1, 0)
    k_dec = jnp.moveaxis(k * jnp.exp(blast - b), 1, 0)
    v_m = jnp.moveaxis(v, 1, 0)
    dec = jnp.moveaxis(jnp.exp(blast[:, :, 0]), 1, 0)

    def step(S, xs):
        qd, kd, vv, dd = xs
        o = jnp.einsum('bthd,bhdv->bthv', qd, S)
        S = dd[..., None] * S + jnp.einsum('bthd,bthv->bhdv', kd, vv)
        return S, o

    S, o_inter = lax.scan(step, s0.astype(f32), (q_dec, k_dec, v_m, dec))
    o = o_intra + jnp.moveaxis(o_inter, 0, 1)
    return o.reshape(B, T, H, DV), S


def peer_ffn(h, w_q, subkeys, u_tab, v_tab):
    B, T, D = h.shape
    x = h.reshape(B * T, D)
    n = x.shape[0]
    q = (x @ w_q).astype(jnp.float32).reshape(n, PEER_HEADS, 2, PEER_DK // 2)
    s = jnp.einsum('nhpd,pkd->nhpk', q, subkeys.astype(jnp.float32))
    s_top, i_top = lax.top_k(s, PEER_TOPK)
    cand = (s_top[:, :, 0, :, None] + s_top[:, :, 1, None, :]).reshape(n, PEER_HEADS, PEER_TOPK * PEER_TOPK)
    cidx = (i_top[:, :, 0, :, None] * N_KEYS + i_top[:, :, 1, None, :]).reshape(n, PEER_HEADS, PEER_TOPK * PEER_TOPK)
    best, pos = lax.top_k(cand, PEER_TOPK)
    eidx = jnp.take_along_axis(cidx, pos, axis=-1)
    wts = jax.nn.softmax(best, axis=-1)
    nb = -(-n // PEER_BLOCK)
    pad = nb * PEER_BLOCK - n
    hk = PEER_HEADS * PEER_TOPK
    xb = jnp.pad(x, ((0, pad), (0, 0))).reshape(nb, PEER_BLOCK, D)
    eb = jnp.pad(eidx.reshape(n, hk), ((0, pad), (0, 0))).reshape(nb, PEER_BLOCK, hk)
    wb = jnp.pad(wts.reshape(n, hk), ((0, pad), (0, 0))).reshape(nb, PEER_BLOCK, hk)

    def blk(args):
        xx, ee, ww = args
        u = u_tab[ee]
        act = jax.nn.gelu(jnp.einsum('td,ted->te', xx, u).astype(jnp.float32), approximate=False)
        coef = (ww * act).astype(v_tab.dtype)
        return jnp.einsum('te,ted->td', coef, v_tab[ee])

    out = lax.map(blk, (xb, eb, wb)).reshape(nb * PEER_BLOCK, D)[:n]
    return out.reshape(B, T, D).astype(h.dtype)


def trunk_layer(x, c, cache_k, cache_v, s0, is_sample, w_ada, b_ada, g_norm1, w_in, rel_bias,
                w_alpha_up, b_alpha, g_gla_norm, w_branch_a, w_branch_b, w_out, g_norm2,
                w_peer_q, peer_subkeys, peer_u, peer_v):
    B, T, _ = x.shape
    mod = (jax.nn.silu(c) @ w_ada + b_ada)[:, None, :]
    sh1, sc1, gt1, sh2, sc2, gt2 = jnp.split(mod, 6, axis=-1)
    h = rmsnorm(x, g_norm1) * (1 + sc1) + sh1
    z = h @ w_in
    qa, ka, va, qb, kb, vb, rb, alr, gl = jnp.split(z, _in_offsets(), axis=-1)
    qa = qa.reshape(B, T, H_A, DH_A)
    ka = ka.reshape(B, T, H_A, DH_A)
    va = va.reshape(B, T, H_A, DH_A)
    if is_sample:
        oa = chunk_band_sample(qa, ka, va, cache_k, cache_v, rel_bias)
        k_rows, v_rows = ka, va
    else:
        oa = chunk_band_prompt(qa, ka, va, rel_bias)
        keep = min(BAND_LEFT, T)
        k_rows, v_rows = ka[:, T - keep:], va[:, T - keep:]
    logg = jax.nn.log_sigmoid((alr @ w_alpha_up + b_alpha).astype(jnp.float32)) / GATE_TAU
    ob, s_new = gla_chunked(qb.reshape(B, T, H_B, DK_B), kb.reshape(B, T, H_B, DK_B),
                            vb.reshape(B, T, H_B, DV_B), logg.reshape(B, T, H_B, DK_B), s0)
    ob = rmsnorm(ob.astype(x.dtype), g_gla_norm).reshape(B, T, VW_B) * jax.nn.silu(rb)
    ga, gb = jnp.split(jax.nn.sigmoid(gl), N_BRANCH, axis=-1)
    merged = ga * (oa @ w_branch_a) + gb * (ob @ w_branch_b)
    x = x + gt1 * (merged @ w_out)
    h2 = rmsnorm(x, g_norm2) * (1 + sc2) + sh2
    x = x + gt2 * peer_ffn(h2, w_peer_q, peer_subkeys, peer_u, peer_v)
    return x, k_rows, v_rows, s_new.astype(x.dtype)


def setup_inputs(seed: int = 0) -> dict:
    key = jax.random.key(seed)
    ks = jax.random.split(key, 26)
    f32 = jnp.float32
    a_cache = min(BAND_LEFT, PAST_LEN)

    def nrm(k, shape, scale):
        return jax.random.normal(k, shape, f32) * scale

    return {
        "x_prompt": nrm(ks[0], (BATCH, SEQ, D_MODEL), 1.0),
        "x_sample": nrm(ks[1], (DEC_BATCH, DEC_SEQ, D_MODEL), 1.0),
        "cache_a_k": nrm(ks[2], (DEPTH, DEC_BATCH, a_cache, H_A, DH_A), 1.0),
        "cache_a_v": nrm(ks[3], (DEPTH, DEC_BATCH, a_cache, H_A, DH_A), 1.0),
        "state_gla": nrm(ks[4], (DEPTH, DEC_BATCH, H_B, DK_B, DV_B), 0.3),
        "c_prompt": nrm(ks[5], (BATCH, D_MODEL), 1.0),
        "c_sample": nrm(ks[6], (DEC_BATCH, D_MODEL), 1.0),
        "w_ada": nrm(ks[7], (DEPTH, D_MODEL, 6 * D_MODEL), 0.5 * D_MODEL ** -0.5),
        "b_ada": nrm(ks[8], (DEPTH, 6 * D_MODEL), 0.02),
        "g_norm1": 1.0 + nrm(ks[9], (DEPTH, D_MODEL), 0.05),
        "w_in": nrm(ks[10], (DEPTH, D_MODEL, IN_WIDTH), D_MODEL ** -0.5),
        "rel_bias": nrm(ks[11], (DEPTH, H_A, 2 * REL_CLIP + 1), 0.5),
        "w_alpha_up": nrm(ks[12], (DEPTH, ALPHA_RANK, KW_B), ALPHA_RANK ** -0.5),
        "b_alpha": nrm(ks[13], (DEPTH, KW_B), 0.1),
        "g_gla_norm": 1.0 + nrm(ks[14], (DEPTH, DV_B), 0.05),
        "w_branch_a": nrm(ks[15], (DEPTH, W_A, D_MODEL), W_A ** -0.5),
        "w_branch_b": nrm(ks[16], (DEPTH, VW_B, D_MODEL), VW_B ** -0.5),
        "w_out": nrm(ks[17], (DEPTH, D_MODEL, D_MODEL), D_MODEL ** -0.5),
        "g_norm2": 1.0 + nrm(ks[18], (DEPTH, D_MODEL), 0.05),
        "w_peer_q": nrm(ks[19], (DEPTH, D_MODEL, PEER_HEADS * PEER_DK), D_MODEL ** -0.5),
        "peer_subkeys": nrm(ks[20], (DEPTH, 2, N_KEYS, PEER_DK // 2), (PEER_DK // 2) ** -0.5),
        "peer_u": nrm(ks[21], (DEPTH, N_EXPERTS, D_MODEL), D_MODEL ** -0.5),
        "peer_v": nrm(ks[22], (DEPTH, N_EXPERTS, D_MODEL), PEER_HEADS ** -0.5),
        "g_final": 1.0 + nrm(ks[23], (D_MODEL,), 0.05),
    }


def reference(x_prompt, x_sample, cache_a_k, cache_a_v, state_gla, c_prompt, c_sample,
              w_ada, b_ada, g_norm1, w_in, rel_bias, w_alpha_up, b_alpha, g_gla_norm,
              w_branch_a, w_branch_b, w_out, g_norm2, w_peer_q, peer_subkeys, peer_u, peer_v,
              g_final):
    xp, xs = x_prompt, x_sample
    kp_l, vp_l, sp_l, ks_l, vs_l, ss_l = [], [], [], [], [], []
    for l in range(DEPTH):
        w = (w_ada[l], b_ada[l], g_norm1[l], w_in[l], rel_bias[l], w_alpha_up[l], b_alpha[l],
             g_gla_norm[l], w_branch_a[l], w_branch_b[l], w_out[l], g_norm2[l], w_peer_q[l],
             peer_subkeys[l], peer_u[l], peer_v[l])
        s0 = jnp.zeros((xp.shape[0], H_B, DK_B, DV_B), xp.dtype)
        xp, kp, vp, sp = trunk_layer(xp, c_prompt, None, None, s0, False, *w)
        xs, ks_, vs_, ss = trunk_layer(xs, c_sample, cache_a_k[l], cache_a_v[l], state_gla[l], True, *w)
        kp_l.append(kp)
        vp_l.append(vp)
        sp_l.append(sp)
        ks_l.append(ks_)
        vs_l.append(vs_)
        ss_l.append(ss)
    y_prompt = rmsnorm(xp, g_final)
    y_sample = rmsnorm(xs, g_final)
    return (y_prompt, y_sample, jnp.stack(kp_l), jnp.stack(vp_l), jnp.stack(sp_l),
            jnp.stack(ks_l), jnp.stack(vs_l), jnp.stack(ss_l))
```

```python
import functools
import math

import jax
import jax.numpy as jnp
from jax import lax
from jax.experimental import pallas as pl
from jax.experimental.pallas import tpu as pltpu

F32 = jnp.float32
BF16 = jnp.bfloat16

D_MODEL = 2048
CHUNK = 64
N_LEFT_CHUNKS = 8
BAND_LEFT = N_LEFT_CHUNKS * CHUNK
H_A = 8
DH_A = 128
W_A = H_A * DH_A
REL_CLIP = 128
H_B = 4
DK_B = 128
DV_B = 256
KW_B = H_B * DK_B
VW_B = H_B * DV_B
ALPHA_RANK = 16
GATE_TAU = 16.0
PEER_HEADS = 8
N_KEYS = 128
N_EXPERTS = N_KEYS * N_KEYS
PEER_TOPK = 16
EPS = 1e-6
NEG_INF = -1e30

Z_QA, Z_KA, Z_VA = 0, W_A, 2 * W_A
Z_QB = 3 * W_A
Z_KB = Z_QB + KW_B
Z_VB = Z_KB + KW_B
Z_RB = Z_VB + VW_B
Z_GL = Z_RB + VW_B
Z_WIDTH = Z_GL + 2 * D_MODEL
ALR_PAD = 128

VMEM_LIMIT_BIG = 56 * 1024 * 1024
VMEM_LIMIT_MID = 40 * 1024 * 1024

Q_TILE = 512
Q_SUB = 128
K_WIN = Q_SUB + BAND_LEFT


def _cparams(sem, vmem=None):
    return pltpu.CompilerParams(dimension_semantics=sem, vmem_limit_bytes=vmem)


def _rms(xf):
    return xf * lax.rsqrt(jnp.mean(xf * xf, axis=-1, keepdims=True) + EPS)


def _sigmoid(x):
    return 1.0 / (1.0 + jnp.exp(-x))


def _ada_kernel(c_ref, w_ref, b_ref, o_ref):
    c = c_ref[...]
    a = (c * _sigmoid(c)).astype(BF16)
    o_ref[...] = jnp.dot(a, w_ref[...].astype(BF16), preferred_element_type=F32) + b_ref[...]


def _ada(c_all, w_ada, b_ada):
    nb = c_all.shape[0]
    n_out = w_ada.shape[1]
    tn = 1024
    return pl.pallas_call(
        _ada_kernel,
        grid=(n_out // tn,),
        in_specs=[
            pl.BlockSpec((nb, D_MODEL), lambda j: (0, 0)),
            pl.BlockSpec((D_MODEL, tn), lambda j: (0, j)),
            pl.BlockSpec((1, tn), lambda j: (0, j)),
        ],
        out_specs=pl.BlockSpec((nb, tn), lambda j: (0, j)),
        out_shape=jax.ShapeDtypeStruct((nb, n_out), F32),
        compiler_params=_cparams(("arbitrary",), VMEM_LIMIT_MID),
        name="ada",
    )(c_all, w_ada, b_ada.reshape(1, n_out))


def _mod_spec(per_token, tm, tiles_per_batch, grid_rank):
    if per_token:
        if grid_rank == 1:
            return pl.BlockSpec((tm, D_MODEL), lambda m: (m, 0))
        return pl.BlockSpec((tm, D_MODEL), lambda m, n: (m, 0))
    if grid_rank == 1:
        return pl.BlockSpec((None, 1, D_MODEL), lambda m: (m // tiles_per_batch, 0, 0))
    return pl.BlockSpec((None, 1, D_MODEL), lambda m, n: (m // tiles_per_batch, 0, 0))


def _inproj_kernel(x_ref, sh_ref, sc_ref, g_ref, w_ref, walr_ref, z_ref, alr_ref, h_scr):
    @pl.when(pl.program_id(1) == 0)
    def _():
        h = _rms(x_ref[...]) * g_ref[...]
        h = h * (1.0 + sc_ref[...]) + sh_ref[...]
        hb = h.astype(BF16)
        h_scr[...] = hb
        alr_ref[...] = jnp.dot(hb, walr_ref[...], preferred_element_type=F32)

    z_ref[...] = jnp.dot(h_scr[...], w_ref[...], preferred_element_type=F32)


def _inproj(x, sh, sc, g, w_main, w_alr, tm, per_token, rows_per_batch):
    n = x.shape[0]
    tn = 1024
    tpb = max(rows_per_batch // tm, 1)
    return pl.pallas_call(
        _inproj_kernel,
        grid=(n // tm, Z_WIDTH // tn),
        in_specs=[
            pl.BlockSpec((tm, D_MODEL), lambda m, j: (m, 0)),
            _mod_spec(per_token, tm, tpb, 2),
            _mod_spec(per_token, tm, tpb, 2),
            pl.BlockSpec((1, D_MODEL), lambda m, j: (0, 0)),
            pl.BlockSpec((D_MODEL, tn), lambda m, j: (0, j)),
            pl.BlockSpec((D_MODEL, ALR_PAD), lambda m, j: (0, 0)),
        ],
        out_specs=[
            pl.BlockSpec((tm, tn), lambda m, j: (m, j)),
            pl.BlockSpec((tm, ALR_PAD), lambda m, j: (m, 0)),
        ],
        out_shape=[
            jax.ShapeDtypeStruct((n, Z_WIDTH), F32),
            jax.ShapeDtypeStruct((n, ALR_PAD), F32),
        ],
        scratch_shapes=[pltpu.VMEM((tm, D_MODEL), BF16)],
        compiler_params=_cparams(("parallel", "arbitrary"), VMEM_LIMIT_BIG),
        name="inproj",
    )(x, sh, sc, g, w_main, w_alr)


def _attn_prompt_kernel(q_ref, kp_ref, kc_ref, vp_ref, vc_ref, bias_ref, o_ref, kw, vw):
    first = pl.program_id(2) == 0
    kw[0:Q_TILE, :] = kp_ref[...].astype(BF16)
    kw[Q_TILE:2 * Q_TILE, :] = kc_ref[...].astype(BF16)
    vw[0:Q_TILE, :] = vp_ref[...].astype(BF16)
    vw[Q_TILE:2 * Q_TILE, :] = vc_ref[...].astype(BF16)
    bias = bias_ref[...]
    scale = DH_A ** -0.5
    for r in range(Q_TILE // Q_SUB):
        lo = r * Q_SUB
        q = q_ref[lo:lo + Q_SUB, :].astype(BF16)
        s = lax.dot_general(q, kw[lo:lo + K_WIN, :], (((1,), (1,)), ((), ())),
                            preferred_element_type=F32)
        s = s * scale + bias
        col = lax.broadcasted_iota(jnp.int32, (Q_SUB, K_WIN), 1) + lo
        s = jnp.where(jnp.logical_and(first, col < Q_TILE), NEG_INF, s)
        m = jnp.max(s, axis=-1, keepdims=True)
        p = jnp.exp(s - m)
        l = jnp.sum(p, axis=-1, keepdims=True)
        o = jnp.dot(p.astype(BF16), vw[lo:lo + K_WIN, :], preferred_element_type=F32)
        o_ref[lo:lo + Q_SUB, :] = (o / l).astype(o_ref.dtype)


def _attn_prompt(z, bias, batch, seq):
    n = batch * seq
    tiles = seq // Q_TILE

    def cur(col):
        return pl.BlockSpec((Q_TILE, DH_A), lambda h, b, i: (b * tiles + i, col + h))

    def prev(col):
        return pl.BlockSpec((Q_TILE, DH_A),
                            lambda h, b, i: (b * tiles + jnp.maximum(i - 1, 0), col + h))

    kcol, vcol = Z_KA // DH_A, Z_VA // DH_A
    return pl.pallas_call(
        _attn_prompt_kernel,
        grid=(H_A, batch, tiles),
        in_specs=[cur(0), prev(kcol), cur(kcol), prev(vcol), cur(vcol),
                  pl.BlockSpec((None, Q_SUB, K_WIN), lambda h, b, i: (h, 0, 0))],
        out_specs=pl.BlockSpec((Q_TILE, DH_A), lambda h, b, i: (b * tiles + i, h)),
        out_shape=jax.ShapeDtypeStruct((n, W_A), BF16),
        scratch_shapes=[pltpu.VMEM((2 * Q_TILE, DH_A), BF16), pltpu.VMEM((2 * Q_TILE, DH_A), BF16)],
        compiler_params=_cparams(("arbitrary", "arbitrary", "arbitrary")),
        name="attn_prompt",
    )(z, z, z, z, z, bias)


def _attn_sample_kernel(q_ref, kn_ref, vn_ref, ck_ref, cv_ref, bc_ref, bn_ref, o_ref):
    scale = DH_A ** -0.5
    nt = (((1,), (1,)), ((), ()))
    for h in range(H_A):
        sl = slice(h * DH_A, (h + 1) * DH_A)
        q = q_ref[:, sl].astype(BF16)
        sc = lax.dot_general(q, ck_ref[:, sl].astype(BF16), nt, preferred_element_type=F32)
        sn = lax.dot_general(q, kn_ref[:, sl].astype(BF16), nt, preferred_element_type=F32)
        sc = sc * scale + bc_ref[h]
        sn = sn * scale + bn_ref[h]
        m = jnp.maximum(jnp.max(sc, axis=-1, keepdims=True), jnp.max(sn, axis=-1, keepdims=True))
        pc = jnp.exp(sc - m)
        pn = jnp.exp(sn - m)
        l = jnp.sum(pc, axis=-1, keepdims=True) + jnp.sum(pn, axis=-1, keepdims=True)
        o = jnp.dot(pc.astype(BF16), cv_ref[:, sl].astype(BF16), preferred_element_type=F32)
        o = o + jnp.dot(pn.astype(BF16), vn_ref[:, sl].astype(BF16), preferred_element_type=F32)
        o_ref[:, sl] = (o / l).astype(o_ref.dtype)


def _attn_sample(z, cache_k, cache_v, bias_c, bias_n, batch, seq):
    lc = cache_k.shape[1]
    return pl.pallas_call(
        _attn_sample_kernel,
        grid=(batch,),
        in_specs=[
            pl.BlockSpec((seq, W_A), lambda b: (b, Z_QA // W_A)),
            pl.BlockSpec((seq, W_A), lambda b: (b, Z_KA // W_A)),
            pl.BlockSpec((seq, W_A), lambda b: (b, Z_VA // W_A)),
            pl.BlockSpec((None, lc, W_A), lambda b: (b, 0, 0)),
            pl.BlockSpec((None, lc, W_A), lambda b: (b, 0, 0)),
            pl.BlockSpec((H_A, seq, lc), lambda b: (0, 0, 0)),
            pl.BlockSpec((H_A, seq, seq), lambda b: (0, 0, 0)),
        ],
        out_specs=pl.BlockSpec((seq, W_A), lambda b: (b, 0)),
        out_shape=jax.ShapeDtypeStruct((batch * seq, W_A), BF16),
        compiler_params=_cparams(("arbitrary",), VMEM_LIMIT_MID),
        name="attn_sample",
    )(z, z, z, cache_k, cache_v, bias_c, bias_n)


def _split_bf16(x):
    hi = x.astype(BF16)
    lo = (x - hi.astype(F32)).astype(BF16)
    return hi, lo


def _gla_block(q, k, v, g, s_prev, tri, causal, c):
    tn = (((0,), (0,)), ((), ()))
    nt = (((1,), (1,)), ((), ()))
    scale = DK_B ** -0.5
    g_hi, g_lo = _split_bf16(g)
    b = jnp.dot(tri, g_hi, preferred_element_type=F32) + jnp.dot(tri, g_lo, preferred_element_type=F32)
    mid = b[c // 2:c // 2 + 1, :]
    blast = b[c - 1:c, :]
    qs = q * scale
    ks = k * scale
    vb = v.astype(BF16)
    a = lax.dot_general((qs * jnp.exp(b - mid)).astype(BF16), (ks * jnp.exp(mid - b)).astype(BF16), nt,
                        preferred_element_type=F32)
    a = jnp.where(causal, a, 0.0)
    o = jnp.dot(a.astype(BF16), vb, preferred_element_type=F32)
    o = o + jnp.dot((qs * jnp.exp(b)).astype(BF16), s_prev.astype(BF16), preferred_element_type=F32)
    kd = (ks * jnp.exp(blast - b)).astype(BF16)
    ones = jnp.ones((c, DK_B), BF16)
    bl_cols = (lax.dot_general(g_hi, ones, tn, preferred_element_type=F32)
               + lax.dot_general(g_lo, ones, tn, preferred_element_type=F32))
    dec = jnp.exp(bl_cols)
    s_new = jnp.concatenate([dec, dec], axis=1) * s_prev + lax.dot_general(kd, vb, tn, preferred_element_type=F32)
    return o, s_new


def _log_decay(alr, wup, balpha):
    x = jnp.dot(alr.astype(BF16), wup, preferred_element_type=F32) + balpha
    return (jnp.minimum(x, 0.0) - jnp.log1p(jnp.exp(-jnp.abs(x)))) * (1.0 / GATE_TAU)


def _gla_epilogue(o, rb, gnorm):
    y = _rms(o) * gnorm
    return y * (rb * _sigmoid(rb))


def _tri_and_causal(c):
    row = lax.broadcasted_iota(jnp.int32, (c, c), 0)
    col = lax.broadcasted_iota(jnp.int32, (c, c), 1)
    causal = col <= row
    return jnp.where(causal, 1.0, 0.0).astype(BF16), causal


def _gla_prompt_kernel(q_ref, k_ref, v_ref, rb_ref, alr_ref, wup_ref, balpha_ref, gnorm_ref,
                       ob_ref, st_ref, s_scr, *, batch, blocks):
    step = pl.program_id(0)

    @pl.when(step == 0)
    def _():
        s_scr[...] = jnp.zeros_like(s_scr)

    tri, causal = _tri_and_causal(CHUNK)
    wup = wup_ref[...]
    balpha = balpha_ref[...]
    gnorm = gnorm_ref[...]
    for blk in range(blocks):
        rows = slice(blk * CHUNK, (blk + 1) * CHUNK)
        for bi in range(batch):
            logg = _log_decay(alr_ref[bi, rows, :], wup, balpha)
            for h in range(H_B):
                ks = slice(h * DK_B, (h + 1) * DK_B)
                vs = slice(h * DV_B, (h + 1) * DV_B)
                o, s_new = _gla_block(q_ref[bi, rows, ks], k_ref[bi, rows, ks], v_ref[bi, rows, vs],
                                      logg[:, ks], s_scr[bi * H_B + h], tri, causal, CHUNK)
                s_scr[bi * H_B + h] = s_new
                ob_ref[bi, rows, vs] = _gla_epilogue(o, rb_ref[bi, rows, vs], gnorm).astype(ob_ref.dtype)

    @pl.when(step == pl.num_programs(0) - 1)
    def _():
        st_ref[...] = s_scr[...]


def _gla_prompt(z, alr, wup, balpha, gnorm, batch, seq):
    blocks = 4
    rows = blocks * CHUNK
    z3 = z.reshape(batch, seq, Z_WIDTH)
    alr3 = alr.reshape(batch, seq, ALR_PAD)
    ob, st = pl.pallas_call(
        functools.partial(_gla_prompt_kernel, batch=batch, blocks=blocks),
        grid=(seq // rows,),
        in_specs=[
            pl.BlockSpec((batch, rows, KW_B), lambda j: (0, j, Z_QB // KW_B)),
            pl.BlockSpec((batch, rows, KW_B), lambda j: (0, j, Z_KB // KW_B)),
            pl.BlockSpec((batch, rows, VW_B), lambda j: (0, j, Z_VB // VW_B)),
            pl.BlockSpec((batch, rows, VW_B), lambda j: (0, j, Z_RB // VW_B)),
            pl.BlockSpec((batch, rows, ALR_PAD), lambda j: (0, j, 0)),
            pl.BlockSpec((ALR_PAD, KW_B), lambda j: (0, 0)),
            pl.BlockSpec((1, KW_B), lambda j: (0, 0)),
            pl.BlockSpec((1, DV_B), lambda j: (0, 0)),
        ],
        out_specs=[
            pl.BlockSpec((batch, rows, VW_B), lambda j: (0, j, 0)),
            pl.BlockSpec((batch * H_B, DK_B, DV_B), lambda j: (0, 0, 0)),
        ],
        out_shape=[
            jax.ShapeDtypeStruct((batch, seq, VW_B), BF16),
            jax.ShapeDtypeStruct((batch * H_B, DK_B, DV_B), F32),
        ],
        scratch_shapes=[pltpu.VMEM((batch * H_B, DK_B, DV_B), F32)],
        compiler_params=_cparams(("arbitrary",), VMEM_LIMIT_MID),
        name="gla_prompt",
    )(z3, z3, z3, z3, alr3, wup, balpha, gnorm)
    return ob.reshape(batch * seq, VW_B), st.reshape(batch, H_B, DK_B, DV_B)


def _gla_sample_kernel(q_ref, k_ref, v_ref, rb_ref, alr_ref, s0_ref, wup_ref, balpha_ref, gnorm_ref,
                       ob_ref, st_ref, *, seq):
    tri, causal = _tri_and_causal(seq)
    logg = _log_decay(alr_ref[...], wup_ref[...], balpha_ref[...])
    gnorm = gnorm_ref[...]
    for h in range(H_B):
        ks = slice(h * DK_B, (h + 1) * DK_B)
        vs = slice(h * DV_B, (h + 1) * DV_B)
        o, s_new = _gla_block(q_ref[:, ks], k_ref[:, ks], v_ref[:, vs], logg[:, ks], s0_ref[h], tri, causal, seq)
        st_ref[h] = s_new
        ob_ref[:, vs] = _gla_epilogue(o, rb_ref[:, vs], gnorm).astype(ob_ref.dtype)


def _gla_sample(z, alr, state, wup, balpha, gnorm, batch, seq):
    return pl.pallas_call(
        functools.partial(_gla_sample_kernel, seq=seq),
        grid=(batch,),
        in_specs=[
            pl.BlockSpec((seq, KW_B), lambda b: (b, Z_QB // KW_B)),
            pl.BlockSpec((seq, KW_B), lambda b: (b, Z_KB // KW_B)),
            pl.BlockSpec((seq, VW_B), lambda b: (b, Z_VB // VW_B)),
            pl.BlockSpec((seq, VW_B), lambda b: (b, Z_RB // VW_B)),
            pl.BlockSpec((seq, ALR_PAD), lambda b: (b, 0)),
            pl.BlockSpec((None, H_B, DK_B, DV_B), lambda b: (b, 0, 0, 0)),
            pl.BlockSpec((ALR_PAD, KW_B), lambda b: (0, 0)),
            pl.BlockSpec((1, KW_B), lambda b: (0, 0)),
            pl.BlockSpec((1, DV_B), lambda b: (0, 0)),
        ],
        out_specs=[
            pl.BlockSpec((seq, VW_B), lambda b: (b, 0)),
            pl.BlockSpec((None, H_B, DK_B, DV_B), lambda b: (b, 0, 0, 0)),
        ],
        out_shape=[
            jax.ShapeDtypeStruct((batch * seq, VW_B), BF16),
            jax.ShapeDtypeStruct((batch, H_B, DK_B, DV_B), F32),
        ],
        compiler_params=_cparams(("arbitrary",)),
        name="gla_sample",
    )(z, z, z, z, alr, state, wup, balpha, gnorm)


def _merge_kernel(oa_ref, ob_ref, gla_ref, glb_ref, x_ref, gt1_ref, sc2_ref, sh2_ref, g2_ref,
                  wa_ref, wb_ref, wo_ref, x1_ref, h2t_ref):
    ya = jnp.dot(oa_ref[...], wa_ref[...], preferred_element_type=F32)
    yb = jnp.dot(ob_ref[...], wb_ref[...], preferred_element_type=F32)
    merged = _sigmoid(gla_ref[...]) * ya + _sigmoid(glb_ref[...]) * yb
    x1 = x_ref[...] + gt1_ref[...] * jnp.dot(merged.astype(BF16), wo_ref[...], preferred_element_type=F32)
    x1_ref[...] = x1
    h2 = _rms(x1) * g2_ref[...]
    h2 = h2 * (1.0 + sc2_ref[...]) + sh2_ref[...]
    h2t_ref[...] = jnp.transpose(h2).astype(BF16)


def _merge(oa, ob, z, x, gt1, sc2, sh2, g2, wa, wb, wo, tm, per_token, rows_per_batch):
    n = x.shape[0]
    tpb = max(rows_per_batch // tm, 1)
    const = lambda shape: pl.BlockSpec(shape, lambda m: (0, 0))
    mod = _mod_spec(per_token, tm, tpb, 1)
    return pl.pallas_call(
        _merge_kernel,
        grid=(n // tm,),
        in_specs=[
            pl.BlockSpec((tm, W_A), lambda m: (m, 0)),
            pl.BlockSpec((tm, VW_B), lambda m: (m, 0)),
            pl.BlockSpec((tm, D_MODEL), lambda m: (m, Z_GL // D_MODEL)),
            pl.BlockSpec((tm, D_MODEL), lambda m: (m, Z_GL // D_MODEL + 1)),
            pl.BlockSpec((tm, D_MODEL), lambda m: (m, 0)),
            mod, mod, mod,
            const((1, D_MODEL)),
            const((W_A, D_MODEL)), const((VW_B, D_MODEL)), const((D_MODEL, D_MODEL)),
        ],
        out_specs=[
            pl.BlockSpec((tm, D_MODEL), lambda m: (m, 0)),
            pl.BlockSpec((D_MODEL, tm), lambda m: (0, m)),
        ],
        out_shape=[
            jax.ShapeDtypeStruct((n, D_MODEL), F32),
            jax.ShapeDtypeStruct((D_MODEL, n), BF16),
        ],
        compiler_params=_cparams(("arbitrary",), VMEM_LIMIT_BIG),
        name="merge",
    )(oa, ob, z, z, x, gt1, sc2, sh2, g2, wa, wb, wo)


def _qkfold_kernel(wq_ref, sk_ref, o_ref):
    o_ref[...] = lax.dot_general(wq_ref[...], sk_ref[...], (((1,), (1,)), ((), ())),
                                 preferred_element_type=F32)


def _qkfold(w_q, subkeys):
    half = N_KEYS
    return pl.pallas_call(
        _qkfold_kernel,
        grid=(2 * PEER_HEADS,),
        in_specs=[
            pl.BlockSpec((D_MODEL, half), lambda c: (0, c)),
            pl.BlockSpec((None, N_KEYS, half), lambda c: (c % 2, 0, 0)),
        ],
        out_specs=pl.BlockSpec((D_MODEL, N_KEYS), lambda c: (0, c)),
        out_shape=jax.ShapeDtypeStruct((D_MODEL, 2 * PEER_HEADS * N_KEYS), F32),
        compiler_params=_cparams(("arbitrary",)),
        name="qkfold",
    )(w_q, subkeys)


ROUTE_T = 256
LANES = 128
_SET_ROWS = N_KEYS * PEER_HEADS
_PAIRS = [(r, q) for r in range(PEER_TOPK) for q in range(PEER_TOPK) if (r + 1) * (q + 1) <= PEER_TOPK]


def _route_kernel(h2t_ref, wqk_ref, rank1_ref, b1_ref, cnt0_ref, a0_ref,
                  s_scr, sw_scr, rk_scr, val_scr, idx_scr, tmp_scr):
    halves = h2t_ref.shape[1] // LANES
    shp = (PEER_HEADS, LANES)
    s = jnp.dot(wqk_ref[...], h2t_ref[...], preferred_element_type=F32)
    for hf in range(halves):
        s_scr[hf] = s[:, hf * LANES:(hf + 1) * LANES]
    sw_scr[...] = s_scr[...]
    rk_scr[...] = jnp.full(rk_scr.shape, float(PEER_TOPK), F32)
    neg = jnp.full(shp, -jnp.inf, F32)

    def rows(p, k):
        return pl.ds(p * _SET_ROWS + k * PEER_HEADS, PEER_HEADS)

    def one_half(hf, carry0):
        def extract(r, carry):
            rf = r.astype(F32)
            for p in range(2):
                best, bidx = None, None
                for part in range(4):
                    m = neg
                    ix = jnp.zeros(shp, jnp.int32)
                    for k in range(part * 32, part * 32 + 32):
                        v = sw_scr[hf, rows(p, k), :]
                        gt = v > m
                        m = jnp.where(gt, v, m)
                        ix = jnp.where(gt, k, ix)
                    if best is None:
                        best, bidx = m, ix
                    else:
                        gt = m > best
                        best = jnp.where(gt, m, best)
                        bidx = jnp.where(gt, ix, bidx)
                val_scr[p, r] = best
                idx_scr[p, r] = bidx
                for k in range(N_KEYS):
                    hit = bidx == k
                    sw_scr[hf, rows(p, k), :] = jnp.where(hit, -jnp.inf, sw_scr[hf, rows(p, k), :])
                    rk_scr[hf, rows(p, k), :] = jnp.where(hit, rf, rk_scr[hf, rows(p, k), :])
            return carry

        lax.fori_loop(0, PEER_TOPK, extract, 0)

        v0 = [val_scr[0, r] for r in range(PEER_TOPK)]
        v1 = [val_scr[1, q] for q in range(PEER_TOPK)]
        cand = [v0[r] + v1[q] for (r, q) in _PAIRS]
        top = v0[0] + v1[0]
        cnt = [jnp.zeros(shp, F32) for _ in range(PEER_TOPK)]
        zsum = jnp.zeros(shp, F32)
        for _ in range(PEER_TOPK):
            m = neg
            ix = jnp.zeros(shp, jnp.int32)
            for c, cv in enumerate(cand):
                gt = cv > m
                m = jnp.where(gt, cv, m)
                ix = jnp.where(gt, c, ix)
            zsum = zsum + jnp.exp(m - top)
            for c, (r, q) in enumerate(_PAIRS):
                hit = ix == c
                cand[c] = jnp.where(hit, -jnp.inf, cand[c])
                cnt[r] = cnt[r] + jnp.where(hit, 1.0, 0.0)
        zinv = 1.0 / zsum

        idx0 = [idx_scr[0, r] for r in range(PEER_TOPK)]
        for k in range(N_KEYS):
            c0 = jnp.zeros(shp, F32)
            for r in range(PEER_TOPK):
                c0 = jnp.where(idx0[r] == k, cnt[r], c0)
            kr = pl.ds(k * PEER_HEADS, PEER_HEADS)
            tmp_scr[hf, 0, kr, :] = c0
            tmp_scr[hf, 1, kr, :] = jnp.exp(s_scr[hf, rows(0, k), :] - v0[0]) * zinv
            tmp_scr[hf, 2, kr, :] = jnp.exp(s_scr[hf, rows(1, k), :] - v1[0])
        return carry0

    lax.fori_loop(0, halves, one_half, 0)

    for hf in range(halves):
        ls = slice(hf * LANES, (hf + 1) * LANES)
        for h in range(PEER_HEADS):
            dst = slice(h * N_KEYS, (h + 1) * N_KEYS)
            src = pl.ds(h, N_KEYS, stride=PEER_HEADS)
            cnt0_ref[dst, ls] = tmp_scr[hf, 0, src, :]
            a0_ref[dst, ls] = tmp_scr[hf, 1, src, :]
            b1_ref[dst, ls] = tmp_scr[hf, 2, src, :]
            rank1_ref[dst, ls] = rk_scr[hf, pl.ds(_SET_ROWS + h, N_KEYS, stride=PEER_HEADS), :]


def _route(h2t, wqk_t):
    n = h2t.shape[1]
    t = ROUTE_T
    rows = PEER_HEADS * N_KEYS
    out = jax.ShapeDtypeStruct((rows, n), F32)
    ospec = pl.BlockSpec((rows, t), lambda m: (0, m))
    return pl.pallas_call(
        _route_kernel,
        grid=(n // t,),
        in_specs=[
            pl.BlockSpec((D_MODEL, t), lambda m: (0, m)),
            pl.BlockSpec((2 * rows, D_MODEL), lambda m: (0, 0)),
        ],
        out_specs=[ospec, ospec, ospec, ospec],
        out_shape=[out, out, out, out],
        scratch_shapes=[
            pltpu.VMEM((t // LANES, 2 * rows, LANES), F32),
            pltpu.VMEM((t // LANES, 2 * rows, LANES), F32),
            pltpu.VMEM((t // LANES, 2 * rows, LANES), F32),
            pltpu.VMEM((2, PEER_TOPK, PEER_HEADS, LANES), F32),
            pltpu.VMEM((2, PEER_TOPK, PEER_HEADS, LANES), jnp.int32),
            pltpu.VMEM((t // LANES, 3, rows, LANES), F32),
        ],
        compiler_params=_cparams(("arbitrary",), VMEM_LIMIT_MID),
        name="route",
    )(h2t, wqk_t)


PEER_T = 512
PEER_E = 512


def _gelu(x):
    return 0.5 * x * (1.0 + lax.erf(x * (2.0 ** -0.5)))


def _peer_kernel(h2t_ref, u_ref, vt_ref, rank1_ref, b1_ref, cnt0_ref, a0_ref, x1_ref, gt2_ref, gf_ref,
                 y_ref, acc, coef):
    e = pl.program_id(1)

    @pl.when(e == 0)
    def _():
        acc[...] = jnp.zeros_like(acc)

    act = jnp.dot(u_ref[...], h2t_ref[...], preferred_element_type=F32)
    per = PEER_E // N_KEYS
    for ii in range(per):
        i = e * per + ii
        w = None
        for h in range(PEER_HEADS):
            hs = slice(h * N_KEYS, (h + 1) * N_KEYS)
            cnt = cnt0_ref[pl.ds(h * N_KEYS + i, 1), :]
            a = a0_ref[pl.ds(h * N_KEYS + i, 1), :]
            term = jnp.where(rank1_ref[hs, :] < cnt, b1_ref[hs, :] * a, 0.0)
            w = term if w is None else w + term
        rs = slice(ii * N_KEYS, (ii + 1) * N_KEYS)
        coef[rs, :] = (_gelu(act[rs, :]) * w).astype(BF16)
    acc[...] += jnp.dot(vt_ref[...], coef[...], preferred_element_type=F32)

    @pl.when(e == pl.num_programs(1) - 1)
    def _():
        x2 = x1_ref[...] + gt2_ref[...] * jnp.transpose(acc[...])
        y_ref[...] = _rms(x2) * gf_ref[...]


def _peer(h2t, u_bf, vt_bf, rank1, b1, cnt0, a0, x1, gt2, gfin, per_token, rows_per_batch):
    n = x1.shape[0]
    t = PEER_T
    tpb = max(rows_per_batch // t, 1)
    rows = PEER_HEADS * N_KEYS
    tok = lambda r: pl.BlockSpec((r, t), lambda m, e: (0, m))
    if per_token:
        gt_spec = pl.BlockSpec((t, D_MODEL), lambda m, e: (m, 0))
    else:
        gt_spec = pl.BlockSpec((None, 1, D_MODEL), lambda m, e: (m // tpb, 0, 0))
    return pl.pallas_call(
        _peer_kernel,
        grid=(n // t, N_EXPERTS // PEER_E),
        in_specs=[
            tok(D_MODEL),
            pl.BlockSpec((PEER_E, D_MODEL), lambda m, e: (e, 0)),
            pl.BlockSpec((D_MODEL, PEER_E), lambda m, e: (0, e)),
            tok(rows), tok(rows), tok(rows), tok(rows),
            pl.BlockSpec((t, D_MODEL), lambda m, e: (m, 0)),
            gt_spec,
            pl.BlockSpec((1, D_MODEL), lambda m, e: (0, 0)),
        ],
        out_specs=pl.BlockSpec((t, D_MODEL), lambda m, e: (m, 0)),
        out_shape=jax.ShapeDtypeStruct((n, D_MODEL), F32),
        scratch_shapes=[pltpu.VMEM((D_MODEL, t), F32), pltpu.VMEM((PEER_E, t), BF16)],
        compiler_params=_cparams(("parallel", "arbitrary"), VMEM_LIMIT_BIG),
        name="peer",
    )(h2t, u_bf, vt_bf, rank1, b1, cnt0, a0, x1, gt2, gfin)


def _prompt_bias(rel_bias):
    a = jnp.arange(Q_SUB, dtype=jnp.int32)[:, None]
    c = jnp.arange(K_WIN, dtype=jnp.int32)[None, :]
    rel = jnp.clip(a - c + BAND_LEFT, -REL_CLIP, REL_CLIP) + REL_CLIP
    cq = a // CHUNK
    ck = c // CHUNK
    ok = (ck >= cq) & (ck <= cq + N_LEFT_CHUNKS)
    return jnp.where(ok[None], rel_bias.astype(F32)[:, rel], NEG_INF)


def _sample_bias(rel_bias, seq, lc):
    a = jnp.arange(seq, dtype=jnp.int32)[:, None]
    kc = jnp.arange(lc, dtype=jnp.int32)[None, :]
    rel_c = jnp.clip(a - kc + lc, -REL_CLIP, REL_CLIP) + REL_CLIP
    rel_n = jnp.clip(a - jnp.arange(seq, dtype=jnp.int32)[None, :], -REL_CLIP, REL_CLIP) + REL_CLIP
    rb = rel_bias.astype(F32)
    return rb[:, rel_c], rb[:, rel_n]


def _layer(x, mod, cache_k, cache_v, state, is_sample, wts):
    (g1, w_main, w_alr, rel_bias, wup, balpha, gnorm, wa, wb, wo, g2, wqk_t, u_bf, vt_bf, gfin) = wts
    batch, seq, _ = x.shape
    n = batch * seq
    xf = x.reshape(n, D_MODEL)
    parts = [mod[:, i * D_MODEL:(i + 1) * D_MODEL] for i in range(6)]
    if is_sample:
        sh1, sc1, gt1, sh2, sc2, gt2 = [jnp.repeat(p, seq, axis=0) for p in parts]
        tm_in, tm_mg = n, 256
    else:
        sh1, sc1, gt1, sh2, sc2, gt2 = [p.reshape(batch, 1, D_MODEL) for p in parts]
        tm_in, tm_mg = 1024, 256

    z, alr = _inproj(xf, sh1, sc1, g1, w_main, w_alr, tm_in, is_sample, seq)

    if is_sample:
        lc = cache_k.shape[1]
        bias_c, bias_n = _sample_bias(rel_bias, seq, lc)
        oa = _attn_sample(z, cache_k.reshape(batch, lc, W_A), cache_v.reshape(batch, lc, W_A),
                          bias_c, bias_n, batch, seq)
        ob, s_new = _gla_sample(z, alr, state, wup, balpha, gnorm, batch, seq)
        k_rows = z[:, Z_KA:Z_KA + W_A].reshape(batch, seq, H_A, DH_A)
        v_rows = z[:, Z_VA:Z_VA + W_A].reshape(batch, seq, H_A, DH_A)
    else:
        oa = _attn_prompt(z, _prompt_bias(rel_bias), batch, seq)
        ob, s_new = _gla_prompt(z, alr, wup, balpha, gnorm, batch, seq)
        keep = min(BAND_LEFT, seq)
        zk = z.reshape(batch, seq, Z_WIDTH)[:, seq - keep:]
        k_rows = zk[:, :, Z_KA:Z_KA + W_A].reshape(batch, keep, H_A, DH_A)
        v_rows = zk[:, :, Z_VA:Z_VA + W_A].reshape(batch, keep, H_A, DH_A)

    x1, h2t = _merge(oa, ob, z, xf, gt1, sc2, sh2, g2, wa, wb, wo, tm_mg, is_sample, seq)
    rank1, b1, cnt0, a0 = _route(h2t, wqk_t)
    y = _peer(h2t, u_bf, vt_bf, rank1, b1, cnt0, a0, x1, gt2, gfin, is_sample, seq)
    return y.reshape(batch, seq, D_MODEL), k_rows, v_rows, s_new


def kernel(x_prompt, x_sample, cache_a_k, cache_a_v, state_gla, c_prompt, c_sample, w_ada, b_ada, g_norm1, w_in, rel_bias, w_alpha_up, b_alpha, g_gla_norm, w_branch_a, w_branch_b, w_out, g_norm2, w_peer_q, peer_subkeys, peer_u, peer_v, g_final):
    depth = w_in.shape[0]
    assert depth == 1, "the final rmsnorm is fused into the PEER kernel of the only layer"
    nbp = c_prompt.shape[0]
    l = 0
    nbs = c_sample.shape[0]
    pad = -(nbp + nbs) % 16
    c_all = jnp.concatenate([c_prompt, c_sample, jnp.zeros((pad, D_MODEL), F32)], axis=0)
    mod = _ada(c_all, w_ada[l], b_ada[l])

    alr_lo = Z_GL
    w_main = jnp.concatenate([w_in[l][:, :alr_lo], w_in[l][:, alr_lo + ALPHA_RANK:]], axis=1).astype(BF16)
    w_alr = jnp.pad(w_in[l][:, alr_lo:alr_lo + ALPHA_RANK], ((0, 0), (0, ALR_PAD - ALPHA_RANK))).astype(BF16)
    wup = jnp.pad(w_alpha_up[l], ((0, ALR_PAD - ALPHA_RANK), (0, 0))).astype(BF16)
    wqk = _qkfold(w_peer_q[l].astype(BF16), peer_subkeys[l].astype(BF16))
    wqk_t = wqk.reshape(D_MODEL, PEER_HEADS, 2, N_KEYS).transpose(2, 3, 1, 0).reshape(2 * PEER_HEADS * N_KEYS, D_MODEL)
    wts = (
        g_norm1[l].reshape(1, D_MODEL), w_main, w_alr, rel_bias[l], wup,
        b_alpha[l].reshape(1, KW_B), g_gla_norm[l].reshape(1, DV_B),
        w_branch_a[l].astype(BF16), w_branch_b[l].astype(BF16), w_out[l].astype(BF16),
        g_norm2[l].reshape(1, D_MODEL), wqk_t.astype(BF16),
        peer_u[l].astype(BF16), jnp.transpose(peer_v[l]).astype(BF16), g_final.reshape(1, D_MODEL),
    )
    yp, kp, vp, sp = _layer(x_prompt, mod[:nbp], None, None, None, False, wts)
    ys, ks, vs, ss = _layer(x_sample, mod[nbp:nbp + nbs], cache_a_k[l], cache_a_v[l], state_gla[l], True, wts)
    return (yp, ys, kp[None], vp[None], sp[None], ks[None], vs[None], ss[None])
```

```python
import functools
import math

import jax
import jax.numpy as jnp
from jax import lax
from jax.experimental import pallas as pl
from jax.experimental.pallas import tpu as pltpu

F32 = jnp.float32
BF16 = jnp.bfloat16

D_MODEL = 2048
CHUNK = 64
N_LEFT_CHUNKS = 8
BAND_LEFT = N_LEFT_CHUNKS * CHUNK
H_A = 8
DH_A = 128
W_A = H_A * DH_A
REL_CLIP = 128
H_B = 4
DK_B = 128
DV_B = 256
KW_B = H_B * DK_B
VW_B = H_B * DV_B
ALPHA_RANK = 16
GATE_TAU = 16.0
PEER_HEADS = 8
N_KEYS = 128
N_EXPERTS = N_KEYS * N_KEYS
PEER_TOPK = 16
EPS = 1e-6
NEG_INF = -1e30

Z_QA, Z_KA, Z_VA = 0, W_A, 2 * W_A
Z_QB = 3 * W_A
Z_KB = Z_QB + KW_B
Z_VB = Z_KB + KW_B
Z_RB = Z_VB + VW_B
Z_GL = Z_RB + VW_B
Z_WIDTH = Z_GL + 2 * D_MODEL
ALR_PAD = 128

VMEM_LIMIT_BIG = 56 * 1024 * 1024
VMEM_LIMIT_MID = 40 * 1024 * 1024

Q_TILE = 512
Q_SUB = 128
K_WIN = Q_SUB + BAND_LEFT


def _cparams(sem, vmem=None):
    return pltpu.CompilerParams(dimension_semantics=sem, vmem_limit_bytes=vmem)


def _rms(xf):
    return xf * lax.rsqrt(jnp.mean(xf * xf, axis=-1, keepdims=True) + EPS)


def _sigmoid(x):
    return 1.0 / (1.0 + jnp.exp(-x))


def _ada_kernel(c_ref, w_ref, b_ref, o_ref):
    c = c_ref[...]
    a = (c * _sigmoid(c)).astype(BF16)
    o_ref[...] = jnp.dot(a, w_ref[...].astype(BF16), preferred_element_type=F32) + b_ref[...]


def _ada(c_all, w_ada, b_ada):
    nb = c_all.shape[0]
    n_out = w_ada.shape[1]
    tn = 1024
    return pl.pallas_call(
        _ada_kernel,
        grid=(n_out // tn,),
        in_specs=[
            pl.BlockSpec((nb, D_MODEL), lambda j: (0, 0)),
            pl.BlockSpec((D_MODEL, tn), lambda j: (0, j)),
            pl.BlockSpec((1, tn), lambda j: (0, j)),
        ],
        out_specs=pl.BlockSpec((nb, tn), lambda j: (0, j)),
        out_shape=jax.ShapeDtypeStruct((nb, n_out), F32),
        compiler_params=_cparams(("arbitrary",), VMEM_LIMIT_MID),
        name="ada",
    )(c_all, w_ada, b_ada.reshape(1, n_out))


def _mod_spec(per_token, tm, tiles_per_batch, grid_rank):
    if per_token:
        if grid_rank == 1:
            return pl.BlockSpec((tm, D_MODEL), lambda m: (m, 0))
        return pl.BlockSpec((tm, D_MODEL), lambda m, n: (m, 0))
    if grid_rank == 1:
        return pl.BlockSpec((None, 1, D_MODEL), lambda m: (m // tiles_per_batch, 0, 0))
    return pl.BlockSpec((None, 1, D_MODEL), lambda m, n: (m // tiles_per_batch, 0, 0))


def _inproj_kernel(x_ref, sh_ref, sc_ref, g_ref, w_ref, walr_ref, z_ref, alr_ref, h_scr):
    @pl.when(pl.program_id(1) == 0)
    def _():
        h = _rms(x_ref[...]) * g_ref[...]
        h = h * (1.0 + sc_ref[...]) + sh_ref[...]
        hb = h.astype(BF16)
        h_scr[...] = hb
        alr_ref[...] = jnp.dot(hb, walr_ref[...], preferred_element_type=F32)

    z_ref[...] = jnp.dot(h_scr[...], w_ref[...], preferred_element_type=F32)


def _inproj(x, sh, sc, g, w_main, w_alr, tm, per_token, rows_per_batch):
    n = x.shape[0]
    tn = 1024
    tpb = max(rows_per_batch // tm, 1)
    return pl.pallas_call(
        _inproj_kernel,
        grid=(n // tm, Z_WIDTH // tn),
        in_specs=[
            pl.BlockSpec((tm, D_MODEL), lambda m, j: (m, 0)),
            _mod_spec(per_token, tm, tpb, 2),
            _mod_spec(per_token, tm, tpb, 2),
            pl.BlockSpec((1, D_MODEL), lambda m, j: (0, 0)),
            pl.BlockSpec((D_MODEL, tn), lambda m, j: (0, j)),
            pl.BlockSpec((D_MODEL, ALR_PAD), lambda m, j: (0, 0)),
        ],
        out_specs=[
            pl.BlockSpec((tm, tn), lambda m, j: (m, j)),
            pl.BlockSpec((tm, ALR_PAD), lambda m, j: (m, 0)),
        ],
        out_shape=[
            jax.ShapeDtypeStruct((n, Z_WIDTH), F32),
            jax.ShapeDtypeStruct((n, ALR_PAD), F32),
        ],
        scratch_shapes=[pltpu.VMEM((tm, D_MODEL), BF16)],
        compiler_params=_cparams(("parallel", "arbitrary"), VMEM_LIMIT_BIG),
        name="inproj",
    )(x, sh, sc, g, w_main, w_alr)


def _attn_prompt_kernel(q_ref, kp_ref, kc_ref, vp_ref, vc_ref, bias_ref, o_ref, kw, vw):
    first = pl.program_id(2) == 0
    kw[0:Q_TILE, :] = kp_ref[...].astype(BF16)
    kw[Q_TILE:2 * Q_TILE, :] = kc_ref[...].astype(BF16)
    vw[0:Q_TILE, :] = vp_ref[...].astype(BF16)
    vw[Q_TILE:2 * Q_TILE, :] = vc_ref[...].astype(BF16)
    bias = bias_ref[...]
    scale = DH_A ** -0.5
    for r in range(Q_TILE // Q_SUB):
        lo = r * Q_SUB
        q = q_ref[lo:lo + Q_SUB, :].astype(BF16)
        s = lax.dot_general(q, kw[lo:lo + K_WIN, :], (((1,), (1,)), ((), ())),
                            preferred_element_type=F32)
        s = s * scale + bias
        col = lax.broadcasted_iota(jnp.int32, (Q_SUB, K_WIN), 1) + lo
        s = jnp.where(jnp.logical_and(first, col < Q_TILE), NEG_INF, s)
        m = jnp.max(s, axis=-1, keepdims=True)
        p = jnp.exp(s - m)
        l = jnp.sum(p, axis=-1, keepdims=True)
        o = jnp.dot(p.astype(BF16), vw[lo:lo + K_WIN, :], preferred_element_type=F32)
        o_ref[lo:lo + Q_SUB, :] = (o / l).astype(o_ref.dtype)


def _attn_prompt(z, bias, batch, seq):
    n = batch * seq
    tiles = seq // Q_TILE

    def cur(col):
        return pl.BlockSpec((Q_TILE, DH_A), lambda h, b, i: (b * tiles + i, col + h))

    def prev(col):
        return pl.BlockSpec((Q_TILE, DH_A),
                            lambda h, b, i: (b * tiles + jnp.maximum(i - 1, 0), col + h))

    kcol, vcol = Z_KA // DH_A, Z_VA // DH_A
    return pl.pallas_call(
        _attn_prompt_kernel,
        grid=(H_A, batch, tiles),
        in_specs=[cur(0), prev(kcol), cur(kcol), prev(vcol), cur(vcol),
                  pl.BlockSpec((None, Q_SUB, K_WIN), lambda h, b, i: (h, 0, 0))],
        out_specs=pl.BlockSpec((Q_TILE, DH_A), lambda h, b, i: (b * tiles + i, h)),
        out_shape=jax.ShapeDtypeStruct((n, W_A), BF16),
        scratch_shapes=[pltpu.VMEM((2 * Q_TILE, DH_A), BF16), pltpu.VMEM((2 * Q_TILE, DH_A), BF16)],
        compiler_params=_cparams(("arbitrary", "arbitrary", "arbitrary")),
        name="attn_prompt",
    )(z, z, z, z, z, bias)


def _attn_sample_kernel(q_ref, kn_ref, vn_ref, ck_ref, cv_ref, bc_ref, bn_ref, o_ref):
    scale = DH_A ** -0.5
    nt = (((1,), (1,)), ((), ()))
    for h in range(H_A):
        sl = slice(h * DH_A, (h + 1) * DH_A)
        q = q_ref[:, sl].astype(BF16)
        sc = lax.dot_general(q, ck_ref[:, sl].astype(BF16), nt, preferred_element_type=F32)
        sn = lax.dot_general(q, kn_ref[:, sl].astype(BF16), nt, preferred_element_type=F32)
        sc = sc * scale + bc_ref[h]
        sn = sn * scale + bn_ref[h]
        m = jnp.maximum(jnp.max(sc, axis=-1, keepdims=True), jnp.max(sn, axis=-1, keepdims=True))
        pc = jnp.exp(sc - m)
        pn = jnp.exp(sn - m)
        l = jnp.sum(pc, axis=-1, keepdims=True) + jnp.sum(pn, axis=-1, keepdims=True)
        o = jnp.dot(pc.astype(BF16), cv_ref[:, sl].astype(BF16), preferred_element_type=F32)
        o = o + jnp.dot(pn.astype(BF16), vn_ref[:, sl].astype(BF16), preferred_element_type=F32)
        o_ref[:, sl] = (o / l).astype(o_ref.dtype)


def _attn_sample(z, cache_k, cache_v, bias_c, bias_n, batch, seq):
    lc = cache_k.shape[1]
    return pl.pallas_call(
        _attn_sample_kernel,
        grid=(batch,),
        in_specs=[
            pl.BlockSpec((seq, W_A), lambda b: (b, Z_QA // W_A)),
            pl.BlockSpec((seq, W_A), lambda b: (b, Z_KA // W_A)),
            pl.BlockSpec((seq, W_A), lambda b: (b, Z_VA // W_A)),
            pl.BlockSpec((None, lc, W_A), lambda b: (b, 0, 0)),
            pl.BlockSpec((None, lc, W_A), lambda b: (b, 0, 0)),
            pl.BlockSpec((H_A, seq, lc), lambda b: (0, 0, 0)),
            pl.BlockSpec((H_A, seq, seq), lambda b: (0, 0, 0)),
        ],
        out_specs=pl.BlockSpec((seq, W_A), lambda b: (b, 0)),
        out_shape=jax.ShapeDtypeStruct((batch * seq, W_A), BF16),
        compiler_params=_cparams(("arbitrary",), VMEM_LIMIT_MID),
        name="attn_sample",
    )(z, z, z, cache_k, cache_v, bias_c, bias_n)


def _split_bf16(x):
    hi = x.astype(BF16)
    lo = (x - hi.astype(F32)).astype(BF16)
    return hi, lo


def _gla_block(q, k, v, g, s_prev, tri, causal, c):
    tn = (((0,), (0,)), ((), ()))
    nt = (((1,), (1,)), ((), ()))
    scale = DK_B ** -0.5
    g_hi, g_lo = _split_bf16(g)
    b = jnp.dot(tri, g_hi, preferred_element_type=F32) + jnp.dot(tri, g_lo, preferred_element_type=F32)
    mid = b[c // 2:c // 2 + 1, :]
    blast = b[c - 1:c, :]
    qs = q * scale
    ks = k * scale
    vb = v.astype(BF16)
    a = lax.dot_general((qs * jnp.exp(b - mid)).astype(BF16), (ks * jnp.exp(mid - b)).astype(BF16), nt,
                        preferred_element_type=F32)
    a = jnp.where(causal, a, 0.0)
    o = jnp.dot(a.astype(BF16), vb, preferred_element_type=F32)
    o = o + jnp.dot((qs * jnp.exp(b)).astype(BF16), s_prev.astype(BF16), preferred_element_type=F32)
    kd = (ks * jnp.exp(blast - b)).astype(BF16)
    ones = jnp.ones((c, DK_B), BF16)
    bl_cols = (lax.dot_general(g_hi, ones, tn, preferred_element_type=F32)
               + lax.dot_general(g_lo, ones, tn, preferred_element_type=F32))
    dec = jnp.exp(bl_cols)
    s_new = jnp.concatenate([dec, dec], axis=1) * s_prev + lax.dot_general(kd, vb, tn, preferred_element_type=F32)
    return o, s_new


def _log_decay(alr, wup, balpha):
    x = jnp.dot(alr.astype(BF16), wup, preferred_element_type=F32) + balpha
    return (jnp.minimum(x, 0.0) - jnp.log1p(jnp.exp(-jnp.abs(x)))) * (1.0 / GATE_TAU)


def _gla_epilogue(o, rb, gnorm):
    y = _rms(o) * gnorm
    return y * (rb * _sigmoid(rb))


def _tri_and_causal(c):
    row = lax.broadcasted_iota(jnp.int32, (c, c), 0)
    col = lax.broadcasted_iota(jnp.int32, (c, c), 1)
    causal = col <= row
    return jnp.where(causal, 1.0, 0.0).astype(BF16), causal


def _gla_prompt_kernel(q_ref, k_ref, v_ref, rb_ref, alr_ref, wup_ref, balpha_ref, gnorm_ref,
                       ob_ref, st_ref, s_scr, *, batch, blocks):
    step = pl.program_id(0)

    @pl.when(step == 0)
    def _():
        s_scr[...] = jnp.zeros_like(s_scr)

    tri, causal = _tri_and_causal(CHUNK)
    wup = wup_ref[...]
    balpha = balpha_ref[...]
    gnorm = gnorm_ref[...]
    for blk in range(blocks):
        rows = slice(blk * CHUNK, (blk + 1) * CHUNK)
        for bi in range(batch):
            logg = _log_decay(alr_ref[bi, rows, :], wup, balpha)
            for h in range(H_B):
                ks = slice(h * DK_B, (h + 1) * DK_B)
                vs = slice(h * DV_B, (h + 1) * DV_B)
                o, s_new = _gla_block(q_ref[bi, rows, ks], k_ref[bi, rows, ks], v_ref[bi, rows, vs],
                                      logg[:, ks], s_scr[bi * H_B + h], tri, causal, CHUNK)
                s_scr[bi * H_B + h] = s_new
                ob_ref[bi, rows, vs] = _gla_epilogue(o, rb_ref[bi, rows, vs], gnorm).astype(ob_ref.dtype)

    @pl.when(step == pl.num_programs(0) - 1)
    def _():
        st_ref[...] = s_scr[...]


def _gla_prompt(z, alr, wup, balpha, gnorm, batch, seq):
    blocks = 4
    rows = blocks * CHUNK
    z3 = z.reshape(batch, seq, Z_WIDTH)
    alr3 = alr.reshape(batch, seq, ALR_PAD)
    ob, st = pl.pallas_call(
        functools.partial(_gla_prompt_kernel, batch=batch, blocks=blocks),
        grid=(seq // rows,),
        in_specs=[
            pl.BlockSpec((batch, rows, KW_B), lambda j: (0, j, Z_QB // KW_B)),
            pl.BlockSpec((batch, rows, KW_B), lambda j: (0, j, Z_KB // KW_B)),
            pl.BlockSpec((batch, rows, VW_B), lambda j: (0, j, Z_VB // VW_B)),
            pl.BlockSpec((batch, rows, VW_B), lambda j: (0, j, Z_RB // VW_B)),
            pl.BlockSpec((batch, rows, ALR_PAD), lambda j: (0, j, 0)),
            pl.BlockSpec((ALR_PAD, KW_B), lambda j: (0, 0)),
            pl.BlockSpec((1, KW_B), lambda j: (0, 0)),
            pl.BlockSpec((1, DV_B), lambda j: (0, 0)),
        ],
        out_specs=[
            pl.BlockSpec((batch, rows, VW_B), lambda j: (0, j, 0)),
            pl.BlockSpec((batch * H_B, DK_B, DV_B), lambda j: (0, 0, 0)),
        ],
        out_shape=[
            jax.ShapeDtypeStruct((batch, seq, VW_B), BF16),
            jax.ShapeDtypeStruct((batch * H_B, DK_B, DV_B), F32),
        ],
        scratch_shapes=[pltpu.VMEM((batch * H_B, DK_B, DV_B), F32)],
        compiler_params=_cparams(("arbitrary",), VMEM_LIMIT_MID),
        name="gla_prompt",
    )(z3, z3, z3, z3, alr3, wup, balpha, gnorm)
    return ob.reshape(batch * seq, VW_B), st.reshape(batch, H_B, DK_B, DV_B)


def _gla_sample_kernel(q_ref, k_ref, v_ref, rb_ref, alr_ref, s0_ref, wup_ref, balpha_ref, gnorm_ref,
                       ob_ref, st_ref, *, seq):
    tri, causal = _tri_and_causal(seq)
    logg = _log_decay(alr_ref[...], wup_ref[...], balpha_ref[...])
    gnorm = gnorm_ref[...]
    for h in range(H_B):
        ks = slice(h * DK_B, (h + 1) * DK_B)
        vs = slice(h * DV_B, (h + 1) * DV_B)
        o, s_new = _gla_block(q_ref[:, ks], k_ref[:, ks], v_ref[:, vs], logg[:, ks], s0_ref[h], tri, causal, seq)
        st_ref[h] = s_new
        ob_ref[:, vs] = _gla_epilogue(o, rb_ref[:, vs], gnorm).astype(ob_ref.dtype)


def _gla_sample(z, alr, state, wup, balpha, gnorm, batch, seq):
    return pl.pallas_call(
        functools.partial(_gla_sample_kernel, seq=seq),
        grid=(batch,),
        in_specs=[
            pl.BlockSpec((seq, KW_B), lambda b: (b, Z_QB // KW_B)),
            pl.BlockSpec((seq, KW_B), lambda b: (b, Z_KB // KW_B)),
            pl.BlockSpec((seq, VW_B), lambda b: (b, Z_VB // VW_B)),
            pl.BlockSpec((seq, VW_B), lambda b: (b, Z_RB // VW_B)),
            pl.BlockSpec((seq, ALR_PAD), lambda b: (b, 0)),
            pl.BlockSpec((None, H_B, DK_B, DV_B), lambda b: (b, 0, 0, 0)),
            pl.BlockSpec((ALR_PAD, KW_B), lambda b: (0, 0)),
            pl.BlockSpec((1, KW_B), lambda b: (0, 0)),
            pl.BlockSpec((1, DV_B), lambda b: (0, 0)),
        ],
        out_specs=[
            pl.BlockSpec((seq, VW_B), lambda b: (b, 0)),
            pl.BlockSpec((None, H_B, DK_B, DV_B), lambda b: (b, 0, 0, 0)),
        ],
        out_shape=[
            jax.ShapeDtypeStruct((batch * seq, VW_B), BF16),
            jax.ShapeDtypeStruct((batch, H_B, DK_B, DV_B), F32),
        ],
        compiler_params=_cparams(("arbitrary",)),
        name="gla_sample",
    )(z, z, z, z, alr, state, wup, balpha, gnorm)


def _merge_kernel(oa_ref, ob_ref, gla_ref, glb_ref, x_ref, gt1_ref, sc2_ref, sh2_ref, g2_ref,
                  wa_ref, wb_ref, wo_ref, x1_ref, h2t_ref):
    ya = jnp.dot(oa_ref[...], wa_ref[...], preferred_element_type=F32)
    yb = jnp.dot(ob_ref[...], wb_ref[...], preferred_element_type=F32)
    merged = _sigmoid(gla_ref[...]) * ya + _sigmoid(glb_ref[...]) * yb
    x1 = x_ref[...] + gt1_ref[...] * jnp.dot(merged.astype(BF16), wo_ref[...], preferred_element_type=F32)
    x1_ref[...] = x1
    h2 = _rms(x1) * g2_ref[...]
    h2 = h2 * (1.0 + sc2_ref[...]) + sh2_ref[...]
    h2t_ref[...] = jnp.transpose(h2).astype(BF16)


def _merge(oa, ob, z, x, gt1, sc2, sh2, g2, wa, wb, wo, tm, per_token, rows_per_batch):
    n = x.shape[0]
    tpb = max(rows_per_batch // tm, 1)
    const = lambda shape: pl.BlockSpec(shape, lambda m: (0, 0))
    mod = _mod_spec(per_token, tm, tpb, 1)
    return pl.pallas_call(
        _merge_kernel,
        grid=(n // tm,),
        in_specs=[
            pl.BlockSpec((tm, W_A), lambda m: (m, 0)),
            pl.BlockSpec((tm, VW_B), lambda m: (m, 0)),
            pl.BlockSpec((tm, D_MODEL), lambda m: (m, Z_GL // D_MODEL)),
            pl.BlockSpec((tm, D_MODEL), lambda m: (m, Z_GL // D_MODEL + 1)),
            pl.BlockSpec((tm, D_MODEL), lambda m: (m, 0)),
            mod, mod, mod,
            const((1, D_MODEL)),
            const((W_A, D_MODEL)), const((VW_B, D_MODEL)), const((D_MODEL, D_MODEL)),
        ],
        out_specs=[
            pl.BlockSpec((tm, D_MODEL), lambda m: (m, 0)),
            pl.BlockSpec((D_MODEL, tm), lambda m: (0, m)),
        ],
        out_shape=[
            jax.ShapeDtypeStruct((n, D_MODEL), F32),
            jax.ShapeDtypeStruct((D_MODEL, n), BF16),
        ],
        compiler_params=_cparams(("arbitrary",), VMEM_LIMIT_BIG),
        name="merge",
    )(oa, ob, z, z, x, gt1, sc2, sh2, g2, wa, wb, wo)


def _qkfold_kernel(wq_ref, sk_ref, o_ref):
    o_ref[...] = lax.dot_general(wq_ref[...], sk_ref[...], (((1,), (1,)), ((), ())),
                                 preferred_element_type=F32)


def _qkfold(w_q, subkeys):
    half = N_KEYS
    return pl.pallas_call(
        _qkfold_kernel,
        grid=(2 * PEER_HEADS,),
        in_specs=[
            pl.BlockSpec((D_MODEL, half), lambda c: (0, c)),
            pl.BlockSpec((None, N_KEYS, half), lambda c: (c % 2, 0, 0)),
        ],
        out_specs=pl.BlockSpec((D_MODEL, N_KEYS), lambda c: (0, c)),
        out_shape=jax.ShapeDtypeStruct((D_MODEL, 2 * PEER_HEADS * N_KEYS), F32),
        compiler_params=_cparams(("arbitrary",)),
        name="qkfold",
    )(w_q, subkeys)


ROUTE_T = 256
LANES = 128
_SET_ROWS = N_KEYS * PEER_HEADS
_PAIRS = [(r, q) for r in range(PEER_TOPK) for q in range(PEER_TOPK) if (r + 1) * (q + 1) <= PEER_TOPK]


def _route_kernel(h2t_ref, wqk_ref, rank1_ref, b1_ref, cnt0_ref, a0_ref,
                  s_scr, sw_scr, rk_scr, val_scr, idx_scr, tmp_scr):
    halves = h2t_ref.shape[1] // LANES
    shp = (PEER_HEADS, LANES)
    s = jnp.dot(wqk_ref[...], h2t_ref[...], preferred_element_type=F32)
    for hf in range(halves):
        s_scr[hf] = s[:, hf * LANES:(hf + 1) * LANES]
    sw_scr[...] = s_scr[...]
    rk_scr[...] = jnp.full(rk_scr.shape, float(PEER_TOPK), F32)
    neg = jnp.full(shp, -jnp.inf, F32)

    def rows(p, k):
        return pl.ds(p * _SET_ROWS + k * PEER_HEADS, PEER_HEADS)

    def one_half(hf, carry0):
        def extract(r, carry):
            rf = r.astype(F32)
            for p in range(2):
                best, bidx = None, None
                for part in range(4):
                    m = neg
                    ix = jnp.zeros(shp, jnp.int32)
                    for k in range(part * 32, part * 32 + 32):
                        v = sw_scr[hf, rows(p, k), :]
                        gt = v > m
                        m = jnp.where(gt, v, m)
                        ix = jnp.where(gt, k, ix)
                    if best is None:
                        best, bidx = m, ix
                    else:
                        gt = m > best
                        best = jnp.where(gt, m, best)
                        bidx = jnp.where(gt, ix, bidx)
                val_scr[p, r] = best
                idx_scr[p, r] = bidx
                for k in range(N_KEYS):
                    hit = bidx == k
                    sw_scr[hf, rows(p, k), :] = jnp.where(hit, -jnp.inf, sw_scr[hf, rows(p, k), :])
                    rk_scr[hf, rows(p, k), :] = jnp.where(hit, rf, rk_scr[hf, rows(p, k), :])
            return carry

        lax.fori_loop(0, PEER_TOPK, extract, 0)

        v0 = [val_scr[0, r] for r in range(PEER_TOPK)]
        v1 = [val_scr[1, q] for q in range(PEER_TOPK)]
        cand = [v0[r] + v1[q] for (r, q) in _PAIRS]
        top = v0[0] + v1[0]
        cnt = [jnp.zeros(shp, F32) for _ in range(PEER_TOPK)]
        zsum = jnp.zeros(shp, F32)
        for _ in range(PEER_TOPK):
            m = neg
            ix = jnp.zeros(shp, jnp.int32)
            for c, cv in enumerate(cand):
                gt = cv > m
                m = jnp.where(gt, cv, m)
                ix = jnp.where(gt, c, ix)
            zsum = zsum + jnp.exp(m - top)
            for c, (r, q) in enumerate(_PAIRS):
                hit = ix == c
                cand[c] = jnp.where(hit, -jnp.inf, cand[c])
                cnt[r] = cnt[r] + jnp.where(hit, 1.0, 0.0)
        zinv = 1.0 / zsum

        idx0 = [idx_scr[0, r] for r in range(PEER_TOPK)]
        for k in range(N_KEYS):
            c0 = jnp.zeros(shp, F32)
            for r in range(PEER_TOPK):
                c0 = jnp.where(idx0[r] == k, cnt[r], c0)
            kr = pl.ds(k * PEER_HEADS, PEER_HEADS)
            tmp_scr[hf, 0, kr, :] = c0
            tmp_scr[hf, 1, kr, :] = jnp.exp(s_scr[hf, rows(0, k), :] - v0[0]) * zinv
            tmp_scr[hf, 2, kr, :] = jnp.exp(s_scr[hf, rows(1, k), :] - v1[0])
        return carry0

    lax.fori_loop(0, halves, one_half, 0)

    for hf in range(halves):
        ls = slice(hf * LANES, (hf + 1) * LANES)
        for h in range(PEER_HEADS):
            dst = slice(h * N_KEYS, (h + 1) * N_KEYS)
            src = pl.ds(h, N_KEYS, stride=PEER_HEADS)
            cnt0_ref[dst, ls] = _pack_pair(tmp_scr[hf, 0, src, :])
            a0_ref[dst, ls] = _pack_pair(tmp_scr[hf, 1, src, :])
            b1_ref[dst, ls] = tmp_scr[hf, 2, src, :].astype(BF16)
            rank1_ref[dst, ls] = rk_scr[hf, pl.ds(_SET_ROWS + h, N_KEYS, stride=PEER_HEADS), :].astype(BF16)


def _pack_pair(x):
    bits = pltpu.bitcast(x.astype(BF16).astype(F32), jnp.uint32)
    return bits | (bits >> 16)


def _route(h2t, wqk_t):
    n = h2t.shape[1]
    t = ROUTE_T
    rows = PEER_HEADS * N_KEYS
    ospec = pl.BlockSpec((rows, t), lambda m: (0, m))
    out_bf = jax.ShapeDtypeStruct((rows, n), BF16)
    out_pk = jax.ShapeDtypeStruct((rows, n), jnp.uint32)
    return pl.pallas_call(
        _route_kernel,
        grid=(n // t,),
        in_specs=[
            pl.BlockSpec((D_MODEL, t), lambda m: (0, m)),
            pl.BlockSpec((2 * rows, D_MODEL), lambda m: (0, 0)),
        ],
        out_specs=[ospec, ospec, ospec, ospec],
        out_shape=[out_bf, out_bf, out_pk, out_pk],
        scratch_shapes=[
            pltpu.VMEM((t // LANES, 2 * rows, LANES), F32),
            pltpu.VMEM((t // LANES, 2 * rows, LANES), F32),
            pltpu.VMEM((t // LANES, 2 * rows, LANES), F32),
            pltpu.VMEM((2, PEER_TOPK, PEER_HEADS, LANES), F32),
            pltpu.VMEM((2, PEER_TOPK, PEER_HEADS, LANES), jnp.int32),
            pltpu.VMEM((t // LANES, 3, rows, LANES), F32),
        ],
        compiler_params=_cparams(("arbitrary",), VMEM_LIMIT_MID),
        name="route",
    )(h2t, wqk_t)


PEER_T = 512
PEER_E = 512


def _gelu(x):
    return 0.5 * x * (1.0 + lax.erf(x * (2.0 ** -0.5)))


def _bcast_pair_rows(ref, row, t):
    word = jnp.broadcast_to(ref[pl.ds(row, 1), :], (8, t))
    pair = pltpu.bitcast(word, BF16)
    return jnp.broadcast_to(pair[None], (N_KEYS // 16, 16, t)).reshape(N_KEYS, t)


def _peer_coef(tile, u_ref, row0, h2t_ref, rank1_ref, b1_ref, cnt0_ref, a0_ref, coef_ref):
    t = h2t_ref.shape[1]
    act = jnp.dot(u_ref[row0:row0 + PEER_E, :], h2t_ref[...], preferred_element_type=F32)
    per = PEER_E // N_KEYS
    for ii in range(per):
        i = tile * per + ii
        w = None
        for h in range(PEER_HEADS):
            hs = slice(h * N_KEYS, (h + 1) * N_KEYS)
            cnt = _bcast_pair_rows(cnt0_ref, h * N_KEYS + i, t)
            a = _bcast_pair_rows(a0_ref, h * N_KEYS + i, t)
            term = jnp.where(rank1_ref[hs, :] < cnt, b1_ref[hs, :] * a, jnp.zeros((), BF16))
            w = term if w is None else w + term
        rs = slice(ii * N_KEYS, (ii + 1) * N_KEYS)
        coef_ref[rs, :] = _gelu(act[rs, :]).astype(BF16) * w


def _peer_kernel(h2t_ref, u_ref, vtp_ref, vtc_ref, rank1_ref, b1_ref, cnt0_ref, a0_ref, x1_ref, gt2_ref, gf_ref,
                 y_ref, acc, coef_a, coef_b):
    k = pl.program_id(1)
    last = pl.num_programs(1) - 1
    route = (rank1_ref, b1_ref, cnt0_ref, a0_ref)

    def two_tiles():
        _peer_coef(2 * k, u_ref, 0, h2t_ref, *route, coef_a)
        _peer_coef(2 * k + 1, u_ref, PEER_E, h2t_ref, *route, coef_b)
        return jnp.dot(vtc_ref[...], coef_a[...], preferred_element_type=F32)

    @pl.when(k == 0)
    def _():
        acc[...] = two_tiles()

    @pl.when(jnp.logical_and(k > 0, k < last))
    def _():
        acc[...] += jnp.dot(vtp_ref[...], coef_b[...], preferred_element_type=F32)
        acc[...] += two_tiles()

    @pl.when(k == last)
    def _():
        tot = acc[...] + jnp.dot(vtp_ref[...], coef_b[...], preferred_element_type=F32)
        x2 = x1_ref[...] + gt2_ref[...] * jnp.transpose(tot)
        y_ref[...] = _rms(x2) * gf_ref[...]


def _peer(h2t, u_bf, vt_bf, rank1, b1, cnt0, a0, x1, gt2, gfin, per_token, rows_per_batch):
    n = x1.shape[0]
    t = PEER_T
    tpb = max(rows_per_batch // t, 1)
    rows = PEER_HEADS * N_KEYS
    steps = N_EXPERTS // (2 * PEER_E)
    once = pl.Buffered(1)
    tok = lambda r: pl.BlockSpec((r, t), lambda m, k: (0, m), pipeline_mode=once)
    if per_token:
        gt_spec = pl.BlockSpec((t, D_MODEL), lambda m, k: (m, 0), pipeline_mode=once)
    else:
        gt_spec = pl.BlockSpec((None, 1, D_MODEL), lambda m, k: (m // tpb, 0, 0))
    return pl.pallas_call(
        _peer_kernel,
        grid=(n // t, steps + 1),
        in_specs=[
            pl.BlockSpec((D_MODEL, t), lambda m, k: (0, m)),
            pl.BlockSpec((2 * PEER_E, D_MODEL), lambda m, k: (jnp.minimum(k, steps - 1), 0)),
            pl.BlockSpec((D_MODEL, PEER_E), lambda m, k: (0, jnp.maximum(2 * k - 1, 0))),
            pl.BlockSpec((D_MODEL, PEER_E), lambda m, k: (0, 2 * jnp.minimum(k, steps - 1))),
            tok(rows), tok(rows), tok(rows), tok(rows),
            pl.BlockSpec((t, D_MODEL), lambda m, k: (m, 0), pipeline_mode=once),
            gt_spec,
            pl.BlockSpec((1, D_MODEL), lambda m, k: (0, 0)),
        ],
        out_specs=pl.BlockSpec((t, D_MODEL), lambda m, k: (m, 0)),
        out_shape=jax.ShapeDtypeStruct((n, D_MODEL), F32),
        scratch_shapes=[pltpu.VMEM((D_MODEL, t), F32), pltpu.VMEM((PEER_E, t), BF16), pltpu.VMEM((PEER_E, t), BF16)],
        compiler_params=_cparams(("parallel", "arbitrary"), VMEM_LIMIT_BIG),
        name="peer",
    )(h2t, u_bf, vt_bf, vt_bf, rank1, b1, cnt0, a0, x1, gt2, gfin)


def _rel_bias_tile(rel_bias, rows, cols, offset):
    rb = rel_bias.astype(F32)
    heads = rb.shape[0]
    rel_max = offset + rows - 1
    rel_min = offset - (cols - 1)
    lo, hi = max(rel_min, -REL_CLIP), min(rel_max, REL_CLIP)
    parts = []
    if rel_min < -REL_CLIP:
        parts.append(jnp.broadcast_to(rb[:, :1], (heads, -REL_CLIP - rel_min)))
    parts.append(rb[:, lo + REL_CLIP:hi + REL_CLIP + 1])
    if rel_max > REL_CLIP:
        parts.append(jnp.broadcast_to(rb[:, -1:], (heads, rel_max - REL_CLIP)))
    g = jnp.flip(jnp.concatenate(parts, axis=1), axis=1)
    period = rows + cols
    gp = jnp.pad(g, ((0, 0), (0, 1)))
    shifted = jnp.tile(gp, (1, rows))[:, :rows * (period - 1)].reshape(heads, rows, period - 1)
    return shifted[:, :, rows - 1:rows - 1 + cols]


def _prompt_bias(rel_bias):
    a = jnp.arange(Q_SUB, dtype=jnp.int32)[:, None]
    c = jnp.arange(K_WIN, dtype=jnp.int32)[None, :]
    cq = a // CHUNK
    ck = c // CHUNK
    ok = (ck >= cq) & (ck <= cq + N_LEFT_CHUNKS)
    return jnp.where(ok[None], _rel_bias_tile(rel_bias, Q_SUB, K_WIN, BAND_LEFT), NEG_INF)


def _sample_bias(rel_bias, seq, lc):
    return _rel_bias_tile(rel_bias, seq, lc, lc), _rel_bias_tile(rel_bias, seq, seq, 0)


def _layer(x, mod, cache_k, cache_v, state, is_sample, wts):
    (g1, w_main, w_alr, rel_bias, wup, balpha, gnorm, wa, wb, wo, g2, wqk_t, u_bf, vt_bf, gfin) = wts
    batch, seq, _ = x.shape
    n = batch * seq
    xf = x.reshape(n, D_MODEL)
    parts = [mod[:, i * D_MODEL:(i + 1) * D_MODEL] for i in range(6)]
    if is_sample:
        sh1, sc1, gt1, sh2, sc2, gt2 = [jnp.repeat(p, seq, axis=0) for p in parts]
        tm_in, tm_mg = n, 256
    else:
        sh1, sc1, gt1, sh2, sc2, gt2 = [p.reshape(batch, 1, D_MODEL) for p in parts]
        tm_in, tm_mg = 1024, 256

    z, alr = _inproj(xf, sh1, sc1, g1, w_main, w_alr, tm_in, is_sample, seq)

    if is_sample:
        lc = cache_k.shape[1]
        bias_c, bias_n = _sample_bias(rel_bias, seq, lc)
        oa = _attn_sample(z, cache_k.reshape(batch, lc, W_A), cache_v.reshape(batch, lc, W_A),
                          bias_c, bias_n, batch, seq)
        ob, s_new = _gla_sample(z, alr, state, wup, balpha, gnorm, batch, seq)
        k_rows = z[:, Z_KA:Z_KA + W_A].reshape(batch, seq, H_A, DH_A)
        v_rows = z[:, Z_VA:Z_VA + W_A].reshape(batch, seq, H_A, DH_A)
    else:
        oa = _attn_prompt(z, _prompt_bias(rel_bias), batch, seq)
        ob, s_new = _gla_prompt(z, alr, wup, balpha, gnorm, batch, seq)
        keep = min(BAND_LEFT, seq)
        zk = z.reshape(batch, seq, Z_WIDTH)[:, seq - keep:]
        k_rows = zk[:, :, Z_KA:Z_KA + W_A].reshape(batch, keep, H_A, DH_A)
        v_rows = zk[:, :, Z_VA:Z_VA + W_A].reshape(batch, keep, H_A, DH_A)

    x1, h2t = _merge(oa, ob, z, xf, gt1, sc2, sh2, g2, wa, wb, wo, tm_mg, is_sample, seq)
    rank1, b1, cnt0, a0 = _route(h2t, wqk_t)
    y = _peer(h2t, u_bf, vt_bf, rank1, b1, cnt0, a0, x1, gt2, gfin, is_sample, seq)
    return y.reshape(batch, seq, D_MODEL), k_rows, v_rows, s_new


def kernel(x_prompt, x_sample, cache_a_k, cache_a_v, state_gla, c_prompt, c_sample, w_ada, b_ada, g_norm1, w_in, rel_bias, w_alpha_up, b_alpha, g_gla_norm, w_branch_a, w_branch_b, w_out, g_norm2, w_peer_q, peer_subkeys, peer_u, peer_v, g_final):
    depth = w_in.shape[0]
    assert depth == 1, "the final rmsnorm is fused into the PEER kernel of the only layer"
    nbp = c_prompt.shape[0]
    l = 0
    nbs = c_sample.shape[0]
    pad = -(nbp + nbs) % 16
    c_all = jnp.concatenate([c_prompt, c_sample, jnp.zeros((pad, D_MODEL), F32)], axis=0)
    def only(w):
        return w.reshape(w.shape[1:])

    mod = _ada(c_all, only(w_ada), b_ada[l])

    alr_lo = Z_GL
    w_in0 = only(w_in)
    w_main = jnp.concatenate([w_in0[:, :alr_lo].astype(BF16), w_in0[:, alr_lo + ALPHA_RANK:].astype(BF16)], axis=1)
    w_alr = jnp.pad(w_in0[:, alr_lo:alr_lo + ALPHA_RANK], ((0, 0), (0, ALR_PAD - ALPHA_RANK))).astype(BF16)
    wup = jnp.pad(w_alpha_up[l], ((0, ALR_PAD - ALPHA_RANK), (0, 0))).astype(BF16)
    wqk = _qkfold(only(w_peer_q).astype(BF16), peer_subkeys[l].astype(BF16))
    wqk_t = wqk.reshape(D_MODEL, PEER_HEADS, 2, N_KEYS).transpose(2, 3, 1, 0).reshape(2 * PEER_HEADS * N_KEYS, D_MODEL)
    wts = (
        g_norm1[l].reshape(1, D_MODEL), w_main, w_alr, rel_bias[l], wup,
        b_alpha[l].reshape(1, KW_B), g_gla_norm[l].reshape(1, DV_B),
        only(w_branch_a).astype(BF16), only(w_branch_b).astype(BF16), only(w_out).astype(BF16),
        g_norm2[l].reshape(1, D_MODEL), wqk_t.astype(BF16),
        only(peer_u).astype(BF16), jnp.transpose(only(peer_v).astype(BF16)), g_final.reshape(1, D_MODEL),
    )
    yp, kp, vp, sp = _layer(x_prompt, mod[:nbp], None, None, None, False, wts)
    ys, ks, vs, ss = _layer(x_sample, mod[nbp:nbp + nbs], only(cache_a_k), only(cache_a_v), only(state_gla), True, wts)
    return (yp, ys, kp[None], vp[None], sp[None], ks[None], vs[None], ss[None])
```

```python
import functools
import math

import jax
import jax.numpy as jnp
from jax import lax
from jax.experimental import pallas as pl
from jax.experimental.pallas import tpu as pltpu

F32 = jnp.float32
BF16 = jnp.bfloat16

D_MODEL = 2048
CHUNK = 64
N_LEFT_CHUNKS = 8
BAND_LEFT = N_LEFT_CHUNKS * CHUNK
H_A = 8
DH_A = 128
W_A = H_A * DH_A
REL_CLIP = 128
H_B = 4
DK_B = 128
DV_B = 256
KW_B = H_B * DK_B
VW_B = H_B * DV_B
ALPHA_RANK = 16
GATE_TAU = 16.0
PEER_HEADS = 8
N_KEYS = 128
N_EXPERTS = N_KEYS * N_KEYS
PEER_TOPK = 16
EPS = 1e-6
NEG_INF = -1e30

Z_QA, Z_KA, Z_VA = 0, W_A, 2 * W_A
Z_QB = 3 * W_A
Z_KB = Z_QB + KW_B
Z_VB = Z_KB + KW_B
Z_RB = Z_VB + VW_B
Z_GL = Z_RB + VW_B
Z_WIDTH = Z_GL + 2 * D_MODEL
ALR_PAD = 128

VMEM_LIMIT_BIG = 56 * 1024 * 1024
VMEM_LIMIT_MID = 40 * 1024 * 1024

Q_TILE = 512
Q_SUB = 128
K_WIN = Q_SUB + BAND_LEFT


def _cparams(sem, vmem=None):
    return pltpu.CompilerParams(dimension_semantics=sem, vmem_limit_bytes=vmem)


def _rms(xf):
    return xf * lax.rsqrt(jnp.mean(xf * xf, axis=-1, keepdims=True) + EPS)


def _sigmoid(x):
    return 1.0 / (1.0 + jnp.exp(-x))


def _ada_kernel(c_ref, w_ref, b_ref, o_ref):
    c = c_ref[...]
    a = (c * _sigmoid(c)).astype(BF16)
    o_ref[...] = jnp.dot(a, w_ref[...].astype(BF16), preferred_element_type=F32) + b_ref[...]


def _ada(c_all, w_ada, b_ada):
    nb = c_all.shape[0]
    n_out = w_ada.shape[1]
    tn = 1024
    return pl.pallas_call(
        _ada_kernel,
        grid=(n_out // tn,),
        in_specs=[
            pl.BlockSpec((nb, D_MODEL), lambda j: (0, 0)),
            pl.BlockSpec((D_MODEL, tn), lambda j: (0, j)),
            pl.BlockSpec((1, tn), lambda j: (0, j)),
        ],
        out_specs=pl.BlockSpec((nb, tn), lambda j: (0, j)),
        out_shape=jax.ShapeDtypeStruct((nb, n_out), F32),
        compiler_params=_cparams(("arbitrary",), VMEM_LIMIT_MID),
        name="ada",
    )(c_all, w_ada, b_ada.reshape(1, n_out))


def _mod_spec(per_token, tm, tiles_per_batch, grid_rank):
    if per_token:
        if grid_rank == 1:
            return pl.BlockSpec((tm, D_MODEL), lambda m: (m, 0))
        return pl.BlockSpec((tm, D_MODEL), lambda m, n: (m, 0))
    if grid_rank == 1:
        return pl.BlockSpec((None, 1, D_MODEL), lambda m: (m // tiles_per_batch, 0, 0))
    return pl.BlockSpec((None, 1, D_MODEL), lambda m, n: (m // tiles_per_batch, 0, 0))


def _inproj_kernel(x_ref, sh_ref, sc_ref, g_ref, w_ref, walr_ref, z_ref, alr_ref, h_scr):
    @pl.when(pl.program_id(1) == 0)
    def _():
        h = _rms(x_ref[...]) * g_ref[...]
        h = h * (1.0 + sc_ref[...]) + sh_ref[...]
        hb = h.astype(BF16)
        h_scr[...] = hb
        alr_ref[...] = jnp.dot(hb, walr_ref[...], preferred_element_type=F32)

    z_ref[...] = jnp.dot(h_scr[...], w_ref[...], preferred_element_type=F32)


def _inproj(x, sh, sc, g, w_main, w_alr, tm, per_token, rows_per_batch):
    n = x.shape[0]
    tn = 1024
    tpb = max(rows_per_batch // tm, 1)
    return pl.pallas_call(
        _inproj_kernel,
        grid=(n // tm, Z_WIDTH // tn),
        in_specs=[
            pl.BlockSpec((tm, D_MODEL), lambda m, j: (m, 0)),
            _mod_spec(per_token, tm, tpb, 2),
            _mod_spec(per_token, tm, tpb, 2),
            pl.BlockSpec((1, D_MODEL), lambda m, j: (0, 0)),
            pl.BlockSpec((D_MODEL, tn), lambda m, j: (0, j)),
            pl.BlockSpec((D_MODEL, ALR_PAD), lambda m, j: (0, 0)),
        ],
        out_specs=[
            pl.BlockSpec((tm, tn), lambda m, j: (m, j)),
            pl.BlockSpec((tm, ALR_PAD), lambda m, j: (m, 0)),
        ],
        out_shape=[
            jax.ShapeDtypeStruct((n, Z_WIDTH), F32),
            jax.ShapeDtypeStruct((n, ALR_PAD), F32),
        ],
        scratch_shapes=[pltpu.VMEM((tm, D_MODEL), BF16)],
        compiler_params=_cparams(("parallel", "arbitrary"), VMEM_LIMIT_BIG),
        name="inproj",
    )(x, sh, sc, g, w_main, w_alr)


def _attn_prompt_kernel(q_ref, kp_ref, kc_ref, vp_ref, vc_ref, bias_ref, o_ref, kw, vw):
    first = pl.program_id(2) == 0
    kw[0:Q_TILE, :] = kp_ref[...].astype(BF16)
    kw[Q_TILE:2 * Q_TILE, :] = kc_ref[...].astype(BF16)
    vw[0:Q_TILE, :] = vp_ref[...].astype(BF16)
    vw[Q_TILE:2 * Q_TILE, :] = vc_ref[...].astype(BF16)
    bias = bias_ref[...]
    scale = DH_A ** -0.5
    subs = range(Q_TILE // Q_SUB)
    scores = [lax.dot_general(q_ref[r * Q_SUB:(r + 1) * Q_SUB, :].astype(BF16), kw[r * Q_SUB:r * Q_SUB + K_WIN, :],
                              (((1,), (1,)), ((), ())), preferred_element_type=F32) for r in subs]
    probs, sums = [], []
    for r in subs:
        s = scores[r] * scale + bias
        col = lax.broadcasted_iota(jnp.int32, (Q_SUB, K_WIN), 1) + r * Q_SUB
        s = jnp.where(jnp.logical_and(first, col < Q_TILE), NEG_INF, s)
        m = jnp.max(s, axis=-1, keepdims=True)
        p = jnp.exp(s - m)
        sums.append(jnp.sum(p, axis=-1, keepdims=True))
        probs.append(p.astype(BF16))
    outs = [jnp.dot(probs[r], vw[r * Q_SUB:r * Q_SUB + K_WIN, :], preferred_element_type=F32) for r in subs]
    for r in subs:
        o_ref[r * Q_SUB:(r + 1) * Q_SUB, :] = (outs[r] / sums[r]).astype(o_ref.dtype)


def _attn_prompt(z, bias, batch, seq):
    n = batch * seq
    tiles = seq // Q_TILE

    def cur(col):
        return pl.BlockSpec((Q_TILE, DH_A), lambda h, b, i: (b * tiles + i, col + h))

    def prev(col):
        return pl.BlockSpec((Q_TILE, DH_A),
                            lambda h, b, i: (b * tiles + jnp.maximum(i - 1, 0), col + h))

    kcol, vcol = Z_KA // DH_A, Z_VA // DH_A
    return pl.pallas_call(
        _attn_prompt_kernel,
        grid=(H_A, batch, tiles),
        in_specs=[cur(0), prev(kcol), cur(kcol), prev(vcol), cur(vcol),
                  pl.BlockSpec((None, Q_SUB, K_WIN), lambda h, b, i: (h, 0, 0))],
        out_specs=pl.BlockSpec((Q_TILE, DH_A), lambda h, b, i: (b * tiles + i, h)),
        out_shape=jax.ShapeDtypeStruct((n, W_A), BF16),
        scratch_shapes=[pltpu.VMEM((2 * Q_TILE, DH_A), BF16), pltpu.VMEM((2 * Q_TILE, DH_A), BF16)],
        compiler_params=_cparams(("arbitrary", "arbitrary", "arbitrary")),
        name="attn_prompt",
    )(z, z, z, z, z, bias)


def _attn_sample_kernel(q_ref, kn_ref, vn_ref, ck_ref, cv_ref, bc_ref, bn_ref, o_ref):
    scale = DH_A ** -0.5
    nt = (((1,), (1,)), ((), ()))
    lc = ck_ref.shape[0] // H_A
    heads = range(H_A)
    cols = [slice(h * DH_A, (h + 1) * DH_A) for h in heads]
    crow = [pl.ds(h, lc, stride=H_A) for h in heads]
    qs = [q_ref[:, cols[h]].astype(BF16) for h in heads]
    s_c = [lax.dot_general(qs[h], ck_ref[crow[h], :].astype(BF16), nt, preferred_element_type=F32) for h in heads]
    s_n = [lax.dot_general(qs[h], kn_ref[:, cols[h]].astype(BF16), nt, preferred_element_type=F32) for h in heads]
    p_c, p_n, sums = [], [], []
    for h in heads:
        sc = s_c[h] * scale + bc_ref[h]
        sn = s_n[h] * scale + bn_ref[h]
        m = jnp.maximum(jnp.max(sc, axis=-1, keepdims=True), jnp.max(sn, axis=-1, keepdims=True))
        pc = jnp.exp(sc - m)
        pn = jnp.exp(sn - m)
        sums.append(jnp.sum(pc, axis=-1, keepdims=True) + jnp.sum(pn, axis=-1, keepdims=True))
        p_c.append(pc.astype(BF16))
        p_n.append(pn.astype(BF16))
    outs = [jnp.dot(p_c[h], cv_ref[crow[h], :].astype(BF16), preferred_element_type=F32)
            + jnp.dot(p_n[h], vn_ref[:, cols[h]].astype(BF16), preferred_element_type=F32) for h in heads]
    for h in heads:
        o_ref[:, cols[h]] = (outs[h] / sums[h]).astype(o_ref.dtype)


def _attn_sample(z, cache_k, cache_v, bias_c, bias_n, batch, seq):
    lc = cache_k.shape[1] // H_A
    return pl.pallas_call(
        _attn_sample_kernel,
        grid=(batch,),
        in_specs=[
            pl.BlockSpec((seq, W_A), lambda b: (b, Z_QA // W_A)),
            pl.BlockSpec((seq, W_A), lambda b: (b, Z_KA // W_A)),
            pl.BlockSpec((seq, W_A), lambda b: (b, Z_VA // W_A)),
            pl.BlockSpec((None, lc * H_A, DH_A), lambda b: (b, 0, 0)),
            pl.BlockSpec((None, lc * H_A, DH_A), lambda b: (b, 0, 0)),
            pl.BlockSpec((H_A, seq, lc), lambda b: (0, 0, 0)),
            pl.BlockSpec((H_A, seq, seq), lambda b: (0, 0, 0)),
        ],
        out_specs=pl.BlockSpec((seq, W_A), lambda b: (b, 0)),
        out_shape=jax.ShapeDtypeStruct((batch * seq, W_A), BF16),
        compiler_params=_cparams(("arbitrary",), VMEM_LIMIT_MID),
        name="attn_sample",
    )(z, z, z, cache_k, cache_v, bias_c, bias_n)


def _split_bf16(x):
    hi = x.astype(BF16)
    lo = (x - hi.astype(F32)).astype(BF16)
    return hi, lo


_TN = (((0,), (0,)), ((), ()))
_NT = (((1,), (1,)), ((), ()))


def _gla_state_free(units, tri, causal, c):
    scale = DK_B ** -0.5
    ones = jnp.ones((c, DK_B), BF16)
    split = [_split_bf16(g) for (_, _, _, g) in units]
    cum = [jnp.dot(tri, hi, preferred_element_type=F32) + jnp.dot(tri, lo, preferred_element_type=F32)
           for (hi, lo) in split]
    tot = [lax.dot_general(hi, ones, _TN, preferred_element_type=F32)
           + lax.dot_general(lo, ones, _TN, preferred_element_type=F32) for (hi, lo) in split]
    qd, kd, vb, att = [], [], [], []
    for (q, k, v, _), b in zip(units, cum):
        mid = b[c // 2:c // 2 + 1, :]
        blast = b[c - 1:c, :]
        qs = q * scale
        ks = k * scale
        att.append(lax.dot_general((qs * jnp.exp(b - mid)).astype(BF16), (ks * jnp.exp(mid - b)).astype(BF16),
                                   _NT, preferred_element_type=F32))
        qd.append((qs * jnp.exp(b)).astype(BF16))
        kd.append((ks * jnp.exp(blast - b)).astype(BF16))
        vb.append(v.astype(BF16))
    intra = [jnp.dot(jnp.where(causal, a, 0.0).astype(BF16), v, preferred_element_type=F32)
             for a, v in zip(att, vb)]
    out = []
    for i in range(len(units)):
        dec = jnp.exp(tot[i])
        out.append((intra[i], qd[i], kd[i], vb[i], jnp.concatenate([dec, dec], axis=1)))
    return out


def _gla_read_state(pre, s_prev):
    o_intra, qd, _, _, _ = pre
    return o_intra + jnp.dot(qd, s_prev.astype(BF16), preferred_element_type=F32)


def _gla_next_state(pre, s_prev):
    _, _, kd, vb, dec = pre
    return dec * s_prev + lax.dot_general(kd, vb, _TN, preferred_element_type=F32)


def _log_decay(alr, wup, balpha):
    x = jnp.dot(alr.astype(BF16), wup, preferred_element_type=F32) + balpha
    return (jnp.minimum(x, 0.0) - jnp.log1p(jnp.exp(-jnp.abs(x)))) * (1.0 / GATE_TAU)


def _gla_epilogue(o, rb, gnorm):
    y = _rms(o) * gnorm
    return y * (rb * _sigmoid(rb))


def _tri_and_causal(c):
    row = lax.broadcasted_iota(jnp.int32, (c, c), 0)
    col = lax.broadcasted_iota(jnp.int32, (c, c), 1)
    causal = col <= row
    return jnp.where(causal, 1.0, 0.0).astype(BF16), causal


def _gla_prompt_kernel(q_ref, k_ref, v_ref, rb_ref, alr_ref, wup_ref, balpha_ref, gnorm_ref,
                       ob_ref, st_ref, s_scr, *, batch, blocks):
    step = pl.program_id(0)

    @pl.when(step == 0)
    def _():
        s_scr[...] = jnp.zeros_like(s_scr)

    tri, causal = _tri_and_causal(CHUNK)
    wup = wup_ref[...]
    balpha = balpha_ref[...]
    gnorm = gnorm_ref[...]
    ksl = [slice(h * DK_B, (h + 1) * DK_B) for h in range(H_B)]
    vsl = [slice(h * DV_B, (h + 1) * DV_B) for h in range(H_B)]
    streams = [(bi, h) for bi in range(batch) for h in range(H_B)]
    rows = [slice(blk * CHUNK, (blk + 1) * CHUNK) for blk in range(blocks)]
    logg = {(blk, bi): _log_decay(alr_ref[bi, rows[blk], :], wup, balpha)
            for blk in range(blocks) for bi in range(batch)}
    units = [(q_ref[bi, rows[blk], ksl[h]], k_ref[bi, rows[blk], ksl[h]], v_ref[bi, rows[blk], vsl[h]],
              logg[(blk, bi)][:, ksl[h]]) for blk in range(blocks) for (bi, h) in streams]
    pre = _gla_state_free(units, tri, causal, CHUNK)
    for blk in range(blocks):
        mine = pre[blk * len(streams):(blk + 1) * len(streams)]
        outs = [_gla_read_state(p, s_scr[si]) for si, p in enumerate(mine)]
        nxt = [_gla_next_state(p, s_scr[si]) for si, p in enumerate(mine)]
        for si, (bi, h) in enumerate(streams):
            s_scr[si] = nxt[si]
            ob_ref[bi, rows[blk], vsl[h]] = _gla_epilogue(outs[si], rb_ref[bi, rows[blk], vsl[h]],
                                                          gnorm).astype(ob_ref.dtype)

    @pl.when(step == pl.num_programs(0) - 1)
    def _():
        st_ref[...] = s_scr[...]


def _gla_prompt(z, alr, wup, balpha, gnorm, batch, seq):
    blocks = 4
    rows = blocks * CHUNK
    z3 = z.reshape(batch, seq, Z_WIDTH)
    alr3 = alr.reshape(batch, seq, ALR_PAD)
    ob, st = pl.pallas_call(
        functools.partial(_gla_prompt_kernel, batch=batch, blocks=blocks),
        grid=(seq // rows,),
        in_specs=[
            pl.BlockSpec((batch, rows, KW_B), lambda j: (0, j, Z_QB // KW_B)),
            pl.BlockSpec((batch, rows, KW_B), lambda j: (0, j, Z_KB // KW_B)),
            pl.BlockSpec((batch, rows, VW_B), lambda j: (0, j, Z_VB // VW_B)),
            pl.BlockSpec((batch, rows, VW_B), lambda j: (0, j, Z_RB // VW_B)),
            pl.BlockSpec((batch, rows, ALR_PAD), lambda j: (0, j, 0)),
            pl.BlockSpec((ALR_PAD, KW_B), lambda j: (0, 0)),
            pl.BlockSpec((1, KW_B), lambda j: (0, 0)),
            pl.BlockSpec((1, DV_B), lambda j: (0, 0)),
        ],
        out_specs=[
            pl.BlockSpec((batch, rows, VW_B), lambda j: (0, j, 0)),
            pl.BlockSpec((batch * H_B, DK_B, DV_B), lambda j: (0, 0, 0)),
        ],
        out_shape=[
            jax.ShapeDtypeStruct((batch, seq, VW_B), BF16),
            jax.ShapeDtypeStruct((batch * H_B, DK_B, DV_B), F32),
        ],
        scratch_shapes=[pltpu.VMEM((batch * H_B, DK_B, DV_B), F32)],
        compiler_params=_cparams(("arbitrary",), VMEM_LIMIT_MID),
        name="gla_prompt",
    )(z3, z3, z3, z3, alr3, wup, balpha, gnorm)
    return ob.reshape(batch * seq, VW_B), st.reshape(batch, H_B, DK_B, DV_B)


def _gla_sample_kernel(q_ref, k_ref, v_ref, rb_ref, alr_ref, s0_ref, wup_ref, balpha_ref, gnorm_ref,
                       ob_ref, st_ref, *, seq):
    tri, causal = _tri_and_causal(seq)
    logg = _log_decay(alr_ref[...], wup_ref[...], balpha_ref[...])
    gnorm = gnorm_ref[...]
    ksl = [slice(h * DK_B, (h + 1) * DK_B) for h in range(H_B)]
    vsl = [slice(h * DV_B, (h + 1) * DV_B) for h in range(H_B)]
    pre = _gla_state_free([(q_ref[:, ksl[h]], k_ref[:, ksl[h]], v_ref[:, vsl[h]], logg[:, ksl[h]])
                           for h in range(H_B)], tri, causal, seq)
    outs = [_gla_read_state(pre[h], s0_ref[h]) for h in range(H_B)]
    nxt = [_gla_next_state(pre[h], s0_ref[h]) for h in range(H_B)]
    for h in range(H_B):
        st_ref[h] = nxt[h]
        ob_ref[:, vsl[h]] = _gla_epilogue(outs[h], rb_ref[:, vsl[h]], gnorm).astype(ob_ref.dtype)


def _gla_sample(z, alr, state, wup, balpha, gnorm, batch, seq):
    return pl.pallas_call(
        functools.partial(_gla_sample_kernel, seq=seq),
        grid=(batch,),
        in_specs=[
            pl.BlockSpec((seq, KW_B), lambda b: (b, Z_QB // KW_B)),
            pl.BlockSpec((seq, KW_B), lambda b: (b, Z_KB // KW_B)),
            pl.BlockSpec((seq, VW_B), lambda b: (b, Z_VB // VW_B)),
            pl.BlockSpec((seq, VW_B), lambda b: (b, Z_RB // VW_B)),
            pl.BlockSpec((seq, ALR_PAD), lambda b: (b, 0)),
            pl.BlockSpec((None, H_B, DK_B, DV_B), lambda b: (b, 0, 0, 0)),
            pl.BlockSpec((ALR_PAD, KW_B), lambda b: (0, 0)),
            pl.BlockSpec((1, KW_B), lambda b: (0, 0)),
            pl.BlockSpec((1, DV_B), lambda b: (0, 0)),
        ],
        out_specs=[
            pl.BlockSpec((seq, VW_B), lambda b: (b, 0)),
            pl.BlockSpec((None, H_B, DK_B, DV_B), lambda b: (b, 0, 0, 0)),
        ],
        out_shape=[
            jax.ShapeDtypeStruct((batch * seq, VW_B), BF16),
            jax.ShapeDtypeStruct((batch, H_B, DK_B, DV_B), F32),
        ],
        compiler_params=_cparams(("arbitrary",)),
        name="gla_sample",
    )(z, z, z, z, alr, state, wup, balpha, gnorm)


def _merge_kernel(oa_ref, ob_ref, gla_ref, glb_ref, x_ref, gt1_ref, sc2_ref, sh2_ref, g2_ref,
                  wa_ref, wb_ref, wo_ref, x1_ref, h2t_ref):
    ya = jnp.dot(oa_ref[...], wa_ref[...], preferred_element_type=F32)
    yb = jnp.dot(ob_ref[...], wb_ref[...], preferred_element_type=F32)
    merged = _sigmoid(gla_ref[...]) * ya + _sigmoid(glb_ref[...]) * yb
    x1 = x_ref[...] + gt1_ref[...] * jnp.dot(merged.astype(BF16), wo_ref[...], preferred_element_type=F32)
    x1_ref[...] = x1
    h2 = _rms(x1) * g2_ref[...]
    h2 = h2 * (1.0 + sc2_ref[...]) + sh2_ref[...]
    h2t_ref[...] = jnp.transpose(h2).astype(BF16)


def _merge(oa, ob, z, x, gt1, sc2, sh2, g2, wa, wb, wo, tm, per_token, rows_per_batch):
    n = x.shape[0]
    tpb = max(rows_per_batch // tm, 1)
    const = lambda shape: pl.BlockSpec(shape, lambda m: (0, 0))
    mod = _mod_spec(per_token, tm, tpb, 1)
    return pl.pallas_call(
        _merge_kernel,
        grid=(n // tm,),
        in_specs=[
            pl.BlockSpec((tm, W_A), lambda m: (m, 0)),
            pl.BlockSpec((tm, VW_B), lambda m: (m, 0)),
            pl.BlockSpec((tm, D_MODEL), lambda m: (m, Z_GL // D_MODEL)),
            pl.BlockSpec((tm, D_MODEL), lambda m: (m, Z_GL // D_MODEL + 1)),
            pl.BlockSpec((tm, D_MODEL), lambda m: (m, 0)),
            mod, mod, mod,
            const((1, D_MODEL)),
            const((W_A, D_MODEL)), const((VW_B, D_MODEL)), const((D_MODEL, D_MODEL)),
        ],
        out_specs=[
            pl.BlockSpec((tm, D_MODEL), lambda m: (m, 0)),
            pl.BlockSpec((D_MODEL, tm), lambda m: (0, m)),
        ],
        out_shape=[
            jax.ShapeDtypeStruct((n, D_MODEL), F32),
            jax.ShapeDtypeStruct((D_MODEL, n), BF16),
        ],
        compiler_params=_cparams(("arbitrary",), VMEM_LIMIT_BIG),
        name="merge",
    )(oa, ob, z, z, x, gt1, sc2, sh2, g2, wa, wb, wo)


def _qkfold_kernel(wq_ref, sk_ref, o_ref):
    o_ref[...] = lax.dot_general(wq_ref[...], sk_ref[...], (((1,), (1,)), ((), ())),
                                 preferred_element_type=F32)


def _qkfold(w_q, subkeys):
    half = N_KEYS
    return pl.pallas_call(
        _qkfold_kernel,
        grid=(2 * PEER_HEADS,),
        in_specs=[
            pl.BlockSpec((D_MODEL, half), lambda c: (0, c)),
            pl.BlockSpec((None, N_KEYS, half), lambda c: (c % 2, 0, 0)),
        ],
        out_specs=pl.BlockSpec((D_MODEL, N_KEYS), lambda c: (0, c)),
        out_shape=jax.ShapeDtypeStruct((D_MODEL, 2 * PEER_HEADS * N_KEYS), F32),
        compiler_params=_cparams(("arbitrary",)),
        name="qkfold",
    )(w_q, subkeys)


ROUTE_T = 256
LANES = 128
_SET_ROWS = N_KEYS * PEER_HEADS
_PAIRS = [(r, q) for r in range(PEER_TOPK) for q in range(PEER_TOPK) if (r + 1) * (q + 1) <= PEER_TOPK]


def _route_kernel(h2t_ref, wqk_ref, rank1_ref, b1_ref, cnt0_ref, a0_ref,
                  s_scr, sw_scr, rk_scr, val_scr, idx_scr, tmp_scr):
    halves = h2t_ref.shape[1] // LANES
    shp = (PEER_HEADS, LANES)
    s = jnp.dot(wqk_ref[...], h2t_ref[...], preferred_element_type=F32)
    for hf in range(halves):
        s_scr[hf] = s[:, hf * LANES:(hf + 1) * LANES]
    sw_scr[...] = s_scr[...]
    rk_scr[...] = jnp.full(rk_scr.shape, float(PEER_TOPK), F32)
    neg = jnp.full(shp, -jnp.inf, F32)

    def rows(p, k):
        return pl.ds(p * _SET_ROWS + k * PEER_HEADS, PEER_HEADS)

    def one_half(hf, carry0):
        def extract(r, carry):
            rf = jnp.full(shp, r, jnp.int32).astype(F32)
            for p in range(2):
                best, bidx = None, None
                for part in range(4):
                    m = neg
                    ix = jnp.zeros(shp, jnp.int32)
                    for k in range(part * 32, part * 32 + 32):
                        v = sw_scr[hf, rows(p, k), :]
                        gt = v > m
                        m = jnp.where(gt, v, m)
                        ix = jnp.where(gt, k, ix)
                    if best is None:
                        best, bidx = m, ix
                    else:
                        gt = m > best
                        best = jnp.where(gt, m, best)
                        bidx = jnp.where(gt, ix, bidx)
                val_scr[p, r] = best
                idx_scr[p, r] = bidx
                for k in range(N_KEYS):
                    hit = bidx == k
                    sw_scr[hf, rows(p, k), :] = jnp.where(hit, -jnp.inf, sw_scr[hf, rows(p, k), :])
                    rk_scr[hf, rows(p, k), :] = jnp.where(hit, rf, rk_scr[hf, rows(p, k), :])
            return carry

        lax.fori_loop(0, PEER_TOPK, extract, 0)

        v0 = [val_scr[0, r] for r in range(PEER_TOPK)]
        v1 = [val_scr[1, q] for q in range(PEER_TOPK)]
        cand = [v0[r] + v1[q] for (r, q) in _PAIRS]
        top = v0[0] + v1[0]
        cnt = [jnp.zeros(shp, F32) for _ in range(PEER_TOPK)]
        zsum = jnp.zeros(shp, F32)
        for _ in range(PEER_TOPK):
            m = neg
            ix = jnp.zeros(shp, jnp.int32)
            for c, cv in enumerate(cand):
                gt = cv > m
                m = jnp.where(gt, cv, m)
                ix = jnp.where(gt, c, ix)
            zsum = zsum + jnp.exp(m - top)
            for c, (r, q) in enumerate(_PAIRS):
                hit = ix == c
                cand[c] = jnp.where(hit, -jnp.inf, cand[c])
                cnt[r] = cnt[r] + jnp.where(hit, 1.0, 0.0)
        zinv = 1.0 / zsum

        idx0 = [idx_scr[0, r] for r in range(PEER_TOPK)]
        for k in range(N_KEYS):
            c0 = jnp.zeros(shp, F32)
            for r in range(PEER_TOPK):
                c0 = jnp.where(idx0[r] == k, cnt[r], c0)
            kr = pl.ds(k * PEER_HEADS, PEER_HEADS)
            tmp_scr[hf, 0, kr, :] = c0
            tmp_scr[hf, 1, kr, :] = jnp.exp(s_scr[hf, rows(0, k), :] - v0[0]) * zinv
            tmp_scr[hf, 2, kr, :] = jnp.exp(s_scr[hf, rows(1, k), :] - v1[0])
        return carry0

    lax.fori_loop(0, halves, one_half, 0)

    for hf in range(halves):
        ls = slice(hf * LANES, (hf + 1) * LANES)
        for h in range(PEER_HEADS):
            dst = slice(h * N_KEYS, (h + 1) * N_KEYS)
            src = pl.ds(h, N_KEYS, stride=PEER_HEADS)
            cnt0_ref[dst, ls] = _pack_pair(tmp_scr[hf, 0, src, :])
            a0_ref[dst, ls] = _pack_pair(tmp_scr[hf, 1, src, :])
            b1_ref[dst, ls] = tmp_scr[hf, 2, src, :].astype(BF16)
            rank1_ref[dst, ls] = rk_scr[hf, pl.ds(_SET_ROWS + h, N_KEYS, stride=PEER_HEADS), :].astype(BF16)


def _pack_pair(x):
    bits = pltpu.bitcast(x.astype(BF16).astype(F32), jnp.uint32)
    return bits | (bits >> 16)


def _route(h2t, wqk_t):
    n = h2t.shape[1]
    t = ROUTE_T
    rows = PEER_HEADS * N_KEYS
    ospec = pl.BlockSpec((rows, t), lambda m: (0, m))
    out_bf = jax.ShapeDtypeStruct((rows, n), BF16)
    out_pk = jax.ShapeDtypeStruct((rows, n), jnp.uint32)
    return pl.pallas_call(
        _route_kernel,
        grid=(n // t,),
        in_specs=[
            pl.BlockSpec((D_MODEL, t), lambda m: (0, m)),
            pl.BlockSpec((2 * rows, D_MODEL), lambda m: (0, 0)),
        ],
        out_specs=[ospec, ospec, ospec, ospec],
        out_shape=[out_bf, out_bf, out_pk, out_pk],
        scratch_shapes=[
            pltpu.VMEM((t // LANES, 2 * rows, LANES), F32),
            pltpu.VMEM((t // LANES, 2 * rows, LANES), F32),
            pltpu.VMEM((t // LANES, 2 * rows, LANES), F32),
            pltpu.VMEM((2, PEER_TOPK, PEER_HEADS, LANES), F32),
            pltpu.VMEM((2, PEER_TOPK, PEER_HEADS, LANES), jnp.int32),
            pltpu.VMEM((t // LANES, 3, rows, LANES), F32),
        ],
        compiler_params=_cparams(("arbitrary",), VMEM_LIMIT_MID),
        name="route",
    )(h2t, wqk_t)


PEER_T = 512
PEER_E = 512


def _gelu(x):
    return 0.5 * x * (1.0 + lax.erf(x * (2.0 ** -0.5)))


def _bcast_pair_rows(ref, row, t):
    word = jnp.broadcast_to(ref[pl.ds(row, 1), :], (8, t))
    pair = pltpu.bitcast(word, BF16)
    return jnp.broadcast_to(pair[None], (N_KEYS // 16, 16, t)).reshape(N_KEYS, t)


def _peer_coef(tile, u_ref, row0, h2t_ref, rank1_ref, b1_ref, cnt0_ref, a0_ref, coef_ref):
    t = h2t_ref.shape[1]
    act = jnp.dot(u_ref[row0:row0 + PEER_E, :], h2t_ref[...], preferred_element_type=F32)
    per = PEER_E // N_KEYS
    for ii in range(per):
        i = tile * per + ii
        w = None
        for h in range(PEER_HEADS):
            hs = slice(h * N_KEYS, (h + 1) * N_KEYS)
            cnt = _bcast_pair_rows(cnt0_ref, h * N_KEYS + i, t)
            a = _bcast_pair_rows(a0_ref, h * N_KEYS + i, t)
            term = jnp.where(rank1_ref[hs, :] < cnt, b1_ref[hs, :] * a, jnp.zeros((), BF16))
            w = term if w is None else w + term
        rs = slice(ii * N_KEYS, (ii + 1) * N_KEYS)
        coef_ref[rs, :] = _gelu(act[rs, :]).astype(BF16) * w


def _peer_kernel(h2t_ref, u_ref, vtp_ref, vtc_ref, rank1_ref, b1_ref, cnt0_ref, a0_ref, x1_ref, gt2_ref, gf_ref,
                 y_ref, acc, coef_a, coef_b):
    k = pl.program_id(1)
    last = pl.num_programs(1) - 1
    route = (rank1_ref, b1_ref, cnt0_ref, a0_ref)

    def two_tiles():
        _peer_coef(2 * k, u_ref, 0, h2t_ref, *route, coef_a)
        _peer_coef(2 * k + 1, u_ref, PEER_E, h2t_ref, *route, coef_b)
        return jnp.dot(vtc_ref[...], coef_a[...], preferred_element_type=F32)

    @pl.when(k == 0)
    def _():
        acc[...] = two_tiles()

    @pl.when(jnp.logical_and(k > 0, k < last))
    def _():
        acc[...] += jnp.dot(vtp_ref[...], coef_b[...], preferred_element_type=F32)
        acc[...] += two_tiles()

    @pl.when(k == last)
    def _():
        tot = acc[...] + jnp.dot(vtp_ref[...], coef_b[...], preferred_element_type=F32)
        x2 = x1_ref[...] + gt2_ref[...] * jnp.transpose(tot)
        y_ref[...] = _rms(x2) * gf_ref[...]


def _peer(h2t, u_bf, vt_bf, rank1, b1, cnt0, a0, x1, gt2, gfin, per_token, rows_per_batch):
    n = x1.shape[0]
    t = PEER_T
    tpb = max(rows_per_batch // t, 1)
    rows = PEER_HEADS * N_KEYS
    steps = N_EXPERTS // (2 * PEER_E)
    once = pl.Buffered(1)
    tok = lambda r: pl.BlockSpec((r, t), lambda m, k: (0, m), pipeline_mode=once)
    if per_token:
        gt_spec = pl.BlockSpec((t, D_MODEL), lambda m, k: (m, 0), pipeline_mode=once)
    else:
        gt_spec = pl.BlockSpec((None, 1, D_MODEL), lambda m, k: (m // tpb, 0, 0))
    return pl.pallas_call(
        _peer_kernel,
        grid=(n // t, steps + 1),
        in_specs=[
            pl.BlockSpec((D_MODEL, t), lambda m, k: (0, m)),
            pl.BlockSpec((2 * PEER_E, D_MODEL), lambda m, k: (jnp.minimum(k, steps - 1), 0)),
            pl.BlockSpec((D_MODEL, PEER_E), lambda m, k: (0, jnp.maximum(2 * k - 1, 0))),
            pl.BlockSpec((D_MODEL, PEER_E), lambda m, k: (0, 2 * jnp.minimum(k, steps - 1))),
            tok(rows), tok(rows), tok(rows), tok(rows),
            pl.BlockSpec((t, D_MODEL), lambda m, k: (m, 0), pipeline_mode=once),
            gt_spec,
            pl.BlockSpec((1, D_MODEL), lambda m, k: (0, 0)),
        ],
        out_specs=pl.BlockSpec((t, D_MODEL), lambda m, k: (m, 0)),
        out_shape=jax.ShapeDtypeStruct((n, D_MODEL), F32),
        scratch_shapes=[pltpu.VMEM((D_MODEL, t), F32), pltpu.VMEM((PEER_E, t), BF16), pltpu.VMEM((PEER_E, t), BF16)],
        compiler_params=_cparams(("parallel", "arbitrary"), VMEM_LIMIT_BIG),
        name="peer",
    )(h2t, u_bf, vt_bf, vt_bf, rank1, b1, cnt0, a0, x1, gt2, gfin)


def _rel_bias_tile(rel_bias, rows, cols, offset):
    rb = rel_bias.astype(F32)
    heads = rb.shape[0]
    rel_max = offset + rows - 1
    rel_min = offset - (cols - 1)
    lo, hi = max(rel_min, -REL_CLIP), min(rel_max, REL_CLIP)
    parts = []
    if rel_min < -REL_CLIP:
        parts.append(jnp.broadcast_to(rb[:, :1], (heads, -REL_CLIP - rel_min)))
    parts.append(rb[:, lo + REL_CLIP:hi + REL_CLIP + 1])
    if rel_max > REL_CLIP:
        parts.append(jnp.broadcast_to(rb[:, -1:], (heads, rel_max - REL_CLIP)))
    g = jnp.flip(jnp.concatenate(parts, axis=1), axis=1)
    period = rows + cols
    gp = jnp.pad(g, ((0, 0), (0, 1)))
    shifted = jnp.tile(gp, (1, rows))[:, :rows * (period - 1)].reshape(heads, rows, period - 1)
    return shifted[:, :, rows - 1:rows - 1 + cols]


def _prompt_bias(rel_bias):
    a = jnp.arange(Q_SUB, dtype=jnp.int32)[:, None]
    c = jnp.arange(K_WIN, dtype=jnp.int32)[None, :]
    cq = a // CHUNK
    ck = c // CHUNK
    ok = (ck >= cq) & (ck <= cq + N_LEFT_CHUNKS)
    return jnp.where(ok[None], _rel_bias_tile(rel_bias, Q_SUB, K_WIN, BAND_LEFT), NEG_INF)


def _sample_bias(rel_bias, seq, lc):
    return _rel_bias_tile(rel_bias, seq, lc, lc), _rel_bias_tile(rel_bias, seq, seq, 0)


def _layer(x, mod, cache_k, cache_v, state, is_sample, wts):
    (g1, w_main, w_alr, rel_bias, wup, balpha, gnorm, wa, wb, wo, g2, wqk_t, u_bf, vt_bf, gfin) = wts
    batch, seq, _ = x.shape
    n = batch * seq
    xf = x.reshape(n, D_MODEL)
    parts = [mod[:, i * D_MODEL:(i + 1) * D_MODEL] for i in range(6)]
    if is_sample:
        sh1, sc1, gt1, sh2, sc2, gt2 = [jnp.repeat(p, seq, axis=0) for p in parts]
        tm_in, tm_mg = n, 256
    else:
        sh1, sc1, gt1, sh2, sc2, gt2 = [p.reshape(batch, 1, D_MODEL) for p in parts]
        tm_in, tm_mg = 1024, 256

    z, alr = _inproj(xf, sh1, sc1, g1, w_main, w_alr, tm_in, is_sample, seq)

    if is_sample:
        lc = cache_k.shape[1]
        bias_c, bias_n = _sample_bias(rel_bias, seq, lc)
        oa = _attn_sample(z, cache_k.reshape(batch, lc * H_A, DH_A), cache_v.reshape(batch, lc * H_A, DH_A),
                          bias_c, bias_n, batch, seq)
        ob, s_new = _gla_sample(z, alr, state, wup, balpha, gnorm, batch, seq)
        k_rows = z[:, Z_KA:Z_KA + W_A].reshape(batch, seq, H_A, DH_A)
        v_rows = z[:, Z_VA:Z_VA + W_A].reshape(batch, seq, H_A, DH_A)
    else:
        oa = _attn_prompt(z, _prompt_bias(rel_bias), batch, seq)
        ob, s_new = _gla_prompt(z, alr, wup, balpha, gnorm, batch, seq)
        keep = min(BAND_LEFT, seq)
        zk = z.reshape(batch, seq, Z_WIDTH)[:, seq - keep:]
        k_rows = zk[:, :, Z_KA:Z_KA + W_A].reshape(batch, keep, H_A, DH_A)
        v_rows = zk[:, :, Z_VA:Z_VA + W_A].reshape(batch, keep, H_A, DH_A)

    x1, h2t = _merge(oa, ob, z, xf, gt1, sc2, sh2, g2, wa, wb, wo, tm_mg, is_sample, seq)
    rank1, b1, cnt0, a0 = _route(h2t, wqk_t)
    y = _peer(h2t, u_bf, vt_bf, rank1, b1, cnt0, a0, x1, gt2, gfin, is_sample, seq)
    return y.reshape(batch, seq, D_MODEL), k_rows, v_rows, s_new


def kernel(x_prompt, x_sample, cache_a_k, cache_a_v, state_gla, c_prompt, c_sample, w_ada, b_ada, g_norm1, w_in, rel_bias, w_alpha_up, b_alpha, g_gla_norm, w_branch_a, w_branch_b, w_out, g_norm2, w_peer_q, peer_subkeys, peer_u, peer_v, g_final):
    depth = w_in.shape[0]
    assert depth == 1, "the final rmsnorm is fused into the PEER kernel of the only layer"
    nbp = c_prompt.shape[0]
    l = 0
    nbs = c_sample.shape[0]
    pad = -(nbp + nbs) % 16
    c_all = jnp.concatenate([c_prompt, c_sample, jnp.zeros((pad, D_MODEL), F32)], axis=0)
    def only(w):
        return w.reshape(w.shape[1:])

    mod = _ada(c_all, only(w_ada), b_ada[l])

    alr_lo = Z_GL
    w_in0 = only(w_in)
    w_main = jnp.concatenate([w_in0[:, :alr_lo].astype(BF16), w_in0[:, alr_lo + ALPHA_RANK:].astype(BF16)], axis=1)
    w_alr = jnp.pad(w_in0[:, alr_lo:alr_lo + ALPHA_RANK], ((0, 0), (0, ALR_PAD - ALPHA_RANK))).astype(BF16)
    wup = jnp.pad(w_alpha_up[l], ((0, ALR_PAD - ALPHA_RANK), (0, 0))).astype(BF16)
    wqk = _qkfold(only(w_peer_q).astype(BF16), peer_subkeys[l].astype(BF16))
    wqk_t = wqk.reshape(D_MODEL, PEER_HEADS, 2, N_KEYS).transpose(2, 3, 1, 0).reshape(2 * PEER_HEADS * N_KEYS, D_MODEL)
    wts = (
        g_norm1[l].reshape(1, D_MODEL), w_main, w_alr, rel_bias[l], wup,
        b_alpha[l].reshape(1, KW_B), g_gla_norm[l].reshape(1, DV_B),
        only(w_branch_a).astype(BF16), only(w_branch_b).astype(BF16), only(w_out).astype(BF16),
        g_norm2[l].reshape(1, D_MODEL), wqk_t.astype(BF16),
        only(peer_u).astype(BF16), jnp.transpose(only(peer_v).astype(BF16)), g_final.reshape(1, D_MODEL),
    )
    yp, kp, vp, sp = _layer(x_prompt, mod[:nbp], None, None, None, False, wts)
    ys, ks, vs, ss = _layer(x_sample, mod[nbp:nbp + nbs], only(cache_a_k), only(cache_a_v), only(state_gla), True, wts)
    return (yp, ys, kp[None], vp[None], sp[None], ks[None], vs[None], ss[None])
```

```python
import functools
import math

import jax
import jax.numpy as jnp
from jax import lax
from jax.experimental import pallas as pl
from jax.experimental.pallas import tpu as pltpu

F32 = jnp.float32
BF16 = jnp.bfloat16

D_MODEL = 2048
CHUNK = 64
N_LEFT_CHUNKS = 8
BAND_LEFT = N_LEFT_CHUNKS * CHUNK
H_A = 8
DH_A = 128
W_A = H_A * DH_A
REL_CLIP = 128
H_B = 4
DK_B = 128
DV_B = 256
KW_B = H_B * DK_B
VW_B = H_B * DV_B
ALPHA_RANK = 16
GATE_TAU = 16.0
PEER_HEADS = 8
N_KEYS = 128
N_EXPERTS = N_KEYS * N_KEYS
PEER_TOPK = 16
EPS = 1e-6
NEG_INF = -1e30

Z_QA, Z_KA, Z_VA = 0, W_A, 2 * W_A
Z_QB = 3 * W_A
Z_KB = Z_QB + KW_B
Z_VB = Z_KB + KW_B
Z_RB = Z_VB + VW_B
Z_GL = Z_RB + VW_B
Z_WIDTH = Z_GL + 2 * D_MODEL
ALR_PAD = 128

VMEM_LIMIT_BIG = 56 * 1024 * 1024
VMEM_LIMIT_MID = 40 * 1024 * 1024

Q_TILE = 512
Q_SUB = 128
K_WIN = Q_SUB + BAND_LEFT


def _cparams(sem, vmem=None):
    return pltpu.CompilerParams(dimension_semantics=sem, vmem_limit_bytes=vmem)


def _rms(xf):
    return xf * lax.rsqrt(jnp.mean(xf * xf, axis=-1, keepdims=True) + EPS)


def _sigmoid(x):
    return 1.0 / (1.0 + jnp.exp(-x))


def _ada_kernel(c_ref, w_ref, b_ref, o_ref):
    c = c_ref[...]
    a = (c * _sigmoid(c)).astype(BF16)
    o_ref[...] = jnp.dot(a, w_ref[...].astype(BF16), preferred_element_type=F32) + b_ref[...]


def _ada(c_all, w_ada, b_ada):
    nb = c_all.shape[0]
    n_out = w_ada.shape[1]
    tn = 1024
    return pl.pallas_call(
        _ada_kernel,
        grid=(n_out // tn,),
        in_specs=[
            pl.BlockSpec((nb, D_MODEL), lambda j: (0, 0)),
            pl.BlockSpec((D_MODEL, tn), lambda j: (0, j)),
            pl.BlockSpec((1, tn), lambda j: (0, j)),
        ],
        out_specs=pl.BlockSpec((nb, tn), lambda j: (0, j)),
        out_shape=jax.ShapeDtypeStruct((nb, n_out), F32),
        compiler_params=_cparams(("arbitrary",), VMEM_LIMIT_MID),
        name="ada",
    )(c_all, w_ada, b_ada.reshape(1, n_out))


def _mod_spec(per_token, tm, tiles_per_batch, grid_rank):
    if per_token:
        if grid_rank == 1:
            return pl.BlockSpec((tm, D_MODEL), lambda m: (m, 0))
        return pl.BlockSpec((tm, D_MODEL), lambda m, n: (m, 0))
    if grid_rank == 1:
        return pl.BlockSpec((None, 1, D_MODEL), lambda m: (m // tiles_per_batch, 0, 0))
    return pl.BlockSpec((None, 1, D_MODEL), lambda m, n: (m // tiles_per_batch, 0, 0))


def _inproj_kernel(x_ref, sh_ref, sc_ref, g_ref, w_ref, walr_ref, z_ref, alr_ref, h_scr):
    @pl.when(pl.program_id(1) == 0)
    def _():
        h = _rms(x_ref[...]) * g_ref[...]
        h = h * (1.0 + sc_ref[...]) + sh_ref[...]
        hb = h.astype(BF16)
        h_scr[...] = hb
        alr_ref[...] = jnp.dot(hb, walr_ref[...], preferred_element_type=F32)

    z_ref[...] = jnp.dot(h_scr[...], w_ref[...], preferred_element_type=F32)


def _inproj(x, sh, sc, g, w_main, w_alr, tm, per_token, rows_per_batch):
    n = x.shape[0]
    tn = 1024
    tpb = max(rows_per_batch // tm, 1)
    return pl.pallas_call(
        _inproj_kernel,
        grid=(n // tm, Z_WIDTH // tn),
        in_specs=[
            pl.BlockSpec((tm, D_MODEL), lambda m, j: (m, 0)),
            _mod_spec(per_token, tm, tpb, 2),
            _mod_spec(per_token, tm, tpb, 2),
            pl.BlockSpec((1, D_MODEL), lambda m, j: (0, 0)),
            pl.BlockSpec((D_MODEL, tn), lambda m, j: (0, j)),
            pl.BlockSpec((D_MODEL, ALR_PAD), lambda m, j: (0, 0)),
        ],
        out_specs=[
            pl.BlockSpec((tm, tn), lambda m, j: (m, j)),
            pl.BlockSpec((tm, ALR_PAD), lambda m, j: (m, 0)),
        ],
        out_shape=[
            jax.ShapeDtypeStruct((n, Z_WIDTH), F32),
            jax.ShapeDtypeStruct((n, ALR_PAD), F32),
        ],
        scratch_shapes=[pltpu.VMEM((tm, D_MODEL), BF16)],
        compiler_params=_cparams(("parallel", "arbitrary"), VMEM_LIMIT_BIG),
        name="inproj",
    )(x, sh, sc, g, w_main, w_alr)


def _attn_prompt_kernel(q_ref, kp_ref, kc_ref, vp_ref, vc_ref, bias_ref, o_ref, kw, vw):
    first = pl.program_id(2) == 0
    kw[0:Q_TILE, :] = kp_ref[...].astype(BF16)
    kw[Q_TILE:2 * Q_TILE, :] = kc_ref[...].astype(BF16)
    vw[0:Q_TILE, :] = vp_ref[...].astype(BF16)
    vw[Q_TILE:2 * Q_TILE, :] = vc_ref[...].astype(BF16)
    bias = bias_ref[...]
    scale = DH_A ** -0.5
    subs = range(Q_TILE // Q_SUB)
    scores = [lax.dot_general(q_ref[r * Q_SUB:(r + 1) * Q_SUB, :].astype(BF16), kw[r * Q_SUB:r * Q_SUB + K_WIN, :],
                              (((1,), (1,)), ((), ())), preferred_element_type=F32) for r in subs]
    probs, sums = [], []
    for r in subs:
        s = scores[r] * scale + bias
        col = lax.broadcasted_iota(jnp.int32, (Q_SUB, K_WIN), 1) + r * Q_SUB
        s = jnp.where(jnp.logical_and(first, col < Q_TILE), NEG_INF, s)
        m = jnp.max(s, axis=-1, keepdims=True)
        p = jnp.exp(s - m)
        sums.append(jnp.sum(p, axis=-1, keepdims=True))
        probs.append(p.astype(BF16))
    outs = [jnp.dot(probs[r], vw[r * Q_SUB:r * Q_SUB + K_WIN, :], preferred_element_type=F32) for r in subs]
    for r in subs:
        o_ref[r * Q_SUB:(r + 1) * Q_SUB, :] = (outs[r] / sums[r]).astype(o_ref.dtype)


def _attn_prompt(z, bias, batch, seq):
    n = batch * seq
    tiles = seq // Q_TILE

    def cur(col):
        return pl.BlockSpec((Q_TILE, DH_A), lambda h, b, i: (b * tiles + i, col + h))

    def prev(col):
        return pl.BlockSpec((Q_TILE, DH_A),
                            lambda h, b, i: (b * tiles + jnp.maximum(i - 1, 0), col + h))

    kcol, vcol = Z_KA // DH_A, Z_VA // DH_A
    return pl.pallas_call(
        _attn_prompt_kernel,
        grid=(H_A, batch, tiles),
        in_specs=[cur(0), prev(kcol), cur(kcol), prev(vcol), cur(vcol),
                  pl.BlockSpec((None, Q_SUB, K_WIN), lambda h, b, i: (h, 0, 0))],
        out_specs=pl.BlockSpec((Q_TILE, DH_A), lambda h, b, i: (b * tiles + i, h)),
        out_shape=jax.ShapeDtypeStruct((n, W_A), BF16),
        scratch_shapes=[pltpu.VMEM((2 * Q_TILE, DH_A), BF16), pltpu.VMEM((2 * Q_TILE, DH_A), BF16)],
        compiler_params=_cparams(("arbitrary", "arbitrary", "arbitrary")),
        name="attn_prompt",
    )(z, z, z, z, z, bias)


def _attn_sample_kernel(q_ref, kn_ref, vn_ref, ck_ref, cv_ref, bc_ref, bn_ref, o_ref):
    scale = DH_A ** -0.5
    nt = (((1,), (1,)), ((), ()))
    lc = ck_ref.shape[0] // H_A
    heads = range(H_A)
    cols = [slice(h * DH_A, (h + 1) * DH_A) for h in heads]
    crow = [pl.ds(h, lc, stride=H_A) for h in heads]
    qs = [q_ref[:, cols[h]].astype(BF16) for h in heads]
    s_c = [lax.dot_general(qs[h], ck_ref[crow[h], :].astype(BF16), nt, preferred_element_type=F32) for h in heads]
    s_n = [lax.dot_general(qs[h], kn_ref[:, cols[h]].astype(BF16), nt, preferred_element_type=F32) for h in heads]
    p_c, p_n, sums = [], [], []
    for h in heads:
        sc = s_c[h] * scale + bc_ref[h]
        sn = s_n[h] * scale + bn_ref[h]
        m = jnp.maximum(jnp.max(sc, axis=-1, keepdims=True), jnp.max(sn, axis=-1, keepdims=True))
        pc = jnp.exp(sc - m)
        pn = jnp.exp(sn - m)
        sums.append(jnp.sum(pc, axis=-1, keepdims=True) + jnp.sum(pn, axis=-1, keepdims=True))
        p_c.append(pc.astype(BF16))
        p_n.append(pn.astype(BF16))
    outs = [jnp.dot(p_c[h], cv_ref[crow[h], :].astype(BF16), preferred_element_type=F32)
            + jnp.dot(p_n[h], vn_ref[:, cols[h]].astype(BF16), preferred_element_type=F32) for h in heads]
    for h in heads:
        o_ref[:, cols[h]] = (outs[h] / sums[h]).astype(o_ref.dtype)


def _attn_sample(z, cache_k, cache_v, bias_c, bias_n, batch, seq):
    lc = cache_k.shape[1] // H_A
    return pl.pallas_call(
        _attn_sample_kernel,
        grid=(batch,),
        in_specs=[
            pl.BlockSpec((seq, W_A), lambda b: (b, Z_QA // W_A)),
            pl.BlockSpec((seq, W_A), lambda b: (b, Z_KA // W_A)),
            pl.BlockSpec((seq, W_A), lambda b: (b, Z_VA // W_A)),
            pl.BlockSpec((None, lc * H_A, DH_A), lambda b: (b, 0, 0)),
            pl.BlockSpec((None, lc * H_A, DH_A), lambda b: (b, 0, 0)),
            pl.BlockSpec((H_A, seq, lc), lambda b: (0, 0, 0)),
            pl.BlockSpec((H_A, seq, seq), lambda b: (0, 0, 0)),
        ],
        out_specs=pl.BlockSpec((seq, W_A), lambda b: (b, 0)),
        out_shape=jax.ShapeDtypeStruct((batch * seq, W_A), BF16),
        compiler_params=_cparams(("arbitrary",), VMEM_LIMIT_MID),
        name="attn_sample",
    )(z, z, z, cache_k, cache_v, bias_c, bias_n)


def _split_bf16(x):
    hi = x.astype(BF16)
    lo = (x - hi.astype(F32)).astype(BF16)
    return hi, lo


_TN = (((0,), (0,)), ((), ()))
_NT = (((1,), (1,)), ((), ()))


def _gla_state_free(units, tri, causal, c):
    scale = DK_B ** -0.5
    ones = jnp.ones((c, DK_B), BF16)
    split = [_split_bf16(g) for (_, _, _, g) in units]
    cum = [jnp.dot(tri, hi, preferred_element_type=F32) + jnp.dot(tri, lo, preferred_element_type=F32)
           for (hi, lo) in split]
    tot = [lax.dot_general(hi, ones, _TN, preferred_element_type=F32)
           + lax.dot_general(lo, ones, _TN, preferred_element_type=F32) for (hi, lo) in split]
    qd, kd, vb, att = [], [], [], []
    for (q, k, v, _), b in zip(units, cum):
        mid = b[c // 2:c // 2 + 1, :]
        blast = b[c - 1:c, :]
        qs = q * scale
        ks = k * scale
        att.append(lax.dot_general((qs * jnp.exp(b - mid)).astype(BF16), (ks * jnp.exp(mid - b)).astype(BF16),
                                   _NT, preferred_element_type=F32))
        qd.append((qs * jnp.exp(b)).astype(BF16))
        kd.append((ks * jnp.exp(blast - b)).astype(BF16))
        vb.append(v.astype(BF16))
    intra = [jnp.dot(jnp.where(causal, a, 0.0).astype(BF16), v, preferred_element_type=F32)
             for a, v in zip(att, vb)]
    out = []
    for i in range(len(units)):
        dec = jnp.exp(tot[i])
        out.append((intra[i], qd[i], kd[i], vb[i], jnp.concatenate([dec, dec], axis=1)))
    return out


def _gla_read_state(pre, s_prev):
    o_intra, qd, _, _, _ = pre
    return o_intra + jnp.dot(qd, s_prev.astype(BF16), preferred_element_type=F32)


def _gla_next_state(pre, s_prev):
    _, _, kd, vb, dec = pre
    return dec * s_prev + lax.dot_general(kd, vb, _TN, preferred_element_type=F32)


def _log_decay(alr, wup, balpha):
    x = jnp.dot(alr.astype(BF16), wup, preferred_element_type=F32) + balpha
    return (jnp.minimum(x, 0.0) - jnp.log1p(jnp.exp(-jnp.abs(x)))) * (1.0 / GATE_TAU)


def _gla_epilogue(o, rb, gnorm):
    y = _rms(o) * gnorm
    return y * (rb * _sigmoid(rb))


def _tri_and_causal(c):
    row = lax.broadcasted_iota(jnp.int32, (c, c), 0)
    col = lax.broadcasted_iota(jnp.int32, (c, c), 1)
    causal = col <= row
    return jnp.where(causal, 1.0, 0.0).astype(BF16), causal


def _gla_prompt_kernel(q_ref, k_ref, v_ref, rb_ref, alr_ref, wup_ref, balpha_ref, gnorm_ref,
                       ob_ref, st_ref, s_scr, *, batch, blocks):
    step = pl.program_id(0)

    @pl.when(step == 0)
    def _():
        s_scr[...] = jnp.zeros_like(s_scr)

    tri, causal = _tri_and_causal(CHUNK)
    wup = wup_ref[...]
    balpha = balpha_ref[...]
    gnorm = gnorm_ref[...]
    ksl = [slice(h * DK_B, (h + 1) * DK_B) for h in range(H_B)]
    vsl = [slice(h * DV_B, (h + 1) * DV_B) for h in range(H_B)]
    streams = [(bi, h) for bi in range(batch) for h in range(H_B)]
    rows = [slice(blk * CHUNK, (blk + 1) * CHUNK) for blk in range(blocks)]
    logg = {(blk, bi): _log_decay(alr_ref[bi, rows[blk], :], wup, balpha)
            for blk in range(blocks) for bi in range(batch)}
    units = [(q_ref[bi, rows[blk], ksl[h]], k_ref[bi, rows[blk], ksl[h]], v_ref[bi, rows[blk], vsl[h]],
              logg[(blk, bi)][:, ksl[h]]) for blk in range(blocks) for (bi, h) in streams]
    pre = _gla_state_free(units, tri, causal, CHUNK)
    for blk in range(blocks):
        mine = pre[blk * len(streams):(blk + 1) * len(streams)]
        outs = [_gla_read_state(p, s_scr[si]) for si, p in enumerate(mine)]
        nxt = [_gla_next_state(p, s_scr[si]) for si, p in enumerate(mine)]
        for si, (bi, h) in enumerate(streams):
            s_scr[si] = nxt[si]
            ob_ref[bi, rows[blk], vsl[h]] = _gla_epilogue(outs[si], rb_ref[bi, rows[blk], vsl[h]],
                                                          gnorm).astype(ob_ref.dtype)

    @pl.when(step == pl.num_programs(0) - 1)
    def _():
        st_ref[...] = s_scr[...]


def _gla_prompt(z, alr, wup, balpha, gnorm, batch, seq):
    blocks = 4
    rows = blocks * CHUNK
    z3 = z.reshape(batch, seq, Z_WIDTH)
    alr3 = alr.reshape(batch, seq, ALR_PAD)
    ob, st = pl.pallas_call(
        functools.partial(_gla_prompt_kernel, batch=batch, blocks=blocks),
        grid=(seq // rows,),
        in_specs=[
            pl.BlockSpec((batch, rows, KW_B), lambda j: (0, j, Z_QB // KW_B)),
            pl.BlockSpec((batch, rows, KW_B), lambda j: (0, j, Z_KB // KW_B)),
            pl.BlockSpec((batch, rows, VW_B), lambda j: (0, j, Z_VB // VW_B)),
            pl.BlockSpec((batch, rows, VW_B), lambda j: (0, j, Z_RB // VW_B)),
            pl.BlockSpec((batch, rows, ALR_PAD), lambda j: (0, j, 0)),
            pl.BlockSpec((ALR_PAD, KW_B), lambda j: (0, 0)),
            pl.BlockSpec((1, KW_B), lambda j: (0, 0)),
            pl.BlockSpec((1, DV_B), lambda j: (0, 0)),
        ],
        out_specs=[
            pl.BlockSpec((batch, rows, VW_B), lambda j: (0, j, 0)),
            pl.BlockSpec((batch * H_B, DK_B, DV_B), lambda j: (0, 0, 0)),
        ],
        out_shape=[
            jax.ShapeDtypeStruct((batch, seq, VW_B), BF16),
            jax.ShapeDtypeStruct((batch * H_B, DK_B, DV_B), F32),
        ],
        scratch_shapes=[pltpu.VMEM((batch * H_B, DK_B, DV_B), F32)],
        compiler_params=_cparams(("arbitrary",), VMEM_LIMIT_MID),
        name="gla_prompt",
    )(z3, z3, z3, z3, alr3, wup, balpha, gnorm)
    return ob.reshape(batch * seq, VW_B), st.reshape(batch, H_B, DK_B, DV_B)


def _gla_sample_kernel(q_ref, k_ref, v_ref, rb_ref, alr_ref, s0_ref, wup_ref, balpha_ref, gnorm_ref,
                       ob_ref, st_ref, *, seq):
    tri, causal = _tri_and_causal(seq)
    logg = _log_decay(alr_ref[...], wup_ref[...], balpha_ref[...])
    gnorm = gnorm_ref[...]
    ksl = [slice(h * DK_B, (h + 1) * DK_B) for h in range(H_B)]
    vsl = [slice(h * DV_B, (h + 1) * DV_B) for h in range(H_B)]
    pre = _gla_state_free([(q_ref[:, ksl[h]], k_ref[:, ksl[h]], v_ref[:, vsl[h]], logg[:, ksl[h]])
                           for h in range(H_B)], tri, causal, seq)
    outs = [_gla_read_state(pre[h], s0_ref[h]) for h in range(H_B)]
    nxt = [_gla_next_state(pre[h], s0_ref[h]) for h in range(H_B)]
    for h in range(H_B):
        st_ref[h] = nxt[h]
        ob_ref[:, vsl[h]] = _gla_epilogue(outs[h], rb_ref[:, vsl[h]], gnorm).astype(ob_ref.dtype)


def _gla_sample(z, alr, state, wup, balpha, gnorm, batch, seq):
    return pl.pallas_call(
        functools.partial(_gla_sample_kernel, seq=seq),
        grid=(batch,),
        in_specs=[
            pl.BlockSpec((seq, KW_B), lambda b: (b, Z_QB // KW_B)),
            pl.BlockSpec((seq, KW_B), lambda b: (b, Z_KB // KW_B)),
            pl.BlockSpec((seq, VW_B), lambda b: (b, Z_VB // VW_B)),
            pl.BlockSpec((seq, VW_B), lambda b: (b, Z_RB // VW_B)),
            pl.BlockSpec((seq, ALR_PAD), lambda b: (b, 0)),
            pl.BlockSpec((None, H_B, DK_B, DV_B), lambda b: (b, 0, 0, 0)),
            pl.BlockSpec((ALR_PAD, KW_B), lambda b: (0, 0)),
            pl.BlockSpec((1, KW_B), lambda b: (0, 0)),
            pl.BlockSpec((1, DV_B), lambda b: (0, 0)),
        ],
        out_specs=[
            pl.BlockSpec((seq, VW_B), lambda b: (b, 0)),
            pl.BlockSpec((None, H_B, DK_B, DV_B), lambda b: (b, 0, 0, 0)),
        ],
        out_shape=[
            jax.ShapeDtypeStruct((batch * seq, VW_B), BF16),
            jax.ShapeDtypeStruct((batch, H_B, DK_B, DV_B), F32),
        ],
        compiler_params=_cparams(("arbitrary",)),
        name="gla_sample",
    )(z, z, z, z, alr, state, wup, balpha, gnorm)


def _merge_kernel(oa_ref, ob_ref, gla_ref, glb_ref, x_ref, gt1_ref, sc2_ref, sh2_ref, g2_ref,
                  wa_ref, wb_ref, wo_ref, x1_ref, h2t_ref):
    ya = jnp.dot(oa_ref[...], wa_ref[...], preferred_element_type=F32)
    yb = jnp.dot(ob_ref[...], wb_ref[...], preferred_element_type=F32)
    merged = _sigmoid(gla_ref[...]) * ya + _sigmoid(glb_ref[...]) * yb
    x1 = x_ref[...] + gt1_ref[...] * jnp.dot(merged.astype(BF16), wo_ref[...], preferred_element_type=F32)
    x1_ref[...] = x1
    h2 = _rms(x1) * g2_ref[...]
    h2 = h2 * (1.0 + sc2_ref[...]) + sh2_ref[...]
    h2t_ref[...] = jnp.transpose(h2).astype(BF16)


def _merge(oa, ob, z, x, gt1, sc2, sh2, g2, wa, wb, wo, tm, per_token, rows_per_batch):
    n = x.shape[0]
    tpb = max(rows_per_batch // tm, 1)
    const = lambda shape: pl.BlockSpec(shape, lambda m: (0, 0))
    mod = _mod_spec(per_token, tm, tpb, 1)
    return pl.pallas_call(
        _merge_kernel,
        grid=(n // tm,),
        in_specs=[
            pl.BlockSpec((tm, W_A), lambda m: (m, 0)),
            pl.BlockSpec((tm, VW_B), lambda m: (m, 0)),
            pl.BlockSpec((tm, D_MODEL), lambda m: (m, Z_GL // D_MODEL)),
            pl.BlockSpec((tm, D_MODEL), lambda m: (m, Z_GL // D_MODEL + 1)),
            pl.BlockSpec((tm, D_MODEL), lambda m: (m, 0)),
            mod, mod, mod,
            const((1, D_MODEL)),
            const((W_A, D_MODEL)), const((VW_B, D_MODEL)), const((D_MODEL, D_MODEL)),
        ],
        out_specs=[
            pl.BlockSpec((tm, D_MODEL), lambda m: (m, 0)),
            pl.BlockSpec((D_MODEL, tm), lambda m: (0, m)),
        ],
        out_shape=[
            jax.ShapeDtypeStruct((n, D_MODEL), F32),
            jax.ShapeDtypeStruct((D_MODEL, n), BF16),
        ],
        compiler_params=_cparams(("arbitrary",), VMEM_LIMIT_BIG),
        name="merge",
    )(oa, ob, z, z, x, gt1, sc2, sh2, g2, wa, wb, wo)


def _qkfold_kernel(wq_ref, sk_ref, o_ref):
    o_ref[...] = lax.dot_general(wq_ref[...], sk_ref[...], (((1,), (1,)), ((), ())),
                                 preferred_element_type=F32)


def _qkfold(w_q, subkeys):
    half = N_KEYS
    return pl.pallas_call(
        _qkfold_kernel,
        grid=(2 * PEER_HEADS,),
        in_specs=[
            pl.BlockSpec((D_MODEL, half), lambda c: (0, c)),
            pl.BlockSpec((None, N_KEYS, half), lambda c: (c % 2, 0, 0)),
        ],
        out_specs=pl.BlockSpec((D_MODEL, N_KEYS), lambda c: (0, c)),
        out_shape=jax.ShapeDtypeStruct((D_MODEL, 2 * PEER_HEADS * N_KEYS), F32),
        compiler_params=_cparams(("arbitrary",)),
        name="qkfold",
    )(w_q, subkeys)


ROUTE_T = 256
LANES = 128
_SET_ROWS = N_KEYS * PEER_HEADS
_PAIRS = [(r, q) for r in range(PEER_TOPK) for q in range(PEER_TOPK) if (r + 1) * (q + 1) <= PEER_TOPK]


def _sort_desc(x):
    x = list(x)
    n = len(x)
    k = 2
    while k <= n:
        j = k // 2
        while j >= 1:
            for i in range(n):
                l = i ^ j
                if l > i:
                    hi, lo = jnp.maximum(x[i], x[l]), jnp.minimum(x[i], x[l])
                    x[i], x[l] = (hi, lo) if (i & k) == 0 else (lo, hi)
            j //= 2
        k *= 2
    return x


def _merge_top(a, b):
    n = len(a)
    x = [jnp.maximum(a[i], b[n - 1 - i]) for i in range(n)]
    j = n // 2
    while j >= 1:
        for i in range(n):
            l = i ^ j
            if l > i:
                x[i], x[l] = jnp.maximum(x[i], x[l]), jnp.minimum(x[i], x[l])
        j //= 2
    return x


def _top_sorted(vals, top):
    groups = [_sort_desc(vals[g:g + top]) for g in range(0, len(vals), top)]
    while len(groups) > 1:
        groups = [_merge_top(groups[g], groups[g + 1]) for g in range(0, len(groups), 2)]
    return groups[0]


def _route_fast(s0, s1):
    shp = s0[0].shape
    one = jnp.ones(shp, F32)
    zero = jnp.zeros(shp, F32)
    top0 = _top_sorted(s0, PEER_TOPK)
    top1 = _top_sorted(s1, PEER_TOPK)
    cand = [top0[r] + top1[q] for (r, q) in _PAIRS]
    pad = [jnp.full(shp, -jnp.inf, F32)] * (-len(cand) % PEER_TOPK)
    topc = _top_sorted(cand + pad, PEER_TOPK)
    tau = topc[PEER_TOPK - 1]
    cnt = [zero] * PEER_TOPK
    for c, (r, q) in enumerate(_PAIRS):
        cnt[r] = cnt[r] + jnp.where(cand[c] >= tau, one, zero)
    picked = cnt[0]
    for r in range(1, PEER_TOPK):
        picked = picked + cnt[r]
    zsum = zero
    for r in range(PEER_TOPK):
        zsum = zsum + jnp.exp(topc[r] - topc[0])
    zinv = 1.0 / zsum
    tie = jnp.where(picked != float(PEER_TOPK), one, zero)
    for t in (top0, top1):
        for r in range(PEER_TOPK - 1):
            tie = jnp.where(t[r] == t[r + 1], one, tie)
    in0, in1 = zero, zero
    rank1, b1, cnt0, a0 = [], [], [], []
    for k in range(N_KEYS):
        v0, v1 = s0[k], s1[k]
        c0 = zero
        rk = jnp.full(shp, float(PEER_TOPK), F32)
        for r in range(PEER_TOPK - 1, -1, -1):
            c0 = jnp.where(v0 >= top0[r], cnt[r], c0)
            rk = jnp.where(v1 >= top1[r], float(r), rk)
        in0 = in0 + jnp.where(v0 >= top0[PEER_TOPK - 1], one, zero)
        in1 = in1 + jnp.where(v1 >= top1[PEER_TOPK - 1], one, zero)
        cnt0.append(c0)
        rank1.append(rk)
        a0.append(jnp.exp(v0 - top0[0]) * zinv)
        b1.append(jnp.exp(v1 - top1[0]))
    tie = jnp.where(in0 != float(PEER_TOPK), one, tie)
    tie = jnp.where(in1 != float(PEER_TOPK), one, tie)
    return rank1, b1, cnt0, a0, tie


def _route_kernel(h2t_ref, wqk_ref, rank1_ref, b1_ref, cnt0_ref, a0_ref,
                  s_scr, sw_scr, rk_scr, val_scr, idx_scr, tmp_scr):
    halves = h2t_ref.shape[1] // LANES
    shp = (PEER_HEADS, LANES)
    s = jnp.dot(wqk_ref[...], h2t_ref[...], preferred_element_type=F32)
    for hf in range(halves):
        s_scr[hf] = s[:, hf * LANES:(hf + 1) * LANES]
    neg = jnp.full(shp, -jnp.inf, F32)

    def rows(p, k):
        return pl.ds(p * _SET_ROWS + k * PEER_HEADS, PEER_HEADS)

    def one_half(hf, carry0):
        f_rank1, f_b1, f_cnt0, f_a0, tie = _route_fast([s_scr[hf, rows(0, k), :] for k in range(N_KEYS)],
                                                       [s_scr[hf, rows(1, k), :] for k in range(N_KEYS)])
        for k in range(N_KEYS):
            kr = pl.ds(k * PEER_HEADS, PEER_HEADS)
            tmp_scr[hf, 0, kr, :] = f_cnt0[k]
            tmp_scr[hf, 1, kr, :] = f_a0[k]
            tmp_scr[hf, 2, kr, :] = f_b1[k]
            rk_scr[hf, rows(1, k), :] = f_rank1[k]

        @pl.when(jnp.max(tie) > 0.0)
        def _():
            exact_half(hf)

        return carry0

    def exact_half(hf):
        sw_scr[hf] = s_scr[hf]
        rk_scr[hf] = jnp.full(rk_scr.shape[1:], float(PEER_TOPK), F32)

        def extract(r, carry):
            rf = jnp.full(shp, r, jnp.int32).astype(F32)
            for p in range(2):
                best, bidx = None, None
                for part in range(4):
                    m = neg
                    ix = jnp.zeros(shp, jnp.int32)
                    for k in range(part * 32, part * 32 + 32):
                        v = sw_scr[hf, rows(p, k), :]
                        gt = v > m
                        m = jnp.where(gt, v, m)
                        ix = jnp.where(gt, k, ix)
                    if best is None:
                        best, bidx = m, ix
                    else:
                        gt = m > best
                        best = jnp.where(gt, m, best)
                        bidx = jnp.where(gt, ix, bidx)
                val_scr[p, r] = best
                idx_scr[p, r] = bidx
                for k in range(N_KEYS):
                    hit = bidx == k
                    sw_scr[hf, rows(p, k), :] = jnp.where(hit, -jnp.inf, sw_scr[hf, rows(p, k), :])
                    rk_scr[hf, rows(p, k), :] = jnp.where(hit, rf, rk_scr[hf, rows(p, k), :])
            return carry

        lax.fori_loop(0, PEER_TOPK, extract, 0)

        v0 = [val_scr[0, r] for r in range(PEER_TOPK)]
        v1 = [val_scr[1, q] for q in range(PEER_TOPK)]
        cand = [v0[r] + v1[q] for (r, q) in _PAIRS]
        top = v0[0] + v1[0]
        cnt = [jnp.zeros(shp, F32) for _ in range(PEER_TOPK)]
        zsum = jnp.zeros(shp, F32)
        for _ in range(PEER_TOPK):
            m = neg
            ix = jnp.zeros(shp, jnp.int32)
            for c, cv in enumerate(cand):
                gt = cv > m
                m = jnp.where(gt, cv, m)
                ix = jnp.where(gt, c, ix)
            zsum = zsum + jnp.exp(m - top)
            for c, (r, q) in enumerate(_PAIRS):
                hit = ix == c
                cand[c] = jnp.where(hit, -jnp.inf, cand[c])
                cnt[r] = cnt[r] + jnp.where(hit, 1.0, 0.0)
        zinv = 1.0 / zsum

        idx0 = [idx_scr[0, r] for r in range(PEER_TOPK)]
        for k in range(N_KEYS):
            c0 = jnp.zeros(shp, F32)
            for r in range(PEER_TOPK):
                c0 = jnp.where(idx0[r] == k, cnt[r], c0)
            kr = pl.ds(k * PEER_HEADS, PEER_HEADS)
            tmp_scr[hf, 0, kr, :] = c0
            tmp_scr[hf, 1, kr, :] = jnp.exp(s_scr[hf, rows(0, k), :] - v0[0]) * zinv
            tmp_scr[hf, 2, kr, :] = jnp.exp(s_scr[hf, rows(1, k), :] - v1[0])

    lax.fori_loop(0, halves, one_half, 0)

    for hf in range(halves):
        ls = slice(hf * LANES, (hf + 1) * LANES)
        for h in range(PEER_HEADS):
            dst = slice(h * N_KEYS, (h + 1) * N_KEYS)
            src = pl.ds(h, N_KEYS, stride=PEER_HEADS)
            cnt0_ref[dst, ls] = _pack_pair(tmp_scr[hf, 0, src, :])
            a0_ref[dst, ls] = _pack_pair(tmp_scr[hf, 1, src, :])
            b1_ref[dst, ls] = tmp_scr[hf, 2, src, :].astype(BF16)
            rank1_ref[dst, ls] = rk_scr[hf, pl.ds(_SET_ROWS + h, N_KEYS, stride=PEER_HEADS), :].astype(BF16)


def _pack_pair(x):
    bits = pltpu.bitcast(x.astype(BF16).astype(F32), jnp.uint32)
    return bits | (bits >> 16)


def _route(h2t, wqk_t):
    n = h2t.shape[1]
    t = ROUTE_T
    rows = PEER_HEADS * N_KEYS
    ospec = pl.BlockSpec((rows, t), lambda m: (0, m))
    out_bf = jax.ShapeDtypeStruct((rows, n), BF16)
    out_pk = jax.ShapeDtypeStruct((rows, n), jnp.uint32)
    return pl.pallas_call(
        _route_kernel,
        grid=(n // t,),
        in_specs=[
            pl.BlockSpec((D_MODEL, t), lambda m: (0, m)),
            pl.BlockSpec((2 * rows, D_MODEL), lambda m: (0, 0)),
        ],
        out_specs=[ospec, ospec, ospec, ospec],
        out_shape=[out_bf, out_bf, out_pk, out_pk],
        scratch_shapes=[
            pltpu.VMEM((t // LANES, 2 * rows, LANES), F32),
            pltpu.VMEM((t // LANES, 2 * rows, LANES), F32),
            pltpu.VMEM((t // LANES, 2 * rows, LANES), F32),
            pltpu.VMEM((2, PEER_TOPK, PEER_HEADS, LANES), F32),
            pltpu.VMEM((2, PEER_TOPK, PEER_HEADS, LANES), jnp.int32),
            pltpu.VMEM((t // LANES, 3, rows, LANES), F32),
        ],
        compiler_params=_cparams(("arbitrary",), VMEM_LIMIT_MID),
        name="route",
    )(h2t, wqk_t)


PEER_T = 512
PEER_E = 512


def _gelu(x):
    return 0.5 * x * (1.0 + lax.erf(x * (2.0 ** -0.5)))


def _bcast_pair_rows(ref, row, t):
    word = jnp.broadcast_to(ref[pl.ds(row, 1), :], (8, t))
    pair = pltpu.bitcast(word, BF16)
    return jnp.broadcast_to(pair[None], (N_KEYS // 16, 16, t)).reshape(N_KEYS, t)


def _peer_coef(tile, u_ref, row0, h2t_ref, rank1_ref, b1_ref, cnt0_ref, a0_ref, coef_ref):
    t = h2t_ref.shape[1]
    act = jnp.dot(u_ref[row0:row0 + PEER_E, :], h2t_ref[...], preferred_element_type=F32)
    per = PEER_E // N_KEYS
    for ii in range(per):
        i = tile * per + ii
        w = None
        for h in range(PEER_HEADS):
            hs = slice(h * N_KEYS, (h + 1) * N_KEYS)
            cnt = _bcast_pair_rows(cnt0_ref, h * N_KEYS + i, t)
            a = _bcast_pair_rows(a0_ref, h * N_KEYS + i, t)
            term = jnp.where(rank1_ref[hs, :] < cnt, b1_ref[hs, :] * a, jnp.zeros((), BF16))
            w = term if w is None else w + term
        rs = slice(ii * N_KEYS, (ii + 1) * N_KEYS)
        coef_ref[rs, :] = _gelu(act[rs, :]).astype(BF16) * w


def _peer_kernel(h2t_ref, u_ref, vtp_ref, vtc_ref, rank1_ref, b1_ref, cnt0_ref, a0_ref, x1_ref, gt2_ref, gf_ref,
                 y_ref, acc, coef_a, coef_b):
    k = pl.program_id(1)
    last = pl.num_programs(1) - 1
    route = (rank1_ref, b1_ref, cnt0_ref, a0_ref)

    def two_tiles():
        _peer_coef(2 * k, u_ref, 0, h2t_ref, *route, coef_a)
        _peer_coef(2 * k + 1, u_ref, PEER_E, h2t_ref, *route, coef_b)
        return jnp.dot(vtc_ref[...], coef_a[...], preferred_element_type=F32)

    @pl.when(k == 0)
    def _():
        acc[...] = two_tiles()

    @pl.when(jnp.logical_and(k > 0, k < last))
    def _():
        acc[...] += jnp.dot(vtp_ref[...], coef_b[...], preferred_element_type=F32)
        acc[...] += two_tiles()

    @pl.when(k == last)
    def _():
        tot = acc[...] + jnp.dot(vtp_ref[...], coef_b[...], preferred_element_type=F32)
        x2 = x1_ref[...] + gt2_ref[...] * jnp.transpose(tot)
        y_ref[...] = _rms(x2) * gf_ref[...]


def _peer(h2t, u_bf, vt_bf, rank1, b1, cnt0, a0, x1, gt2, gfin, per_token, rows_per_batch):
    n = x1.shape[0]
    t = PEER_T
    tpb = max(rows_per_batch // t, 1)
    rows = PEER_HEADS * N_KEYS
    steps = N_EXPERTS // (2 * PEER_E)
    once = pl.Buffered(1)
    tok = lambda r: pl.BlockSpec((r, t), lambda m, k: (0, m), pipeline_mode=once)
    if per_token:
        gt_spec = pl.BlockSpec((t, D_MODEL), lambda m, k: (m, 0), pipeline_mode=once)
    else:
        gt_spec = pl.BlockSpec((None, 1, D_MODEL), lambda m, k: (m // tpb, 0, 0))
    return pl.pallas_call(
        _peer_kernel,
        grid=(n // t, steps + 1),
        in_specs=[
            pl.BlockSpec((D_MODEL, t), lambda m, k: (0, m)),
            pl.BlockSpec((2 * PEER_E, D_MODEL), lambda m, k: (jnp.minimum(k, steps - 1), 0)),
            pl.BlockSpec((D_MODEL, PEER_E), lambda m, k: (0, jnp.maximum(2 * k - 1, 0))),
            pl.BlockSpec((D_MODEL, PEER_E), lambda m, k: (0, 2 * jnp.minimum(k, steps - 1))),
            tok(rows), tok(rows), tok(rows), tok(rows),
            pl.BlockSpec((t, D_MODEL), lambda m, k: (m, 0), pipeline_mode=once),
            gt_spec,
            pl.BlockSpec((1, D_MODEL), lambda m, k: (0, 0)),
        ],
        out_specs=pl.BlockSpec((t, D_MODEL), lambda m, k: (m, 0)),
        out_shape=jax.ShapeDtypeStruct((n, D_MODEL), F32),
        scratch_shapes=[pltpu.VMEM((D_MODEL, t), F32), pltpu.VMEM((PEER_E, t), BF16), pltpu.VMEM((PEER_E, t), BF16)],
        compiler_params=_cparams(("parallel", "arbitrary"), VMEM_LIMIT_BIG),
        name="peer",
    )(h2t, u_bf, vt_bf, vt_bf, rank1, b1, cnt0, a0, x1, gt2, gfin)


def _rel_bias_tile(rel_bias, rows, cols, offset):
    rb = rel_bias.astype(F32)
    heads = rb.shape[0]
    rel_max = offset + rows - 1
    rel_min = offset - (cols - 1)
    lo, hi = max(rel_min, -REL_CLIP), min(rel_max, REL_CLIP)
    parts = []
    if rel_min < -REL_CLIP:
        parts.append(jnp.broadcast_to(rb[:, :1], (heads, -REL_CLIP - rel_min)))
    parts.append(rb[:, lo + REL_CLIP:hi + REL_CLIP + 1])
    if rel_max > REL_CLIP:
        parts.append(jnp.broadcast_to(rb[:, -1:], (heads, rel_max - REL_CLIP)))
    g = jnp.flip(jnp.concatenate(parts, axis=1), axis=1)
    period = rows + cols
    gp = jnp.pad(g, ((0, 0), (0, 1)))
    shifted = jnp.tile(gp, (1, rows))[:, :rows * (period - 1)].reshape(heads, rows, period - 1)
    return shifted[:, :, rows - 1:rows - 1 + cols]


def _prompt_bias(rel_bias):
    a = jnp.arange(Q_SUB, dtype=jnp.int32)[:, None]
    c = jnp.arange(K_WIN, dtype=jnp.int32)[None, :]
    cq = a // CHUNK
    ck = c // CHUNK
    ok = (ck >= cq) & (ck <= cq + N_LEFT_CHUNKS)
    return jnp.where(ok[None], _rel_bias_tile(rel_bias, Q_SUB, K_WIN, BAND_LEFT), NEG_INF)


def _sample_bias(rel_bias, seq, lc):
    return _rel_bias_tile(rel_bias, seq, lc, lc), _rel_bias_tile(rel_bias, seq, seq, 0)


def _layer(x, mod, cache_k, cache_v, state, is_sample, wts):
    (g1, w_main, w_alr, rel_bias, wup, balpha, gnorm, wa, wb, wo, g2, wqk_t, u_bf, vt_bf, gfin) = wts
    batch, seq, _ = x.shape
    n = batch * seq
    xf = x.reshape(n, D_MODEL)
    parts = [mod[:, i * D_MODEL:(i + 1) * D_MODEL] for i in range(6)]
    if is_sample:
        sh1, sc1, gt1, sh2, sc2, gt2 = [jnp.repeat(p, seq, axis=0) for p in parts]
        tm_in, tm_mg = n, 256
    else:
        sh1, sc1, gt1, sh2, sc2, gt2 = [p.reshape(batch, 1, D_MODEL) for p in parts]
        tm_in, tm_mg = 1024, 256

    z, alr = _inproj(xf, sh1, sc1, g1, w_main, w_alr, tm_in, is_sample, seq)

    if is_sample:
        lc = cache_k.shape[1]
        bias_c, bias_n = _sample_bias(rel_bias, seq, lc)
        oa = _attn_sample(z, cache_k.reshape(batch, lc * H_A, DH_A), cache_v.reshape(batch, lc * H_A, DH_A),
                          bias_c, bias_n, batch, seq)
        ob, s_new = _gla_sample(z, alr, state, wup, balpha, gnorm, batch, seq)
        k_rows = z[:, Z_KA:Z_KA + W_A].reshape(batch, seq, H_A, DH_A)
        v_rows = z[:, Z_VA:Z_VA + W_A].reshape(batch, seq, H_A, DH_A)
    else:
        oa = _attn_prompt(z, _prompt_bias(rel_bias), batch, seq)
        ob, s_new = _gla_prompt(z, alr, wup, balpha, gnorm, batch, seq)
        keep = min(BAND_LEFT, seq)
        zk = z.reshape(batch, seq, Z_WIDTH)[:, seq - keep:]
        k_rows = zk[:, :, Z_KA:Z_KA + W_A].reshape(batch, keep, H_A, DH_A)
        v_rows = zk[:, :, Z_VA:Z_VA + W_A].reshape(batch, keep, H_A, DH_A)

    x1, h2t = _merge(oa, ob, z, xf, gt1, sc2, sh2, g2, wa, wb, wo, tm_mg, is_sample, seq)
    rank1, b1, cnt0, a0 = _route(h2t, wqk_t)
    y = _peer(h2t, u_bf, vt_bf, rank1, b1, cnt0, a0, x1, gt2, gfin, is_sample, seq)
    return y.reshape(batch, seq, D_MODEL), k_rows, v_rows, s_new


def kernel(x_prompt, x_sample, cache_a_k, cache_a_v, state_gla, c_prompt, c_sample, w_ada, b_ada, g_norm1, w_in, rel_bias, w_alpha_up, b_alpha, g_gla_norm, w_branch_a, w_branch_b, w_out, g_norm2, w_peer_q, peer_subkeys, peer_u, peer_v, g_final):
    depth = w_in.shape[0]
    assert depth == 1, "the final rmsnorm is fused into the PEER kernel of the only layer"
    nbp = c_prompt.shape[0]
    l = 0
    nbs = c_sample.shape[0]
    pad = -(nbp + nbs) % 16
    c_all = jnp.concatenate([c_prompt, c_sample, jnp.zeros((pad, D_MODEL), F32)], axis=0)
    def only(w):
        return w.reshape(w.shape[1:])

    mod = _ada(c_all, only(w_ada), b_ada[l])

    alr_lo = Z_GL
    w_in0 = only(w_in)
    w_main = jnp.concatenate([w_in0[:, :alr_lo].astype(BF16), w_in0[:, alr_lo + ALPHA_RANK:].astype(BF16)], axis=1)
    w_alr = jnp.pad(w_in0[:, alr_lo:alr_lo + ALPHA_RANK], ((0, 0), (0, ALR_PAD - ALPHA_RANK))).astype(BF16)
    wup = jnp.pad(w_alpha_up[l], ((0, ALR_PAD - ALPHA_RANK), (0, 0))).astype(BF16)
    wqk = _qkfold(only(w_peer_q).astype(BF16), peer_subkeys[l].astype(BF16))
    wqk_t = wqk.reshape(D_MODEL, PEER_HEADS, 2, N_KEYS).transpose(2, 3, 1, 0).reshape(2 * PEER_HEADS * N_KEYS, D_MODEL)
    wts = (
        g_norm1[l].reshape(1, D_MODEL), w_main, w_alr, rel_bias[l], wup,
        b_alpha[l].reshape(1, KW_B), g_gla_norm[l].reshape(1, DV_B),
        only(w_branch_a).astype(BF16), only(w_branch_b).astype(BF16), only(w_out).astype(BF16),
        g_norm2[l].reshape(1, D_MODEL), wqk_t.astype(BF16),
        only(peer_u).astype(BF16), jnp.transpose(only(peer_v).astype(BF16)), g_final.reshape(1, D_MODEL),
    )
    yp, kp, vp, sp = _layer(x_prompt, mod[:nbp], None, None, None, False, wts)
    ys, ks, vs, ss = _layer(x_sample, mod[nbp:nbp + nbs], only(cache_a_k), only(cache_a_v), only(state_gla), True, wts)
    return (yp, ys, kp[None], vp[None], sp[None], ks[None], vs[None], ss[None])
```

```python
import functools
import math

import jax
import jax.numpy as jnp
from jax import lax
from jax.experimental import pallas as pl
from jax.experimental.pallas import tpu as pltpu

F32 = jnp.float32
BF16 = jnp.bfloat16

D_MODEL = 2048
CHUNK = 64
N_LEFT_CHUNKS = 8
BAND_LEFT = N_LEFT_CHUNKS * CHUNK
H_A = 8
DH_A = 128
W_A = H_A * DH_A
REL_CLIP = 128
H_B = 4
DK_B = 128
DV_B = 256
KW_B = H_B * DK_B
VW_B = H_B * DV_B
ALPHA_RANK = 16
GATE_TAU = 16.0
PEER_HEADS = 8
N_KEYS = 128
N_EXPERTS = N_KEYS * N_KEYS
PEER_TOPK = 16
EPS = 1e-6
NEG_INF = -1e30

Z_QA, Z_KA, Z_VA = 0, W_A, 2 * W_A
Z_QB = 3 * W_A
Z_KB = Z_QB + KW_B
Z_VB = Z_KB + KW_B
Z_RB = Z_VB + VW_B
Z_GL = Z_RB + VW_B
Z_WIDTH = Z_GL + 2 * D_MODEL
ALR_PAD = 128

VMEM_LIMIT_BIG = 56 * 1024 * 1024
VMEM_LIMIT_MID = 40 * 1024 * 1024

Q_TILE = 512
Q_SUB = 128
K_WIN = Q_SUB + BAND_LEFT


def _cparams(sem, vmem=None):
    return pltpu.CompilerParams(dimension_semantics=sem, vmem_limit_bytes=vmem)


def _rms(xf):
    return xf * lax.rsqrt(jnp.mean(xf * xf, axis=-1, keepdims=True) + EPS)


def _sigmoid(x):
    return 1.0 / (1.0 + jnp.exp(-x))


def _ada_kernel(c_ref, w_ref, b_ref, o_ref):
    c = c_ref[...]
    a = (c * _sigmoid(c)).astype(BF16)
    o_ref[...] = jnp.dot(a, w_ref[...].astype(BF16), preferred_element_type=F32) + b_ref[...]


def _ada(c_all, w_ada, b_ada):
    nb = c_all.shape[0]
    n_out = w_ada.shape[1]
    tn = 1024
    return pl.pallas_call(
        _ada_kernel,
        grid=(n_out // tn,),
        in_specs=[
            pl.BlockSpec((nb, D_MODEL), lambda j: (0, 0)),
            pl.BlockSpec((D_MODEL, tn), lambda j: (0, j)),
            pl.BlockSpec((1, tn), lambda j: (0, j)),
        ],
        out_specs=pl.BlockSpec((nb, tn), lambda j: (0, j)),
        out_shape=jax.ShapeDtypeStruct((nb, n_out), F32),
        compiler_params=_cparams(("arbitrary",), VMEM_LIMIT_MID),
        name="ada",
    )(c_all, w_ada, b_ada.reshape(1, n_out))


def _mod_spec(per_token, tm, tiles_per_batch, grid_rank):
    if per_token:
        if grid_rank == 1:
            return pl.BlockSpec((tm, D_MODEL), lambda m: (m, 0))
        return pl.BlockSpec((tm, D_MODEL), lambda m, n: (m, 0))
    if grid_rank == 1:
        return pl.BlockSpec((None, 1, D_MODEL), lambda m: (m // tiles_per_batch, 0, 0))
    return pl.BlockSpec((None, 1, D_MODEL), lambda m, n: (m // tiles_per_batch, 0, 0))


def _inproj_kernel(x_ref, sh_ref, sc_ref, g_ref, w_ref, walr_ref, z_ref, alr_ref, h_scr):
    @pl.when(pl.program_id(1) == 0)
    def _():
        h = _rms(x_ref[...]) * g_ref[...]
        h = h * (1.0 + sc_ref[...]) + sh_ref[...]
        hb = h.astype(BF16)
        h_scr[...] = hb
        alr_ref[...] = jnp.dot(hb, walr_ref[...], preferred_element_type=F32)

    z_ref[...] = jnp.dot(h_scr[...], w_ref[...], preferred_element_type=F32)


def _inproj(x, sh, sc, g, w_main, w_alr, tm, per_token, rows_per_batch):
    n = x.shape[0]
    tn = 1024
    tpb = max(rows_per_batch // tm, 1)
    return pl.pallas_call(
        _inproj_kernel,
        grid=(n // tm, Z_WIDTH // tn),
        in_specs=[
            pl.BlockSpec((tm, D_MODEL), lambda m, j: (m, 0)),
            _mod_spec(per_token, tm, tpb, 2),
            _mod_spec(per_token, tm, tpb, 2),
            pl.BlockSpec((1, D_MODEL), lambda m, j: (0, 0)),
            pl.BlockSpec((D_MODEL, tn), lambda m, j: (0, j)),
            pl.BlockSpec((D_MODEL, ALR_PAD), lambda m, j: (0, 0)),
        ],
        out_specs=[
            pl.BlockSpec((tm, tn), lambda m, j: (m, j)),
            pl.BlockSpec((tm, ALR_PAD), lambda m, j: (m, 0)),
        ],
        out_shape=[
            jax.ShapeDtypeStruct((n, Z_WIDTH), F32),
            jax.ShapeDtypeStruct((n, ALR_PAD), F32),
        ],
        scratch_shapes=[pltpu.VMEM((tm, D_MODEL), BF16)],
        compiler_params=_cparams(("parallel", "arbitrary"), VMEM_LIMIT_BIG),
        name="inproj",
    )(x, sh, sc, g, w_main, w_alr)


def _attn_prompt_kernel(q_ref, kp_ref, kc_ref, vp_ref, vc_ref, bias_ref, o_ref, kw, vw):
    first = pl.program_id(2) == 0
    kw[0:Q_TILE, :] = kp_ref[...].astype(BF16)
    kw[Q_TILE:2 * Q_TILE, :] = kc_ref[...].astype(BF16)
    vw[0:Q_TILE, :] = vp_ref[...].astype(BF16)
    vw[Q_TILE:2 * Q_TILE, :] = vc_ref[...].astype(BF16)
    bias = bias_ref[...]
    scale = DH_A ** -0.5
    subs = range(Q_TILE // Q_SUB)
    scores = [lax.dot_general((q_ref[r * Q_SUB:(r + 1) * Q_SUB, :] * scale).astype(BF16),
                              kw[r * Q_SUB:r * Q_SUB + K_WIN, :],
                              (((1,), (1,)), ((), ())), preferred_element_type=F32) for r in subs]
    probs, sums = [], []
    col = lax.broadcasted_iota(jnp.int32, (1, K_WIN), 1)
    for r in subs:
        before = jnp.where(jnp.logical_and(first, col + r * Q_SUB < Q_TILE), NEG_INF, 0.0)
        s = scores[r] + bias + before
        m = jnp.max(s, axis=-1, keepdims=True)
        p = jnp.exp(s - m)
        sums.append(jnp.sum(p, axis=-1, keepdims=True))
        probs.append(p.astype(BF16))
    outs = [jnp.dot(probs[r], vw[r * Q_SUB:r * Q_SUB + K_WIN, :], preferred_element_type=F32) for r in subs]
    for r in subs:
        o_ref[r * Q_SUB:(r + 1) * Q_SUB, :] = (outs[r] / sums[r]).astype(o_ref.dtype)


def _attn_prompt(z, bias, batch, seq):
    n = batch * seq
    tiles = seq // Q_TILE

    def cur(col):
        return pl.BlockSpec((Q_TILE, DH_A), lambda h, b, i: (b * tiles + i, col + h))

    def prev(col):
        return pl.BlockSpec((Q_TILE, DH_A),
                            lambda h, b, i: (b * tiles + jnp.maximum(i - 1, 0), col + h))

    kcol, vcol = Z_KA // DH_A, Z_VA // DH_A
    return pl.pallas_call(
        _attn_prompt_kernel,
        grid=(H_A, batch, tiles),
        in_specs=[cur(0), prev(kcol), cur(kcol), prev(vcol), cur(vcol),
                  pl.BlockSpec((None, Q_SUB, K_WIN), lambda h, b, i: (h, 0, 0))],
        out_specs=pl.BlockSpec((Q_TILE, DH_A), lambda h, b, i: (b * tiles + i, h)),
        out_shape=jax.ShapeDtypeStruct((n, W_A), BF16),
        scratch_shapes=[pltpu.VMEM((2 * Q_TILE, DH_A), BF16), pltpu.VMEM((2 * Q_TILE, DH_A), BF16)],
        compiler_params=_cparams(("arbitrary", "arbitrary", "arbitrary")),
        name="attn_prompt",
    )(z, z, z, z, z, bias)


def _attn_sample_kernel(q_ref, kn_ref, vn_ref, ck_ref, cv_ref, bc_ref, bn_ref, o_ref):
    scale = DH_A ** -0.5
    nt = (((1,), (1,)), ((), ()))
    lc = ck_ref.shape[0] // H_A
    heads = range(H_A)
    cols = [slice(h * DH_A, (h + 1) * DH_A) for h in heads]
    crow = [pl.ds(h, lc, stride=H_A) for h in heads]
    qs = [q_ref[:, cols[h]].astype(BF16) for h in heads]
    s_c = [lax.dot_general(qs[h], ck_ref[crow[h], :].astype(BF16), nt, preferred_element_type=F32) for h in heads]
    s_n = [lax.dot_general(qs[h], kn_ref[:, cols[h]].astype(BF16), nt, preferred_element_type=F32) for h in heads]
    p_c, p_n, sums = [], [], []
    for h in heads:
        sc = s_c[h] * scale + bc_ref[h]
        sn = s_n[h] * scale + bn_ref[h]
        m = jnp.maximum(jnp.max(sc, axis=-1, keepdims=True), jnp.max(sn, axis=-1, keepdims=True))
        pc = jnp.exp(sc - m)
        pn = jnp.exp(sn - m)
        sums.append(jnp.sum(pc, axis=-1, keepdims=True) + jnp.sum(pn, axis=-1, keepdims=True))
        p_c.append(pc.astype(BF16))
        p_n.append(pn.astype(BF16))
    outs = [jnp.dot(p_c[h], cv_ref[crow[h], :].astype(BF16), preferred_element_type=F32)
            + jnp.dot(p_n[h], vn_ref[:, cols[h]].astype(BF16), preferred_element_type=F32) for h in heads]
    for h in heads:
        o_ref[:, cols[h]] = (outs[h] / sums[h]).astype(o_ref.dtype)


def _attn_sample(z, cache_k, cache_v, bias_c, bias_n, batch, seq):
    lc = cache_k.shape[1] // H_A
    return pl.pallas_call(
        _attn_sample_kernel,
        grid=(batch,),
        in_specs=[
            pl.BlockSpec((seq, W_A), lambda b: (b, Z_QA // W_A)),
            pl.BlockSpec((seq, W_A), lambda b: (b, Z_KA // W_A)),
            pl.BlockSpec((seq, W_A), lambda b: (b, Z_VA // W_A)),
            pl.BlockSpec((None, lc * H_A, DH_A), lambda b: (b, 0, 0)),
            pl.BlockSpec((None, lc * H_A, DH_A), lambda b: (b, 0, 0)),
            pl.BlockSpec((H_A, seq, lc), lambda b: (0, 0, 0)),
            pl.BlockSpec((H_A, seq, seq), lambda b: (0, 0, 0)),
        ],
        out_specs=pl.BlockSpec((seq, W_A), lambda b: (b, 0)),
        out_shape=jax.ShapeDtypeStruct((batch * seq, W_A), BF16),
        compiler_params=_cparams(("arbitrary",), VMEM_LIMIT_MID),
        name="attn_sample",
    )(z, z, z, cache_k, cache_v, bias_c, bias_n)


def _split_bf16(x):
    hi = x.astype(BF16)
    lo = (x - hi.astype(F32)).astype(BF16)
    return hi, lo


_TN = (((0,), (0,)), ((), ()))
_NT = (((1,), (1,)), ((), ()))


def _gla_state_free(units, tri, causal, c):
    scale = DK_B ** -0.5
    ones = jnp.ones((c, DK_B), BF16)
    split = [_split_bf16(g) for (_, _, _, g) in units]
    cum = [jnp.dot(tri, hi, preferred_element_type=F32) + jnp.dot(tri, lo, preferred_element_type=F32)
           for (hi, lo) in split]
    tot = [lax.dot_general(hi, ones, _TN, preferred_element_type=F32)
           + lax.dot_general(lo, ones, _TN, preferred_element_type=F32) for (hi, lo) in split]
    qd, kd, vb, att = [], [], [], []
    for (q, k, v, _), b in zip(units, cum):
        mid = b[c // 2:c // 2 + 1, :]
        blast = b[c - 1:c, :]
        qs = q * scale
        ks = k * scale
        att.append(lax.dot_general((qs * jnp.exp(b - mid)).astype(BF16), (ks * jnp.exp(mid - b)).astype(BF16),
                                   _NT, preferred_element_type=F32))
        qd.append((qs * jnp.exp(b)).astype(BF16))
        kd.append((ks * jnp.exp(blast - b)).astype(BF16))
        vb.append(v.astype(BF16))
    intra = [jnp.dot(jnp.where(causal, a, 0.0).astype(BF16), v, preferred_element_type=F32)
             for a, v in zip(att, vb)]
    out = []
    for i in range(len(units)):
        dec = jnp.exp(tot[i])
        out.append((intra[i], qd[i], kd[i], vb[i], jnp.concatenate([dec, dec], axis=1)))
    return out


def _gla_read_state(pre, s_prev):
    o_intra, qd, _, _, _ = pre
    return o_intra + jnp.dot(qd, s_prev.astype(BF16), preferred_element_type=F32)


def _gla_next_state(pre, s_prev):
    _, _, kd, vb, dec = pre
    return dec * s_prev + lax.dot_general(kd, vb, _TN, preferred_element_type=F32)


def _log_decay(alr, wup, balpha):
    x = jnp.dot(alr.astype(BF16), wup, preferred_element_type=F32) + balpha
    return (jnp.minimum(x, 0.0) - jnp.log1p(jnp.exp(-jnp.abs(x)))) * (1.0 / GATE_TAU)


def _gla_epilogue(o, rb, gnorm):
    y = _rms(o) * gnorm
    return y * (rb * _sigmoid(rb))


def _tri_and_causal(c):
    row = lax.broadcasted_iota(jnp.int32, (c, c), 0)
    col = lax.broadcasted_iota(jnp.int32, (c, c), 1)
    causal = col <= row
    return jnp.where(causal, 1.0, 0.0).astype(BF16), causal


def _gla_prompt_kernel(q_ref, k_ref, v_ref, rb_ref, alr_ref, wup_ref, balpha_ref, gnorm_ref,
                       ob_ref, st_ref, s_scr, *, batch, blocks):
    step = pl.program_id(0)

    @pl.when(step == 0)
    def _():
        s_scr[...] = jnp.zeros_like(s_scr)

    tri, causal = _tri_and_causal(CHUNK)
    wup = wup_ref[...]
    balpha = balpha_ref[...]
    gnorm = gnorm_ref[...]
    ksl = [slice(h * DK_B, (h + 1) * DK_B) for h in range(H_B)]
    vsl = [slice(h * DV_B, (h + 1) * DV_B) for h in range(H_B)]
    streams = [(bi, h) for bi in range(batch) for h in range(H_B)]
    rows = [slice(blk * CHUNK, (blk + 1) * CHUNK) for blk in range(blocks)]
    logg = {(blk, bi): _log_decay(alr_ref[bi, rows[blk], :], wup, balpha)
            for blk in range(blocks) for bi in range(batch)}
    units = [(q_ref[bi, rows[blk], ksl[h]], k_ref[bi, rows[blk], ksl[h]], v_ref[bi, rows[blk], vsl[h]],
              logg[(blk, bi)][:, ksl[h]]) for blk in range(blocks) for (bi, h) in streams]
    pre = _gla_state_free(units, tri, causal, CHUNK)
    for blk in range(blocks):
        mine = pre[blk * len(streams):(blk + 1) * len(streams)]
        outs = [_gla_read_state(p, s_scr[si]) for si, p in enumerate(mine)]
        nxt = [_gla_next_state(p, s_scr[si]) for si, p in enumerate(mine)]
        for si, (bi, h) in enumerate(streams):
            s_scr[si] = nxt[si]
            ob_ref[bi, rows[blk], vsl[h]] = _gla_epilogue(outs[si], rb_ref[bi, rows[blk], vsl[h]],
                                                          gnorm).astype(ob_ref.dtype)

    @pl.when(step == pl.num_programs(0) - 1)
    def _():
        st_ref[...] = s_scr[...]


def _gla_prompt(z, alr, wup, balpha, gnorm, batch, seq):
    blocks = 4
    rows = blocks * CHUNK
    z3 = z.reshape(batch, seq, Z_WIDTH)
    alr3 = alr.reshape(batch, seq, ALR_PAD)
    ob, st = pl.pallas_call(
        functools.partial(_gla_prompt_kernel, batch=batch, blocks=blocks),
        grid=(seq // rows,),
        in_specs=[
            pl.BlockSpec((batch, rows, KW_B), lambda j: (0, j, Z_QB // KW_B)),
            pl.BlockSpec((batch, rows, KW_B), lambda j: (0, j, Z_KB // KW_B)),
            pl.BlockSpec((batch, rows, VW_B), lambda j: (0, j, Z_VB // VW_B)),
            pl.BlockSpec((batch, rows, VW_B), lambda j: (0, j, Z_RB // VW_B)),
            pl.BlockSpec((batch, rows, ALR_PAD), lambda j: (0, j, 0)),
            pl.BlockSpec((ALR_PAD, KW_B), lambda j: (0, 0)),
            pl.BlockSpec((1, KW_B), lambda j: (0, 0)),
            pl.BlockSpec((1, DV_B), lambda j: (0, 0)),
        ],
        out_specs=[
            pl.BlockSpec((batch, rows, VW_B), lambda j: (0, j, 0)),
            pl.BlockSpec((batch * H_B, DK_B, DV_B), lambda j: (0, 0, 0)),
        ],
        out_shape=[
            jax.ShapeDtypeStruct((batch, seq, VW_B), BF16),
            jax.ShapeDtypeStruct((batch * H_B, DK_B, DV_B), F32),
        ],
        scratch_shapes=[pltpu.VMEM((batch * H_B, DK_B, DV_B), F32)],
        compiler_params=_cparams(("arbitrary",), VMEM_LIMIT_MID),
        name="gla_prompt",
    )(z3, z3, z3, z3, alr3, wup, balpha, gnorm)
    return ob.reshape(batch * seq, VW_B), st.reshape(batch, H_B, DK_B, DV_B)


def _gla_sample_kernel(q_ref, k_ref, v_ref, rb_ref, alr_ref, s0_ref, wup_ref, balpha_ref, gnorm_ref,
                       ob_ref, st_ref, *, seq):
    tri, causal = _tri_and_causal(seq)
    logg = _log_decay(alr_ref[...], wup_ref[...], balpha_ref[...])
    gnorm = gnorm_ref[...]
    ksl = [slice(h * DK_B, (h + 1) * DK_B) for h in range(H_B)]
    vsl = [slice(h * DV_B, (h + 1) * DV_B) for h in range(H_B)]
    pre = _gla_state_free([(q_ref[:, ksl[h]], k_ref[:, ksl[h]], v_ref[:, vsl[h]], logg[:, ksl[h]])
                           for h in range(H_B)], tri, causal, seq)
    outs = [_gla_read_state(pre[h], s0_ref[h]) for h in range(H_B)]
    nxt = [_gla_next_state(pre[h], s0_ref[h]) for h in range(H_B)]
    for h in range(H_B):
        st_ref[h] = nxt[h]
        ob_ref[:, vsl[h]] = _gla_epilogue(outs[h], rb_ref[:, vsl[h]], gnorm).astype(ob_ref.dtype)


def _gla_sample(z, alr, state, wup, balpha, gnorm, batch, seq):
    return pl.pallas_call(
        functools.partial(_gla_sample_kernel, seq=seq),
        grid=(batch,),
        in_specs=[
            pl.BlockSpec((seq, KW_B), lambda b: (b, Z_QB // KW_B)),
            pl.BlockSpec((seq, KW_B), lambda b: (b, Z_KB // KW_B)),
            pl.BlockSpec((seq, VW_B), lambda b: (b, Z_VB // VW_B)),
            pl.BlockSpec((seq, VW_B), lambda b: (b, Z_RB // VW_B)),
            pl.BlockSpec((seq, ALR_PAD), lambda b: (b, 0)),
            pl.BlockSpec((None, H_B, DK_B, DV_B), lambda b: (b, 0, 0, 0)),
            pl.BlockSpec((ALR_PAD, KW_B), lambda b: (0, 0)),
            pl.BlockSpec((1, KW_B), lambda b: (0, 0)),
            pl.BlockSpec((1, DV_B), lambda b: (0, 0)),
        ],
        out_specs=[
            pl.BlockSpec((seq, VW_B), lambda b: (b, 0)),
            pl.BlockSpec((None, H_B, DK_B, DV_B), lambda b: (b, 0, 0, 0)),
        ],
        out_shape=[
            jax.ShapeDtypeStruct((batch * seq, VW_B), BF16),
            jax.ShapeDtypeStruct((batch, H_B, DK_B, DV_B), F32),
        ],
        compiler_params=_cparams(("arbitrary",)),
        name="gla_sample",
    )(z, z, z, z, alr, state, wup, balpha, gnorm)


def _merge_kernel(oa_ref, ob_ref, gla_ref, glb_ref, x_ref, gt1_ref, sc2_ref, sh2_ref, g2_ref,
                  wa_ref, wb_ref, wo_ref, x1_ref, h2t_ref):
    ya = jnp.dot(oa_ref[...], wa_ref[...], preferred_element_type=F32)
    yb = jnp.dot(ob_ref[...], wb_ref[...], preferred_element_type=F32)
    merged = _sigmoid(gla_ref[...]) * ya + _sigmoid(glb_ref[...]) * yb
    x1 = x_ref[...] + gt1_ref[...] * jnp.dot(merged.astype(BF16), wo_ref[...], preferred_element_type=F32)
    x1_ref[...] = x1
    h2 = _rms(x1) * g2_ref[...]
    h2 = h2 * (1.0 + sc2_ref[...]) + sh2_ref[...]
    h2t_ref[...] = jnp.transpose(h2).astype(BF16)


PEER_T = 512


def _tok_major_spec(rows, width, grid_rank):
    per = PEER_T // width
    if grid_rank == 1:
        return pl.BlockSpec((None, rows, width), lambda m: (m // per, 0, m % per))
    return pl.BlockSpec((None, rows, width), lambda m, k: (m // per, 0, m % per))


def _merge(oa, ob, z, x, gt1, sc2, sh2, g2, wa, wb, wo, tm, per_token, rows_per_batch):
    n = x.shape[0]
    tpb = max(rows_per_batch // tm, 1)
    const = lambda shape: pl.BlockSpec(shape, lambda m: (0, 0))
    mod = _mod_spec(per_token, tm, tpb, 1)
    return pl.pallas_call(
        _merge_kernel,
        grid=(n // tm,),
        in_specs=[
            pl.BlockSpec((tm, W_A), lambda m: (m, 0)),
            pl.BlockSpec((tm, VW_B), lambda m: (m, 0)),
            pl.BlockSpec((tm, D_MODEL), lambda m: (m, Z_GL // D_MODEL)),
            pl.BlockSpec((tm, D_MODEL), lambda m: (m, Z_GL // D_MODEL + 1)),
            pl.BlockSpec((tm, D_MODEL), lambda m: (m, 0)),
            mod, mod, mod,
            const((1, D_MODEL)),
            const((W_A, D_MODEL)), const((VW_B, D_MODEL)), const((D_MODEL, D_MODEL)),
        ],
        out_specs=[
            pl.BlockSpec((tm, D_MODEL), lambda m: (m, 0)),
            _tok_major_spec(D_MODEL, tm, 1),
        ],
        out_shape=[
            jax.ShapeDtypeStruct((n, D_MODEL), F32),
            jax.ShapeDtypeStruct((n // PEER_T, D_MODEL, PEER_T), BF16),
        ],
        compiler_params=_cparams(("arbitrary",), VMEM_LIMIT_BIG),
        name="merge",
    )(oa, ob, z, z, x, gt1, sc2, sh2, g2, wa, wb, wo)


def _qkfold_kernel(wq_ref, sk_ref, o_ref):
    o_ref[...] = lax.dot_general(wq_ref[...], sk_ref[...], (((1,), (1,)), ((), ())),
                                 preferred_element_type=F32)


def _qkfold(w_q, subkeys):
    half = N_KEYS
    return pl.pallas_call(
        _qkfold_kernel,
        grid=(2 * PEER_HEADS,),
        in_specs=[
            pl.BlockSpec((D_MODEL, half), lambda c: (0, c)),
            pl.BlockSpec((None, N_KEYS, half), lambda c: (c % 2, 0, 0)),
        ],
        out_specs=pl.BlockSpec((D_MODEL, N_KEYS), lambda c: (0, c)),
        out_shape=jax.ShapeDtypeStruct((D_MODEL, 2 * PEER_HEADS * N_KEYS), F32),
        compiler_params=_cparams(("arbitrary",)),
        name="qkfold",
    )(w_q, subkeys)


ROUTE_T = 256
LANES = 128
_SET_ROWS = N_KEYS * PEER_HEADS
_PAIRS = [(r, q) for r in range(PEER_TOPK) for q in range(PEER_TOPK) if (r + 1) * (q + 1) <= PEER_TOPK]


def _sort_desc(x):
    x = list(x)
    n = len(x)
    k = 2
    while k <= n:
        j = k // 2
        while j >= 1:
            for i in range(n):
                l = i ^ j
                if l > i:
                    hi, lo = jnp.maximum(x[i], x[l]), jnp.minimum(x[i], x[l])
                    x[i], x[l] = (hi, lo) if (i & k) == 0 else (lo, hi)
            j //= 2
        k *= 2
    return x


def _merge_top(a, b):
    n = len(a)
    x = [jnp.maximum(a[i], b[n - 1 - i]) for i in range(n)]
    j = n // 2
    while j >= 1:
        for i in range(n):
            l = i ^ j
            if l > i:
                x[i], x[l] = jnp.maximum(x[i], x[l]), jnp.minimum(x[i], x[l])
        j //= 2
    return x


def _top_sorted(vals, top):
    groups = [_sort_desc(vals[g:g + top]) for g in range(0, len(vals), top)]
    while len(groups) > 1:
        groups = [_merge_top(groups[g], groups[g + 1]) for g in range(0, len(groups), 2)]
    return groups[0]


def _route_fast(s0, s1):
    shp = s0[0].shape
    one = jnp.ones(shp, F32)
    zero = jnp.zeros(shp, F32)
    top0 = _top_sorted(s0, PEER_TOPK)
    top1 = _top_sorted(s1, PEER_TOPK)
    cand = [top0[r] + top1[q] for (r, q) in _PAIRS]
    pad = [jnp.full(shp, -jnp.inf, F32)] * (-len(cand) % PEER_TOPK)
    topc = _top_sorted(cand + pad, PEER_TOPK)
    tau = topc[PEER_TOPK - 1]
    cnt = [zero] * PEER_TOPK
    for c, (r, q) in enumerate(_PAIRS):
        cnt[r] = cnt[r] + jnp.where(cand[c] >= tau, one, zero)
    picked = cnt[0]
    for r in range(1, PEER_TOPK):
        picked = picked + cnt[r]
    zsum = zero
    for r in range(PEER_TOPK):
        zsum = zsum + jnp.exp(topc[r] - topc[0])
    zinv = 1.0 / zsum
    tie = jnp.where(picked != float(PEER_TOPK), one, zero)
    for t in (top0, top1):
        for r in range(PEER_TOPK - 1):
            tie = jnp.where(t[r] == t[r + 1], one, tie)
    in0, in1 = zero, zero
    rank1, b1, cnt0, a0 = [], [], [], []
    for k in range(N_KEYS):
        v0, v1 = s0[k], s1[k]
        c0 = zero
        rk = jnp.full(shp, float(PEER_TOPK), F32)
        for r in range(PEER_TOPK - 1, -1, -1):
            c0 = jnp.where(v0 >= top0[r], cnt[r], c0)
            rk = jnp.where(v1 >= top1[r], float(r), rk)
        in0 = in0 + jnp.where(v0 >= top0[PEER_TOPK - 1], one, zero)
        in1 = in1 + jnp.where(v1 >= top1[PEER_TOPK - 1], one, zero)
        cnt0.append(c0)
        rank1.append(rk)
        a0.append(jnp.exp(v0 - top0[0]) * zinv)
        b1.append(jnp.exp(v1 - top1[0]))
    tie = jnp.where(in0 != float(PEER_TOPK), one, tie)
    tie = jnp.where(in1 != float(PEER_TOPK), one, tie)
    return rank1, b1, cnt0, a0, tie


def _route_kernel(h2t_ref, wqk_ref, rank1_ref, b1_ref, cnt0_ref, a0_ref,
                  s_scr, sw_scr, rk_scr, val_scr, idx_scr, tmp_scr):
    halves = h2t_ref.shape[1] // LANES
    shp = (PEER_HEADS, LANES)
    s = jnp.dot(wqk_ref[...], h2t_ref[...], preferred_element_type=F32)
    for hf in range(halves):
        s_scr[hf] = s[:, hf * LANES:(hf + 1) * LANES]
    neg = jnp.full(shp, -jnp.inf, F32)

    def rows(p, k):
        return pl.ds(p * _SET_ROWS + k * PEER_HEADS, PEER_HEADS)

    def one_half(hf, carry0):
        f_rank1, f_b1, f_cnt0, f_a0, tie = _route_fast([s_scr[hf, rows(0, k), :] for k in range(N_KEYS)],
                                                       [s_scr[hf, rows(1, k), :] for k in range(N_KEYS)])
        for k in range(N_KEYS):
            kr = pl.ds(k * PEER_HEADS, PEER_HEADS)
            tmp_scr[hf, 0, kr, :] = f_cnt0[k]
            tmp_scr[hf, 1, kr, :] = f_a0[k]
            tmp_scr[hf, 2, kr, :] = f_b1[k]
            rk_scr[hf, rows(1, k), :] = f_rank1[k]

        @pl.when(jnp.max(tie) > 0.0)
        def _():
            exact_half(hf)

        return carry0

    def exact_half(hf):
        sw_scr[hf] = s_scr[hf]
        rk_scr[hf] = jnp.full(rk_scr.shape[1:], float(PEER_TOPK), F32)

        def extract(r, carry):
            rf = jnp.full(shp, r, jnp.int32).astype(F32)
            for p in range(2):
                best, bidx = None, None
                for part in range(4):
                    m = neg
                    ix = jnp.zeros(shp, jnp.int32)
                    for k in range(part * 32, part * 32 + 32):
                        v = sw_scr[hf, rows(p, k), :]
                        gt = v > m
                        m = jnp.where(gt, v, m)
                        ix = jnp.where(gt, k, ix)
                    if best is None:
                        best, bidx = m, ix
                    else:
                        gt = m > best
                        best = jnp.where(gt, m, best)
                        bidx = jnp.where(gt, ix, bidx)
                val_scr[p, r] = best
                idx_scr[p, r] = bidx
                for k in range(N_KEYS):
                    hit = bidx == k
                    sw_scr[hf, rows(p, k), :] = jnp.where(hit, -jnp.inf, sw_scr[hf, rows(p, k), :])
                    rk_scr[hf, rows(p, k), :] = jnp.where(hit, rf, rk_scr[hf, rows(p, k), :])
            return carry

        lax.fori_loop(0, PEER_TOPK, extract, 0)

        v0 = [val_scr[0, r] for r in range(PEER_TOPK)]
        v1 = [val_scr[1, q] for q in range(PEER_TOPK)]
        cand = [v0[r] + v1[q] for (r, q) in _PAIRS]
        top = v0[0] + v1[0]
        cnt = [jnp.zeros(shp, F32) for _ in range(PEER_TOPK)]
        zsum = jnp.zeros(shp, F32)
        for _ in range(PEER_TOPK):
            m = neg
            ix = jnp.zeros(shp, jnp.int32)
            for c, cv in enumerate(cand):
                gt = cv > m
                m = jnp.where(gt, cv, m)
                ix = jnp.where(gt, c, ix)
            zsum = zsum + jnp.exp(m - top)
            for c, (r, q) in enumerate(_PAIRS):
                hit = ix == c
                cand[c] = jnp.where(hit, -jnp.inf, cand[c])
                cnt[r] = cnt[r] + jnp.where(hit, 1.0, 0.0)
        zinv = 1.0 / zsum

        idx0 = [idx_scr[0, r] for r in range(PEER_TOPK)]
        for k in range(N_KEYS):
            c0 = jnp.zeros(shp, F32)
            for r in range(PEER_TOPK):
                c0 = jnp.where(idx0[r] == k, cnt[r], c0)
            kr = pl.ds(k * PEER_HEADS, PEER_HEADS)
            tmp_scr[hf, 0, kr, :] = c0
            tmp_scr[hf, 1, kr, :] = jnp.exp(s_scr[hf, rows(0, k), :] - v0[0]) * zinv
            tmp_scr[hf, 2, kr, :] = jnp.exp(s_scr[hf, rows(1, k), :] - v1[0])

    lax.fori_loop(0, halves, one_half, 0)

    for hf in range(halves):
        ls = slice(hf * LANES, (hf + 1) * LANES)
        for h in range(PEER_HEADS):
            dst = slice(h * N_KEYS, (h + 1) * N_KEYS)
            src = pl.ds(h, N_KEYS, stride=PEER_HEADS)
            cnt0_ref[dst, ls] = _pack_pair(tmp_scr[hf, 0, src, :])
            a0_ref[dst, ls] = _pack_pair(tmp_scr[hf, 1, src, :])
            b1_ref[dst, ls] = tmp_scr[hf, 2, src, :].astype(BF16)
            rank1_ref[dst, ls] = rk_scr[hf, pl.ds(_SET_ROWS + h, N_KEYS, stride=PEER_HEADS), :].astype(BF16)


def _pack_pair(x):
    bits = pltpu.bitcast(x.astype(BF16).astype(F32), jnp.uint32)
    return bits | (bits >> 16)


def _route(h2t, wqk_t):
    n = h2t.shape[0] * PEER_T
    t = ROUTE_T
    rows = PEER_HEADS * N_KEYS
    ospec = _tok_major_spec(rows, t, 1)
    out_bf = jax.ShapeDtypeStruct((n // PEER_T, rows, PEER_T), BF16)
    out_pk = jax.ShapeDtypeStruct((n // PEER_T, rows, PEER_T), jnp.uint32)
    return pl.pallas_call(
        _route_kernel,
        grid=(n // t,),
        in_specs=[
            _tok_major_spec(D_MODEL, t, 1),
            pl.BlockSpec((2 * rows, D_MODEL), lambda m: (0, 0)),
        ],
        out_specs=[ospec, ospec, ospec, ospec],
        out_shape=[out_bf, out_bf, out_pk, out_pk],
        scratch_shapes=[
            pltpu.VMEM((t // LANES, 2 * rows, LANES), F32),
            pltpu.VMEM((t // LANES, 2 * rows, LANES), F32),
            pltpu.VMEM((t // LANES, 2 * rows, LANES), F32),
            pltpu.VMEM((2, PEER_TOPK, PEER_HEADS, LANES), F32),
            pltpu.VMEM((2, PEER_TOPK, PEER_HEADS, LANES), jnp.int32),
            pltpu.VMEM((t // LANES, 3, rows, LANES), F32),
        ],
        compiler_params=_cparams(("arbitrary",), VMEM_LIMIT_MID),
        name="route",
    )(h2t, wqk_t)


PEER_E = 512


def _gelu(x):
    return 0.5 * x * (1.0 + lax.erf(x * (2.0 ** -0.5)))


def _bcast_pair_rows(ref, row, t):
    word = jnp.broadcast_to(ref[pl.ds(row, 1), :], (8, t))
    pair = pltpu.bitcast(word, BF16)
    return jnp.broadcast_to(pair[None], (N_KEYS // 16, 16, t)).reshape(N_KEYS, t)


def _peer_coef(tile, u_ref, row0, h2t_ref, rank1_ref, b1_ref, cnt0_ref, a0_ref, coef_ref):
    t = h2t_ref.shape[1]
    act = jnp.dot(u_ref[row0:row0 + PEER_E, :], h2t_ref[...], preferred_element_type=F32)
    per = PEER_E // N_KEYS
    for ii in range(per):
        i = tile * per + ii
        w = None
        for h in range(PEER_HEADS):
            hs = slice(h * N_KEYS, (h + 1) * N_KEYS)
            cnt = _bcast_pair_rows(cnt0_ref, h * N_KEYS + i, t)
            a = _bcast_pair_rows(a0_ref, h * N_KEYS + i, t)
            term = jnp.where(rank1_ref[hs, :] < cnt, b1_ref[hs, :] * a, jnp.zeros((), BF16))
            w = term if w is None else w + term
        rs = slice(ii * N_KEYS, (ii + 1) * N_KEYS)
        coef_ref[rs, :] = _gelu(act[rs, :]).astype(BF16) * w


def _peer_kernel(h2t_ref, u_ref, vtp_ref, vtc_ref, rank1_ref, b1_ref, cnt0_ref, a0_ref, x1_ref, gt2_ref, gf_ref,
                 y_ref, acc, coef_a, coef_b):
    k = pl.program_id(1)
    last = pl.num_programs(1) - 1
    route = (rank1_ref, b1_ref, cnt0_ref, a0_ref)

    def two_tiles():
        _peer_coef(2 * k, u_ref, 0, h2t_ref, *route, coef_a)
        _peer_coef(2 * k + 1, u_ref, PEER_E, h2t_ref, *route, coef_b)
        return jnp.dot(vtc_ref[...], coef_a[...], preferred_element_type=F32)

    @pl.when(k == 0)
    def _():
        acc[...] = two_tiles()

    @pl.when(jnp.logical_and(k > 0, k < last))
    def _():
        acc[...] += jnp.dot(vtp_ref[...], coef_b[...], preferred_element_type=F32)
        acc[...] += two_tiles()

    @pl.when(k == last)
    def _():
        tot = acc[...] + jnp.dot(vtp_ref[...], coef_b[...], preferred_element_type=F32)
        x2 = x1_ref[...] + gt2_ref[...] * jnp.transpose(tot)
        y_ref[...] = _rms(x2) * gf_ref[...]


def _peer(h2t, u_bf, vt_bf, rank1, b1, cnt0, a0, x1, gt2, gfin, per_token, rows_per_batch):
    n = x1.shape[0]
    t = PEER_T
    tpb = max(rows_per_batch // t, 1)
    rows = PEER_HEADS * N_KEYS
    steps = N_EXPERTS // (2 * PEER_E)
    once = pl.Buffered(1)
    tok = lambda r: pl.BlockSpec((None, r, t), lambda m, k: (m, 0, 0), pipeline_mode=once)
    if per_token:
        gt_spec = pl.BlockSpec((t, D_MODEL), lambda m, k: (m, 0), pipeline_mode=once)
    else:
        gt_spec = pl.BlockSpec((None, 1, D_MODEL), lambda m, k: (m // tpb, 0, 0))
    return pl.pallas_call(
        _peer_kernel,
        grid=(n // t, steps + 1),
        in_specs=[
            pl.BlockSpec((None, D_MODEL, t), lambda m, k: (m, 0, 0)),
            pl.BlockSpec((2 * PEER_E, D_MODEL), lambda m, k: (jnp.minimum(k, steps - 1), 0)),
            pl.BlockSpec((None, D_MODEL, PEER_E), lambda m, k: (jnp.maximum(2 * k - 1, 0), 0, 0)),
            pl.BlockSpec((None, D_MODEL, PEER_E), lambda m, k: (2 * jnp.minimum(k, steps - 1), 0, 0)),
            tok(rows), tok(rows), tok(rows), tok(rows),
            pl.BlockSpec((t, D_MODEL), lambda m, k: (m, 0), pipeline_mode=once),
            gt_spec,
            pl.BlockSpec((1, D_MODEL), lambda m, k: (0, 0)),
        ],
        out_specs=pl.BlockSpec((t, D_MODEL), lambda m, k: (m, 0)),
        out_shape=jax.ShapeDtypeStruct((n, D_MODEL), F32),
        scratch_shapes=[pltpu.VMEM((D_MODEL, t), F32), pltpu.VMEM((PEER_E, t), BF16), pltpu.VMEM((PEER_E, t), BF16)],
        compiler_params=_cparams(("parallel", "arbitrary"), VMEM_LIMIT_BIG),
        name="peer",
    )(h2t, u_bf, vt_bf, vt_bf, rank1, b1, cnt0, a0, x1, gt2, gfin)


def _rel_bias_tile(rel_bias, rows, cols, offset):
    rb = rel_bias.astype(F32)
    heads = rb.shape[0]
    rel_max = offset + rows - 1
    rel_min = offset - (cols - 1)
    lo, hi = max(rel_min, -REL_CLIP), min(rel_max, REL_CLIP)
    parts = []
    if rel_min < -REL_CLIP:
        parts.append(jnp.broadcast_to(rb[:, :1], (heads, -REL_CLIP - rel_min)))
    parts.append(rb[:, lo + REL_CLIP:hi + REL_CLIP + 1])
    if rel_max > REL_CLIP:
        parts.append(jnp.broadcast_to(rb[:, -1:], (heads, rel_max - REL_CLIP)))
    g = jnp.flip(jnp.concatenate(parts, axis=1), axis=1)
    period = rows + cols
    gp = jnp.pad(g, ((0, 0), (0, 1)))
    shifted = jnp.tile(gp, (1, rows))[:, :rows * (period - 1)].reshape(heads, rows, period - 1)
    return shifted[:, :, rows - 1:rows - 1 + cols]


def _prompt_bias(rel_bias):
    a = jnp.arange(Q_SUB, dtype=jnp.int32)[:, None]
    c = jnp.arange(K_WIN, dtype=jnp.int32)[None, :]
    cq = a // CHUNK
    ck = c // CHUNK
    ok = (ck >= cq) & (ck <= cq + N_LEFT_CHUNKS)
    return jnp.where(ok[None], _rel_bias_tile(rel_bias, Q_SUB, K_WIN, BAND_LEFT), NEG_INF)


def _sample_bias(rel_bias, seq, lc):
    return _rel_bias_tile(rel_bias, seq, lc, lc), _rel_bias_tile(rel_bias, seq, seq, 0)


def _layer(x, mod, cache_k, cache_v, state, is_sample, wts):
    (g1, w_main, w_alr, rel_bias, wup, balpha, gnorm, wa, wb, wo, g2, wqk_t, u_bf, vt_bf, gfin) = wts
    batch, seq, _ = x.shape
    n = batch * seq
    xf = x.reshape(n, D_MODEL)
    parts = [mod[:, i * D_MODEL:(i + 1) * D_MODEL] for i in range(6)]
    if is_sample:
        sh1, sc1, gt1, sh2, sc2, gt2 = [jnp.repeat(p, seq, axis=0) for p in parts]
        tm_in, tm_mg = n, 256
    else:
        sh1, sc1, gt1, sh2, sc2, gt2 = [p.reshape(batch, 1, D_MODEL) for p in parts]
        tm_in, tm_mg = 1024, 256

    z, alr = _inproj(xf, sh1, sc1, g1, w_main, w_alr, tm_in, is_sample, seq)

    if is_sample:
        lc = cache_k.shape[1]
        bias_c, bias_n = _sample_bias(rel_bias, seq, lc)
        oa = _attn_sample(z, cache_k.reshape(batch, lc * H_A, DH_A), cache_v.reshape(batch, lc * H_A, DH_A),
                          bias_c, bias_n, batch, seq)
        ob, s_new = _gla_sample(z, alr, state, wup, balpha, gnorm, batch, seq)
        k_rows = z[:, Z_KA:Z_KA + W_A].reshape(batch, seq, H_A, DH_A)
        v_rows = z[:, Z_VA:Z_VA + W_A].reshape(batch, seq, H_A, DH_A)
    else:
        oa = _attn_prompt(z, _prompt_bias(rel_bias), batch, seq)
        ob, s_new = _gla_prompt(z, alr, wup, balpha, gnorm, batch, seq)
        keep = min(BAND_LEFT, seq)
        zk = z.reshape(batch, seq, Z_WIDTH)[:, seq - keep:]
        k_rows = zk[:, :, Z_KA:Z_KA + W_A].reshape(batch, keep, H_A, DH_A)
        v_rows = zk[:, :, Z_VA:Z_VA + W_A].reshape(batch, keep, H_A, DH_A)

    x1, h2t = _merge(oa, ob, z, xf, gt1, sc2, sh2, g2, wa, wb, wo, tm_mg, is_sample, seq)
    rank1, b1, cnt0, a0 = _route(h2t, wqk_t)
    y = _peer(h2t, u_bf, vt_bf, rank1, b1, cnt0, a0, x1, gt2, gfin, is_sample, seq)
    return y.reshape(batch, seq, D_MODEL), k_rows, v_rows, s_new


def kernel(x_prompt, x_sample, cache_a_k, cache_a_v, state_gla, c_prompt, c_sample, w_ada, b_ada, g_norm1, w_in, rel_bias, w_alpha_up, b_alpha, g_gla_norm, w_branch_a, w_branch_b, w_out, g_norm2, w_peer_q, peer_subkeys, peer_u, peer_v, g_final):
    depth = w_in.shape[0]
    assert depth == 1, "the final rmsnorm is fused into the PEER kernel of the only layer"
    nbp = c_prompt.shape[0]
    l = 0
    nbs = c_sample.shape[0]
    pad = -(nbp + nbs) % 16
    c_all = jnp.concatenate([c_prompt, c_sample, jnp.zeros((pad, D_MODEL), F32)], axis=0)
    def only(w):
        return w.reshape(w.shape[1:])

    mod = _ada(c_all, only(w_ada), b_ada[l])

    alr_lo = Z_GL
    w_in0 = only(w_in)
    w_main = jnp.concatenate([w_in0[:, :alr_lo].astype(BF16), w_in0[:, alr_lo + ALPHA_RANK:].astype(BF16)], axis=1)
    w_alr = jnp.pad(w_in0[:, alr_lo:alr_lo + ALPHA_RANK], ((0, 0), (0, ALR_PAD - ALPHA_RANK))).astype(BF16)
    wup = jnp.pad(w_alpha_up[l], ((0, ALR_PAD - ALPHA_RANK), (0, 0))).astype(BF16)
    wqk = _qkfold(only(w_peer_q).astype(BF16), peer_subkeys[l].astype(BF16))
    wqk_t = wqk.reshape(D_MODEL, PEER_HEADS, 2, N_KEYS).transpose(2, 3, 1, 0).reshape(2 * PEER_HEADS * N_KEYS, D_MODEL)
    wts = (
        g_norm1[l].reshape(1, D_MODEL), w_main, w_alr, rel_bias[l], wup,
        b_alpha[l].reshape(1, KW_B), g_gla_norm[l].reshape(1, DV_B),
        only(w_branch_a).astype(BF16), only(w_branch_b).astype(BF16), only(w_out).astype(BF16),
        g_norm2[l].reshape(1, D_MODEL), wqk_t.astype(BF16),
        only(peer_u).astype(BF16),
        jnp.transpose(peer_v.astype(BF16).reshape(N_EXPERTS // PEER_E, PEER_E, D_MODEL), (0, 2, 1)),
        g_final.reshape(1, D_MODEL),
    )
    yp, kp, vp, sp = _layer(x_prompt, mod[:nbp], None, None, None, False, wts)
    ys, ks, vs, ss = _layer(x_sample, mod[nbp:nbp + nbs], only(cache_a_k), only(cache_a_v), only(state_gla), True, wts)
    return (yp, ys, kp[None], vp[None], sp[None], ks[None], vs[None], ss[None])
```

```python
import functools
import math

import jax
import jax.numpy as jnp
from jax import lax
from jax.experimental import pallas as pl
from jax.experimental.pallas import tpu as pltpu

F32 = jnp.float32
BF16 = jnp.bfloat16

D_MODEL = 2048
CHUNK = 64
N_LEFT_CHUNKS = 8
BAND_LEFT = N_LEFT_CHUNKS * CHUNK
H_A = 8
DH_A = 128
W_A = H_A * DH_A
REL_CLIP = 128
H_B = 4
DK_B = 128
DV_B = 256
KW_B = H_B * DK_B
VW_B = H_B * DV_B
ALPHA_RANK = 16
GATE_TAU = 16.0
PEER_HEADS = 8
N_KEYS = 128
N_EXPERTS = N_KEYS * N_KEYS
PEER_TOPK = 16
EPS = 1e-6
NEG_INF = -1e30

Z_QA, Z_KA, Z_VA = 0, W_A, 2 * W_A
Z_QB = 3 * W_A
Z_KB = Z_QB + KW_B
Z_VB = Z_KB + KW_B
Z_RB = Z_VB + VW_B
Z_GL = Z_RB + VW_B
Z_WIDTH = Z_GL + 2 * D_MODEL
ALR_PAD = 128

VMEM_LIMIT_BIG = 56 * 1024 * 1024
VMEM_LIMIT_MID = 40 * 1024 * 1024

Q_TILE = 512
Q_SUB = 128
K_WIN = Q_SUB + BAND_LEFT


def _cparams(sem, vmem=None):
    return pltpu.CompilerParams(dimension_semantics=sem, vmem_limit_bytes=vmem)


def _rms(xf):
    return xf * lax.rsqrt(jnp.mean(xf * xf, axis=-1, keepdims=True) + EPS)


def _sigmoid(x):
    return 1.0 / (1.0 + jnp.exp(-x))


def _ada_kernel(c_ref, w_ref, b_ref, o_ref):
    c = c_ref[...]
    a = (c * _sigmoid(c)).astype(BF16)
    o_ref[...] = jnp.dot(a, w_ref[...].astype(BF16), preferred_element_type=F32) + b_ref[...]


def _ada(c_all, w_ada, b_ada):
    nb = c_all.shape[0]
    n_out = w_ada.shape[1]
    tn = 1024
    return pl.pallas_call(
        _ada_kernel,
        grid=(n_out // tn,),
        in_specs=[
            pl.BlockSpec((nb, D_MODEL), lambda j: (0, 0)),
            pl.BlockSpec((D_MODEL, tn), lambda j: (0, j)),
            pl.BlockSpec((1, tn), lambda j: (0, j)),
        ],
        out_specs=pl.BlockSpec((nb, tn), lambda j: (0, j)),
        out_shape=jax.ShapeDtypeStruct((nb, n_out), F32),
        compiler_params=_cparams(("arbitrary",), VMEM_LIMIT_MID),
        name="ada",
    )(c_all, w_ada, b_ada.reshape(1, n_out))


def _mod_spec(per_token, tm, tiles_per_batch, grid_rank):
    if per_token:
        if grid_rank == 1:
            return pl.BlockSpec((tm, D_MODEL), lambda m: (m, 0))
        return pl.BlockSpec((tm, D_MODEL), lambda m, n: (m, 0))
    if grid_rank == 1:
        return pl.BlockSpec((None, 1, D_MODEL), lambda m: (m // tiles_per_batch, 0, 0))
    return pl.BlockSpec((None, 1, D_MODEL), lambda m, n: (m // tiles_per_batch, 0, 0))


def _inproj_kernel(x_ref, sh_ref, sc_ref, g_ref, w_ref, walr_ref, z_ref, alr_ref, h_scr):
    @pl.when(pl.program_id(1) == 0)
    def _():
        h = _rms(x_ref[...]) * g_ref[...]
        h = h * (1.0 + sc_ref[...]) + sh_ref[...]
        hb = h.astype(BF16)
        h_scr[...] = hb
        alr_ref[...] = jnp.dot(hb, walr_ref[...], preferred_element_type=F32)

    z_ref[...] = jnp.dot(h_scr[...], w_ref[...], preferred_element_type=F32)


def _inproj(x, sh, sc, g, w_main, w_alr, tm, per_token, rows_per_batch):
    n = x.shape[0]
    tn = 1024
    tpb = max(rows_per_batch // tm, 1)
    return pl.pallas_call(
        _inproj_kernel,
        grid=(n // tm, Z_WIDTH // tn),
        in_specs=[
            pl.BlockSpec((tm, D_MODEL), lambda m, j: (m, 0)),
            _mod_spec(per_token, tm, tpb, 2),
            _mod_spec(per_token, tm, tpb, 2),
            pl.BlockSpec((1, D_MODEL), lambda m, j: (0, 0)),
            pl.BlockSpec((D_MODEL, tn), lambda m, j: (0, j)),
            pl.BlockSpec((D_MODEL, ALR_PAD), lambda m, j: (0, 0)),
        ],
        out_specs=[
            pl.BlockSpec((tm, tn), lambda m, j: (m, j)),
            pl.BlockSpec((tm, ALR_PAD), lambda m, j: (m, 0)),
        ],
        out_shape=[
            jax.ShapeDtypeStruct((n, Z_WIDTH), F32),
            jax.ShapeDtypeStruct((n, ALR_PAD), F32),
        ],
        scratch_shapes=[pltpu.VMEM((tm, D_MODEL), BF16)],
        compiler_params=_cparams(("parallel", "arbitrary"), VMEM_LIMIT_BIG),
        name="inproj",
    )(x, sh, sc, g, w_main, w_alr)


def _attn_prompt_kernel(q_ref, kp_ref, kc_ref, vp_ref, vc_ref, bias_ref, o_ref, kw, vw):
    first = pl.program_id(2) == 0
    kw[0:Q_TILE, :] = kp_ref[...].astype(BF16)
    kw[Q_TILE:2 * Q_TILE, :] = kc_ref[...].astype(BF16)
    vw[0:Q_TILE, :] = vp_ref[...].astype(BF16)
    vw[Q_TILE:2 * Q_TILE, :] = vc_ref[...].astype(BF16)
    scale = DH_A ** -0.5
    probs_ = [(hh, r) for hh in range(ATT_HEADS) for r in range(Q_TILE // Q_SUB)]
    hcol = [slice(hh * DH_A, (hh + 1) * DH_A) for hh in range(ATT_HEADS)]
    scores = [lax.dot_general((q_ref[r * Q_SUB:(r + 1) * Q_SUB, hcol[hh]] * scale).astype(BF16),
                              kw[r * Q_SUB:r * Q_SUB + K_WIN, hcol[hh]],
                              (((1,), (1,)), ((), ())), preferred_element_type=F32) for (hh, r) in probs_]
    probs, sums = [], []
    col = lax.broadcasted_iota(jnp.int32, (1, K_WIN), 1)
    for idx, (hh, r) in enumerate(probs_):
        before = jnp.where(jnp.logical_and(first, col + r * Q_SUB < Q_TILE), NEG_INF, 0.0)
        s = scores[idx] + bias_ref[hh] + before
        m = jnp.max(s, axis=-1, keepdims=True)
        p = jnp.exp(s - m)
        sums.append(jnp.sum(p, axis=-1, keepdims=True))
        probs.append(p.astype(BF16))
    outs = [jnp.dot(probs[idx], vw[r * Q_SUB:r * Q_SUB + K_WIN, hcol[hh]], preferred_element_type=F32)
            for idx, (hh, r) in enumerate(probs_)]
    for idx, (hh, r) in enumerate(probs_):
        o_ref[r * Q_SUB:(r + 1) * Q_SUB, hcol[hh]] = (outs[idx] / sums[idx]).astype(o_ref.dtype)


ATT_HEADS = 2


def _attn_prompt(z, bias, batch, seq):
    n = batch * seq
    tiles = seq // Q_TILE
    width = ATT_HEADS * DH_A

    def cur(col):
        return pl.BlockSpec((Q_TILE, width), lambda h, b, i: (b * tiles + i, col + h))

    def prev(col):
        return pl.BlockSpec((Q_TILE, width),
                            lambda h, b, i: (b * tiles + jnp.maximum(i - 1, 0), col + h))

    kcol, vcol = Z_KA // width, Z_VA // width
    return pl.pallas_call(
        _attn_prompt_kernel,
        grid=(H_A // ATT_HEADS, batch, tiles),
        in_specs=[cur(0), prev(kcol), cur(kcol), prev(vcol), cur(vcol),
                  pl.BlockSpec((ATT_HEADS, Q_SUB, K_WIN), lambda h, b, i: (h, 0, 0))],
        out_specs=pl.BlockSpec((Q_TILE, width), lambda h, b, i: (b * tiles + i, h)),
        out_shape=jax.ShapeDtypeStruct((n, W_A), BF16),
        scratch_shapes=[pltpu.VMEM((2 * Q_TILE, width), BF16), pltpu.VMEM((2 * Q_TILE, width), BF16)],
        compiler_params=_cparams(("arbitrary", "arbitrary", "arbitrary")),
        name="attn_prompt",
    )(z, z, z, z, z, bias)


def _attn_sample_kernel(q_ref, kn_ref, vn_ref, ck_ref, cv_ref, bc_ref, bn_ref, o_ref):
    scale = DH_A ** -0.5
    nt = (((1,), (1,)), ((), ()))
    lc = ck_ref.shape[0] // H_A
    heads = range(H_A)
    cols = [slice(h * DH_A, (h + 1) * DH_A) for h in heads]
    crow = [pl.ds(h, lc, stride=H_A) for h in heads]
    qs = [q_ref[:, cols[h]].astype(BF16) for h in heads]
    s_c = [lax.dot_general(qs[h], ck_ref[crow[h], :].astype(BF16), nt, preferred_element_type=F32) for h in heads]
    s_n = [lax.dot_general(qs[h], kn_ref[:, cols[h]].astype(BF16), nt, preferred_element_type=F32) for h in heads]
    p_c, p_n, sums = [], [], []
    for h in heads:
        sc = s_c[h] * scale + bc_ref[h]
        sn = s_n[h] * scale + bn_ref[h]
        m = jnp.maximum(jnp.max(sc, axis=-1, keepdims=True), jnp.max(sn, axis=-1, keepdims=True))
        pc = jnp.exp(sc - m)
        pn = jnp.exp(sn - m)
        sums.append(jnp.sum(pc, axis=-1, keepdims=True) + jnp.sum(pn, axis=-1, keepdims=True))
        p_c.append(pc.astype(BF16))
        p_n.append(pn.astype(BF16))
    outs = [jnp.dot(p_c[h], cv_ref[crow[h], :].astype(BF16), preferred_element_type=F32)
            + jnp.dot(p_n[h], vn_ref[:, cols[h]].astype(BF16), preferred_element_type=F32) for h in heads]
    for h in heads:
        o_ref[:, cols[h]] = (outs[h] / sums[h]).astype(o_ref.dtype)


def _attn_sample(z, cache_k, cache_v, bias_c, bias_n, batch, seq):
    lc = cache_k.shape[1] // H_A
    return pl.pallas_call(
        _attn_sample_kernel,
        grid=(batch,),
        in_specs=[
            pl.BlockSpec((seq, W_A), lambda b: (b, Z_QA // W_A)),
            pl.BlockSpec((seq, W_A), lambda b: (b, Z_KA // W_A)),
            pl.BlockSpec((seq, W_A), lambda b: (b, Z_VA // W_A)),
            pl.BlockSpec((None, lc * H_A, DH_A), lambda b: (b, 0, 0)),
            pl.BlockSpec((None, lc * H_A, DH_A), lambda b: (b, 0, 0)),
            pl.BlockSpec((H_A, seq, lc), lambda b: (0, 0, 0)),
            pl.BlockSpec((H_A, seq, seq), lambda b: (0, 0, 0)),
        ],
        out_specs=pl.BlockSpec((seq, W_A), lambda b: (b, 0)),
        out_shape=jax.ShapeDtypeStruct((batch * seq, W_A), BF16),
        compiler_params=_cparams(("arbitrary",), VMEM_LIMIT_MID),
        name="attn_sample",
    )(z, z, z, cache_k, cache_v, bias_c, bias_n)


def _split_bf16(x):
    hi = x.astype(BF16)
    lo = (x - hi.astype(F32)).astype(BF16)
    return hi, lo


_TN = (((0,), (0,)), ((), ()))
_NT = (((1,), (1,)), ((), ()))


def _gla_state_free(units, tri, causal, c):
    scale = DK_B ** -0.5
    ones = jnp.ones((c, DK_B), BF16)
    split = [_split_bf16(g) for (_, _, _, g) in units]
    cum = [jnp.dot(tri, hi, preferred_element_type=F32) + jnp.dot(tri, lo, preferred_element_type=F32)
           for (hi, lo) in split]
    tot = [lax.dot_general(hi, ones, _TN, preferred_element_type=F32)
           + lax.dot_general(lo, ones, _TN, preferred_element_type=F32) for (hi, lo) in split]
    qd, kd, vb, att = [], [], [], []
    for (q, k, v, _), b in zip(units, cum):
        mid = b[c // 2:c // 2 + 1, :]
        blast = b[c - 1:c, :]
        qs = q * scale
        ks = k * scale
        att.append(lax.dot_general((qs * jnp.exp(b - mid)).astype(BF16), (ks * jnp.exp(mid - b)).astype(BF16),
                                   _NT, preferred_element_type=F32))
        qd.append((qs * jnp.exp(b)).astype(BF16))
        kd.append((ks * jnp.exp(blast - b)).astype(BF16))
        vb.append(v.astype(BF16))
    intra = [jnp.dot(jnp.where(causal, a, 0.0).astype(BF16), v, preferred_element_type=F32)
             for a, v in zip(att, vb)]
    out = []
    for i in range(len(units)):
        dec = jnp.exp(tot[i])
        out.append((intra[i], qd[i], kd[i], vb[i], jnp.concatenate([dec, dec], axis=1)))
    return out


def _gla_read_state(pre, s_prev):
    o_intra, qd, _, _, _ = pre
    return o_intra + jnp.dot(qd, s_prev.astype(BF16), preferred_element_type=F32)


def _gla_next_state(pre, s_prev):
    _, _, kd, vb, dec = pre
    return dec * s_prev + lax.dot_general(kd, vb, _TN, preferred_element_type=F32)


def _log_decay(alr, wup, balpha):
    x = jnp.dot(alr.astype(BF16), wup, preferred_element_type=F32) + balpha
    return (jnp.minimum(x, 0.0) - jnp.log1p(jnp.exp(-jnp.abs(x)))) * (1.0 / GATE_TAU)


def _gla_epilogue(o, rb, gnorm):
    y = _rms(o) * gnorm
    return y * (rb * _sigmoid(rb))


def _tri_and_causal(c):
    row = lax.broadcasted_iota(jnp.int32, (c, c), 0)
    col = lax.broadcasted_iota(jnp.int32, (c, c), 1)
    causal = col <= row
    return jnp.where(causal, 1.0, 0.0).astype(BF16), causal


def _gla_prompt_kernel(q_ref, k_ref, v_ref, rb_ref, alr_ref, wup_ref, balpha_ref, gnorm_ref,
                       ob_ref, st_ref, s_scr, *, batch, blocks):
    step = pl.program_id(0)

    @pl.when(step == 0)
    def _():
        s_scr[...] = jnp.zeros_like(s_scr)

    tri, causal = _tri_and_causal(CHUNK)
    wup = wup_ref[...]
    balpha = balpha_ref[...]
    gnorm = gnorm_ref[...]
    ksl = [slice(h * DK_B, (h + 1) * DK_B) for h in range(H_B)]
    vsl = [slice(h * DV_B, (h + 1) * DV_B) for h in range(H_B)]
    streams = [(bi, h) for bi in range(batch) for h in range(H_B)]
    rows = [slice(blk * CHUNK, (blk + 1) * CHUNK) for blk in range(blocks)]
    logg = {(blk, bi): _log_decay(alr_ref[bi, rows[blk], :], wup, balpha)
            for blk in range(blocks) for bi in range(batch)}
    units = [(q_ref[bi, rows[blk], ksl[h]], k_ref[bi, rows[blk], ksl[h]], v_ref[bi, rows[blk], vsl[h]],
              logg[(blk, bi)][:, ksl[h]]) for blk in range(blocks) for (bi, h) in streams]
    pre = _gla_state_free(units, tri, causal, CHUNK)
    for blk in range(blocks):
        mine = pre[blk * len(streams):(blk + 1) * len(streams)]
        outs = [_gla_read_state(p, s_scr[si]) for si, p in enumerate(mine)]
        nxt = [_gla_next_state(p, s_scr[si]) for si, p in enumerate(mine)]
        for si, (bi, h) in enumerate(streams):
            s_scr[si] = nxt[si]
            ob_ref[bi, rows[blk], vsl[h]] = _gla_epilogue(outs[si], rb_ref[bi, rows[blk], vsl[h]],
                                                          gnorm).astype(ob_ref.dtype)

    @pl.when(step == pl.num_programs(0) - 1)
    def _():
        st_ref[...] = s_scr[...]


def _gla_prompt(z, alr, wup, balpha, gnorm, batch, seq):
    blocks = 4
    rows = blocks * CHUNK
    z3 = z.reshape(batch, seq, Z_WIDTH)
    alr3 = alr.reshape(batch, seq, ALR_PAD)
    ob, st = pl.pallas_call(
        functools.partial(_gla_prompt_kernel, batch=batch, blocks=blocks),
        grid=(seq // rows,),
        in_specs=[
            pl.BlockSpec((batch, rows, KW_B), lambda j: (0, j, Z_QB // KW_B)),
            pl.BlockSpec((batch, rows, KW_B), lambda j: (0, j, Z_KB // KW_B)),
            pl.BlockSpec((batch, rows, VW_B), lambda j: (0, j, Z_VB // VW_B)),
            pl.BlockSpec((batch, rows, VW_B), lambda j: (0, j, Z_RB // VW_B)),
            pl.BlockSpec((batch, rows, ALR_PAD), lambda j: (0, j, 0)),
            pl.BlockSpec((ALR_PAD, KW_B), lambda j: (0, 0)),
            pl.BlockSpec((1, KW_B), lambda j: (0, 0)),
            pl.BlockSpec((1, DV_B), lambda j: (0, 0)),
        ],
        out_specs=[
            pl.BlockSpec((batch, rows, VW_B), lambda j: (0, j, 0)),
            pl.BlockSpec((batch * H_B, DK_B, DV_B), lambda j: (0, 0, 0)),
        ],
        out_shape=[
            jax.ShapeDtypeStruct((batch, seq, VW_B), BF16),
            jax.ShapeDtypeStruct((batch * H_B, DK_B, DV_B), F32),
        ],
        scratch_shapes=[pltpu.VMEM((batch * H_B, DK_B, DV_B), F32)],
        compiler_params=_cparams(("arbitrary",), VMEM_LIMIT_MID),
        name="gla_prompt",
    )(z3, z3, z3, z3, alr3, wup, balpha, gnorm)
    return ob.reshape(batch * seq, VW_B), st.reshape(batch, H_B, DK_B, DV_B)


def _gla_sample_kernel(q_ref, k_ref, v_ref, rb_ref, alr_ref, s0_ref, wup_ref, balpha_ref, gnorm_ref,
                       ob_ref, st_ref, *, seq):
    tri, causal = _tri_and_causal(seq)
    logg = _log_decay(alr_ref[...], wup_ref[...], balpha_ref[...])
    gnorm = gnorm_ref[...]
    ksl = [slice(h * DK_B, (h + 1) * DK_B) for h in range(H_B)]
    vsl = [slice(h * DV_B, (h + 1) * DV_B) for h in range(H_B)]
    pre = _gla_state_free([(q_ref[:, ksl[h]], k_ref[:, ksl[h]], v_ref[:, vsl[h]], logg[:, ksl[h]])
                           for h in range(H_B)], tri, causal, seq)
    outs = [_gla_read_state(pre[h], s0_ref[h]) for h in range(H_B)]
    nxt = [_gla_next_state(pre[h], s0_ref[h]) for h in range(H_B)]
    for h in range(H_B):
        st_ref[h] = nxt[h]
        ob_ref[:, vsl[h]] = _gla_epilogue(outs[h], rb_ref[:, vsl[h]], gnorm).astype(ob_ref.dtype)


def _gla_sample(z, alr, state, wup, balpha, gnorm, batch, seq):
    return pl.pallas_call(
        functools.partial(_gla_sample_kernel, seq=seq),
        grid=(batch,),
        in_specs=[
            pl.BlockSpec((seq, KW_B), lambda b: (b, Z_QB // KW_B)),
            pl.BlockSpec((seq, KW_B), lambda b: (b, Z_KB // KW_B)),
            pl.BlockSpec((seq, VW_B), lambda b: (b, Z_VB // VW_B)),
            pl.BlockSpec((seq, VW_B), lambda b: (b, Z_RB // VW_B)),
            pl.BlockSpec((seq, ALR_PAD), lambda b: (b, 0)),
            pl.BlockSpec((None, H_B, DK_B, DV_B), lambda b: (b, 0, 0, 0)),
            pl.BlockSpec((ALR_PAD, KW_B), lambda b: (0, 0)),
            pl.BlockSpec((1, KW_B), lambda b: (0, 0)),
            pl.BlockSpec((1, DV_B), lambda b: (0, 0)),
        ],
        out_specs=[
            pl.BlockSpec((seq, VW_B), lambda b: (b, 0)),
            pl.BlockSpec((None, H_B, DK_B, DV_B), lambda b: (b, 0, 0, 0)),
        ],
        out_shape=[
            jax.ShapeDtypeStruct((batch * seq, VW_B), BF16),
            jax.ShapeDtypeStruct((batch, H_B, DK_B, DV_B), F32),
        ],
        compiler_params=_cparams(("arbitrary",)),
        name="gla_sample",
    )(z, z, z, z, alr, state, wup, balpha, gnorm)


def _merge_kernel(oa_ref, ob_ref, gla_ref, glb_ref, x_ref, gt1_ref, sc2_ref, sh2_ref, g2_ref,
                  wa_ref, wb_ref, wo_ref, x1_ref, h2t_ref):
    ya = jnp.dot(oa_ref[...], wa_ref[...], preferred_element_type=F32)
    yb = jnp.dot(ob_ref[...], wb_ref[...], preferred_element_type=F32)
    merged = _sigmoid(gla_ref[...]) * ya + _sigmoid(glb_ref[...]) * yb
    x1 = x_ref[...] + gt1_ref[...] * jnp.dot(merged.astype(BF16), wo_ref[...], preferred_element_type=F32)
    x1_ref[...] = x1
    h2 = _rms(x1) * g2_ref[...]
    h2 = h2 * (1.0 + sc2_ref[...]) + sh2_ref[...]
    h2t_ref[...] = jnp.transpose(h2).astype(BF16)


def _merge(oa, ob, z, x, gt1, sc2, sh2, g2, wa, wb, wo, tm, per_token, rows_per_batch):
    n = x.shape[0]
    tpb = max(rows_per_batch // tm, 1)
    const = lambda shape: pl.BlockSpec(shape, lambda m: (0, 0))
    mod = _mod_spec(per_token, tm, tpb, 1)
    return pl.pallas_call(
        _merge_kernel,
        grid=(n // tm,),
        in_specs=[
            pl.BlockSpec((tm, W_A), lambda m: (m, 0)),
            pl.BlockSpec((tm, VW_B), lambda m: (m, 0)),
            pl.BlockSpec((tm, D_MODEL), lambda m: (m, Z_GL // D_MODEL)),
            pl.BlockSpec((tm, D_MODEL), lambda m: (m, Z_GL // D_MODEL + 1)),
            pl.BlockSpec((tm, D_MODEL), lambda m: (m, 0)),
            mod, mod, mod,
            const((1, D_MODEL)),
            const((W_A, D_MODEL)), const((VW_B, D_MODEL)), const((D_MODEL, D_MODEL)),
        ],
        out_specs=[
            pl.BlockSpec((tm, D_MODEL), lambda m: (m, 0)),
            pl.BlockSpec((D_MODEL, tm), lambda m: (0, m)),
        ],
        out_shape=[
            jax.ShapeDtypeStruct((n, D_MODEL), F32),
            jax.ShapeDtypeStruct((D_MODEL, n), BF16),
        ],
        compiler_params=_cparams(("arbitrary",), VMEM_LIMIT_BIG),
        name="merge",
    )(oa, ob, z, z, x, gt1, sc2, sh2, g2, wa, wb, wo)


def _qkfold_kernel(wq_ref, sk_ref, o_ref):
    o_ref[...] = lax.dot_general(wq_ref[...], sk_ref[...], (((1,), (1,)), ((), ())),
                                 preferred_element_type=F32)


def _qkfold(w_q, subkeys):
    half = N_KEYS
    return pl.pallas_call(
        _qkfold_kernel,
        grid=(2 * PEER_HEADS,),
        in_specs=[
            pl.BlockSpec((D_MODEL, half), lambda c: (0, c)),
            pl.BlockSpec((None, N_KEYS, half), lambda c: (c % 2, 0, 0)),
        ],
        out_specs=pl.BlockSpec((D_MODEL, N_KEYS), lambda c: (0, c)),
        out_shape=jax.ShapeDtypeStruct((D_MODEL, 2 * PEER_HEADS * N_KEYS), F32),
        compiler_params=_cparams(("arbitrary",)),
        name="qkfold",
    )(w_q, subkeys)


ROUTE_T = 256
LANES = 128
_SET_ROWS = N_KEYS * PEER_HEADS
_PAIRS = [(r, q) for r in range(PEER_TOPK) for q in range(PEER_TOPK) if (r + 1) * (q + 1) <= PEER_TOPK]


def _sort_desc(x):
    x = list(x)
    n = len(x)
    k = 2
    while k <= n:
        j = k // 2
        while j >= 1:
            for i in range(n):
                l = i ^ j
                if l > i:
                    hi, lo = jnp.maximum(x[i], x[l]), jnp.minimum(x[i], x[l])
                    x[i], x[l] = (hi, lo) if (i & k) == 0 else (lo, hi)
            j //= 2
        k *= 2
    return x


def _merge_top(a, b):
    n = len(a)
    x = [jnp.maximum(a[i], b[n - 1 - i]) for i in range(n)]
    j = n // 2
    while j >= 1:
        for i in range(n):
            l = i ^ j
            if l > i:
                x[i], x[l] = jnp.maximum(x[i], x[l]), jnp.minimum(x[i], x[l])
        j //= 2
    return x


def _top_sorted(vals, top):
    groups = [_sort_desc(vals[g:g + top]) for g in range(0, len(vals), top)]
    while len(groups) > 1:
        groups = [_merge_top(groups[g], groups[g + 1]) for g in range(0, len(groups), 2)]
    return groups[0]


def _route_fast(s0, s1):
    shp = s0[0].shape
    one = jnp.ones(shp, F32)
    zero = jnp.zeros(shp, F32)
    top0 = _top_sorted(s0, PEER_TOPK)
    top1 = _top_sorted(s1, PEER_TOPK)
    cand = [top0[r] + top1[q] for (r, q) in _PAIRS]
    pad = [jnp.full(shp, -jnp.inf, F32)] * (-len(cand) % PEER_TOPK)
    topc = _top_sorted(cand + pad, PEER_TOPK)
    tau = topc[PEER_TOPK - 1]
    cnt = [zero] * PEER_TOPK
    for c, (r, q) in enumerate(_PAIRS):
        cnt[r] = cnt[r] + jnp.where(cand[c] >= tau, one, zero)
    picked = cnt[0]
    for r in range(1, PEER_TOPK):
        picked = picked + cnt[r]
    zsum = zero
    for r in range(PEER_TOPK):
        zsum = zsum + jnp.exp(topc[r] - topc[0])
    zinv = 1.0 / zsum
    tie = jnp.where(picked != float(PEER_TOPK), one, zero)
    for t in (top0, top1):
        for r in range(PEER_TOPK - 1):
            tie = jnp.where(t[r] == t[r + 1], one, tie)
    in0, in1 = zero, zero
    rank1, b1, cnt0, a0 = [], [], [], []
    for k in range(N_KEYS):
        v0, v1 = s0[k], s1[k]
        c0 = zero
        rk = jnp.full(shp, float(PEER_TOPK), F32)
        for r in range(PEER_TOPK - 1, -1, -1):
            c0 = jnp.where(v0 >= top0[r], cnt[r], c0)
            rk = jnp.where(v1 >= top1[r], float(r), rk)
        in0 = in0 + jnp.where(v0 >= top0[PEER_TOPK - 1], one, zero)
        in1 = in1 + jnp.where(v1 >= top1[PEER_TOPK - 1], one, zero)
        cnt0.append(c0)
        rank1.append(rk)
        a0.append(jnp.exp(v0 - top0[0]) * zinv)
        b1.append(jnp.exp(v1 - top1[0]))
    tie = jnp.where(in0 != float(PEER_TOPK), one, tie)
    tie = jnp.where(in1 != float(PEER_TOPK), one, tie)
    return rank1, b1, cnt0, a0, tie


def _route_kernel(h2t_ref, wqk_ref, rank1_ref, b1_ref, cnt0_ref, a0_ref,
                  s_scr, sw_scr, rk_scr, val_scr, idx_scr, tmp_scr):
    halves = h2t_ref.shape[1] // LANES
    shp = (PEER_HEADS, LANES)
    s = jnp.dot(wqk_ref[...], h2t_ref[...], preferred_element_type=F32)
    for hf in range(halves):
        s_scr[hf] = s[:, hf * LANES:(hf + 1) * LANES]
    neg = jnp.full(shp, -jnp.inf, F32)

    def rows(p, k):
        return pl.ds(p * _SET_ROWS + k * PEER_HEADS, PEER_HEADS)

    def one_half(hf, carry0):
        f_rank1, f_b1, f_cnt0, f_a0, tie = _route_fast([s_scr[hf, rows(0, k), :] for k in range(N_KEYS)],
                                                       [s_scr[hf, rows(1, k), :] for k in range(N_KEYS)])
        for k in range(N_KEYS):
            kr = pl.ds(k * PEER_HEADS, PEER_HEADS)
            tmp_scr[hf, 0, kr, :] = f_cnt0[k]
            tmp_scr[hf, 1, kr, :] = f_a0[k]
            tmp_scr[hf, 2, kr, :] = f_b1[k]
            rk_scr[hf, rows(1, k), :] = f_rank1[k]

        @pl.when(jnp.max(tie) > 0.0)
        def _():
            exact_half(hf)

        return carry0

    def exact_half(hf):
        sw_scr[hf] = s_scr[hf]
        rk_scr[hf] = jnp.full(rk_scr.shape[1:], float(PEER_TOPK), F32)

        def extract(r, carry):
            rf = jnp.full(shp, r, jnp.int32).astype(F32)
            for p in range(2):
                best, bidx = None, None
                for part in range(4):
                    m = neg
                    ix = jnp.zeros(shp, jnp.int32)
                    for k in range(part * 32, part * 32 + 32):
                        v = sw_scr[hf, rows(p, k), :]
                        gt = v > m
                        m = jnp.where(gt, v, m)
                        ix = jnp.where(gt, k, ix)
                    if best is None:
                        best, bidx = m, ix
                    else:
                        gt = m > best
                        best = jnp.where(gt, m, best)
                        bidx = jnp.where(gt, ix, bidx)
                val_scr[p, r] = best
                idx_scr[p, r] = bidx
                for k in range(N_KEYS):
                    hit = bidx == k
                    sw_scr[hf, rows(p, k), :] = jnp.where(hit, -jnp.inf, sw_scr[hf, rows(p, k), :])
                    rk_scr[hf, rows(p, k), :] = jnp.where(hit, rf, rk_scr[hf, rows(p, k), :])
            return carry

        lax.fori_loop(0, PEER_TOPK, extract, 0)

        v0 = [val_scr[0, r] for r in range(PEER_TOPK)]
        v1 = [val_scr[1, q] for q in range(PEER_TOPK)]
        cand = [v0[r] + v1[q] for (r, q) in _PAIRS]
        top = v0[0] + v1[0]
        cnt = [jnp.zeros(shp, F32) for _ in range(PEER_TOPK)]
        zsum = jnp.zeros(shp, F32)
        for _ in range(PEER_TOPK):
            m = neg
            ix = jnp.zeros(shp, jnp.int32)
            for c, cv in enumerate(cand):
                gt = cv > m
                m = jnp.where(gt, cv, m)
                ix = jnp.where(gt, c, ix)
            zsum = zsum + jnp.exp(m - top)
            for c, (r, q) in enumerate(_PAIRS):
                hit = ix == c
                cand[c] = jnp.where(hit, -jnp.inf, cand[c])
                cnt[r] = cnt[r] + jnp.where(hit, 1.0, 0.0)
        zinv = 1.0 / zsum

        idx0 = [idx_scr[0, r] for r in range(PEER_TOPK)]
        for k in range(N_KEYS):
            c0 = jnp.zeros(shp, F32)
            for r in range(PEER_TOPK):
                c0 = jnp.where(idx0[r] == k, cnt[r], c0)
            kr = pl.ds(k * PEER_HEADS, PEER_HEADS)
            tmp_scr[hf, 0, kr, :] = c0
            tmp_scr[hf, 1, kr, :] = jnp.exp(s_scr[hf, rows(0, k), :] - v0[0]) * zinv
            tmp_scr[hf, 2, kr, :] = jnp.exp(s_scr[hf, rows(1, k), :] - v1[0])

    lax.fori_loop(0, halves, one_half, 0)

    for hf in range(halves):
        ls = slice(hf * LANES, (hf + 1) * LANES)
        for h in range(PEER_HEADS):
            dst = slice(h * N_KEYS, (h + 1) * N_KEYS)
            src = pl.ds(h, N_KEYS, stride=PEER_HEADS)
            cnt0_ref[dst, ls] = _pack_pair(tmp_scr[hf, 0, src, :])
            a0_ref[dst, ls] = _pack_pair(tmp_scr[hf, 1, src, :])
            b1_ref[dst, ls] = tmp_scr[hf, 2, src, :].astype(BF16)
            rank1_ref[dst, ls] = rk_scr[hf, pl.ds(_SET_ROWS + h, N_KEYS, stride=PEER_HEADS), :].astype(BF16)


def _pack_pair(x):
    bits = pltpu.bitcast(x.astype(BF16).astype(F32), jnp.uint32)
    return bits | (bits >> 16)


def _route(h2t, wqk_t):
    n = h2t.shape[1]
    t = ROUTE_T
    rows = PEER_HEADS * N_KEYS
    ospec = pl.BlockSpec((rows, t), lambda m: (0, m))
    out_bf = jax.ShapeDtypeStruct((rows, n), BF16)
    out_pk = jax.ShapeDtypeStruct((rows, n), jnp.uint32)
    return pl.pallas_call(
        _route_kernel,
        grid=(n // t,),
        in_specs=[
            pl.BlockSpec((D_MODEL, t), lambda m: (0, m)),
            pl.BlockSpec((2 * rows, D_MODEL), lambda m: (0, 0)),
        ],
        out_specs=[ospec, ospec, ospec, ospec],
        out_shape=[out_bf, out_bf, out_pk, out_pk],
        scratch_shapes=[
            pltpu.VMEM((t // LANES, 2 * rows, LANES), F32),
            pltpu.VMEM((t // LANES, 2 * rows, LANES), F32),
            pltpu.VMEM((t // LANES, 2 * rows, LANES), F32),
            pltpu.VMEM((2, PEER_TOPK, PEER_HEADS, LANES), F32),
            pltpu.VMEM((2, PEER_TOPK, PEER_HEADS, LANES), jnp.int32),
            pltpu.VMEM((t // LANES, 3, rows, LANES), F32),
        ],
        compiler_params=_cparams(("arbitrary",), VMEM_LIMIT_MID),
        name="route",
    )(h2t, wqk_t)


PEER_T = 512
PEER_E = 512


def _gelu(x):
    return 0.5 * x * (1.0 + lax.erf(x * (2.0 ** -0.5)))


def _bcast_pair_rows(ref, row, t):
    word = jnp.broadcast_to(ref[pl.ds(row, 1), :], (8, t))
    pair = pltpu.bitcast(word, BF16)
    return jnp.broadcast_to(pair[None], (N_KEYS // 16, 16, t)).reshape(N_KEYS, t)


def _peer_coef(tile, u_ref, row0, h2t_ref, rank1_ref, b1_ref, cnt0_ref, a0_ref, coef_ref):
    t = h2t_ref.shape[1]
    act = jnp.dot(u_ref[row0:row0 + PEER_E, :], h2t_ref[...], preferred_element_type=F32)
    per = PEER_E // N_KEYS
    for ii in range(per):
        i = tile * per + ii
        w = None
        for h in range(PEER_HEADS):
            hs = slice(h * N_KEYS, (h + 1) * N_KEYS)
            cnt = _bcast_pair_rows(cnt0_ref, h * N_KEYS + i, t)
            a = _bcast_pair_rows(a0_ref, h * N_KEYS + i, t)
            term = jnp.where(rank1_ref[hs, :] < cnt, b1_ref[hs, :] * a, jnp.zeros((), BF16))
            w = term if w is None else w + term
        rs = slice(ii * N_KEYS, (ii + 1) * N_KEYS)
        coef_ref[rs, :] = _gelu(act[rs, :]).astype(BF16) * w


def _peer_kernel(h2t_ref, u_ref, vtp_ref, vtc_ref, rank1_ref, b1_ref, cnt0_ref, a0_ref, x1_ref, gt2_ref, gf_ref,
                 y_ref, acc, coef_a, coef_b):
    k = pl.program_id(1)
    last = pl.num_programs(1) - 1
    route = (rank1_ref, b1_ref, cnt0_ref, a0_ref)

    def two_tiles():
        _peer_coef(2 * k, u_ref, 0, h2t_ref, *route, coef_a)
        _peer_coef(2 * k + 1, u_ref, PEER_E, h2t_ref, *route, coef_b)
        return jnp.dot(vtc_ref[...], coef_a[...], preferred_element_type=F32)

    @pl.when(k == 0)
    def _():
        acc[...] = two_tiles()

    @pl.when(jnp.logical_and(k > 0, k < last))
    def _():
        acc[...] += jnp.dot(vtp_ref[...], coef_b[...], preferred_element_type=F32)
        acc[...] += two_tiles()

    @pl.when(k == last)
    def _():
        tot = acc[...] + jnp.dot(vtp_ref[...], coef_b[...], preferred_element_type=F32)
        x2 = x1_ref[...] + gt2_ref[...] * jnp.transpose(tot)
        y_ref[...] = _rms(x2) * gf_ref[...]


def _peer(h2t, u_bf, vt_bf, rank1, b1, cnt0, a0, x1, gt2, gfin, per_token, rows_per_batch):
    n = x1.shape[0]
    t = PEER_T
    tpb = max(rows_per_batch // t, 1)
    rows = PEER_HEADS * N_KEYS
    steps = N_EXPERTS // (2 * PEER_E)
    once = pl.Buffered(1)
    tok = lambda r: pl.BlockSpec((r, t), lambda m, k: (0, m), pipeline_mode=once)
    if per_token:
        gt_spec = pl.BlockSpec((t, D_MODEL), lambda m, k: (m, 0), pipeline_mode=once)
    else:
        gt_spec = pl.BlockSpec((None, 1, D_MODEL), lambda m, k: (m // tpb, 0, 0))
    return pl.pallas_call(
        _peer_kernel,
        grid=(n // t, steps + 1),
        in_specs=[
            pl.BlockSpec((D_MODEL, t), lambda m, k: (0, m)),
            pl.BlockSpec((2 * PEER_E, D_MODEL), lambda m, k: (jnp.minimum(k, steps - 1), 0)),
            pl.BlockSpec((None, D_MODEL, PEER_E), lambda m, k: (jnp.maximum(2 * k - 1, 0), 0, 0)),
            pl.BlockSpec((None, D_MODEL, PEER_E), lambda m, k: (2 * jnp.minimum(k, steps - 1), 0, 0)),
            tok(rows), tok(rows), tok(rows), tok(rows),
            pl.BlockSpec((t, D_MODEL), lambda m, k: (m, 0), pipeline_mode=once),
            gt_spec,
            pl.BlockSpec((1, D_MODEL), lambda m, k: (0, 0)),
        ],
        out_specs=pl.BlockSpec((t, D_MODEL), lambda m, k: (m, 0)),
        out_shape=jax.ShapeDtypeStruct((n, D_MODEL), F32),
        scratch_shapes=[pltpu.VMEM((D_MODEL, t), F32), pltpu.VMEM((PEER_E, t), BF16), pltpu.VMEM((PEER_E, t), BF16)],
        compiler_params=_cparams(("parallel", "arbitrary"), VMEM_LIMIT_BIG),
        name="peer",
    )(h2t, u_bf, vt_bf, vt_bf, rank1, b1, cnt0, a0, x1, gt2, gfin)


def _rel_bias_tile(rel_bias, rows, cols, offset):
    rb = rel_bias.astype(F32)
    heads = rb.shape[0]
    rel_max = offset + rows - 1
    rel_min = offset - (cols - 1)
    lo, hi = max(rel_min, -REL_CLIP), min(rel_max, REL_CLIP)
    parts = []
    if rel_min < -REL_CLIP:
        parts.append(jnp.broadcast_to(rb[:, :1], (heads, -REL_CLIP - rel_min)))
    parts.append(rb[:, lo + REL_CLIP:hi + REL_CLIP + 1])
    if rel_max > REL_CLIP:
        parts.append(jnp.broadcast_to(rb[:, -1:], (heads, rel_max - REL_CLIP)))
    g = jnp.flip(jnp.concatenate(parts, axis=1), axis=1)
    period = rows + cols
    gp = jnp.pad(g, ((0, 0), (0, 1)))
    shifted = jnp.tile(gp, (1, rows))[:, :rows * (period - 1)].reshape(heads, rows, period - 1)
    return shifted[:, :, rows - 1:rows - 1 + cols]


def _prompt_bias(rel_bias):
    a = jnp.arange(Q_SUB, dtype=jnp.int32)[:, None]
    c = jnp.arange(K_WIN, dtype=jnp.int32)[None, :]
    cq = a // CHUNK
    ck = c // CHUNK
    ok = (ck >= cq) & (ck <= cq + N_LEFT_CHUNKS)
    return jnp.where(ok[None], _rel_bias_tile(rel_bias, Q_SUB, K_WIN, BAND_LEFT), NEG_INF)


def _sample_bias(rel_bias, seq, lc):
    return _rel_bias_tile(rel_bias, seq, lc, lc), _rel_bias_tile(rel_bias, seq, seq, 0)


def _layer(x, mod, cache_k, cache_v, state, is_sample, wts):
    (g1, w_main, w_alr, rel_bias, wup, balpha, gnorm, wa, wb, wo, g2, wqk_t, u_bf, vt_bf, gfin) = wts
    batch, seq, _ = x.shape
    n = batch * seq
    xf = x.reshape(n, D_MODEL)
    parts = [mod[:, i * D_MODEL:(i + 1) * D_MODEL] for i in range(6)]
    if is_sample:
        sh1, sc1, gt1, sh2, sc2, gt2 = [jnp.repeat(p, seq, axis=0) for p in parts]
        tm_in, tm_mg = n, 256
    else:
        sh1, sc1, gt1, sh2, sc2, gt2 = [p.reshape(batch, 1, D_MODEL) for p in parts]
        tm_in, tm_mg = 1024, 256

    z, alr = _inproj(xf, sh1, sc1, g1, w_main, w_alr, tm_in, is_sample, seq)

    if is_sample:
        lc = cache_k.shape[1]
        bias_c, bias_n = _sample_bias(rel_bias, seq, lc)
        oa = _attn_sample(z, cache_k.reshape(batch, lc * H_A, DH_A), cache_v.reshape(batch, lc * H_A, DH_A),
                          bias_c, bias_n, batch, seq)
        ob, s_new = _gla_sample(z, alr, state, wup, balpha, gnorm, batch, seq)
        k_rows = z[:, Z_KA:Z_KA + W_A].reshape(batch, seq, H_A, DH_A)
        v_rows = z[:, Z_VA:Z_VA + W_A].reshape(batch, seq, H_A, DH_A)
    else:
        oa = _attn_prompt(z, _prompt_bias(rel_bias), batch, seq)
        ob, s_new = _gla_prompt(z, alr, wup, balpha, gnorm, batch, seq)
        keep = min(BAND_LEFT, seq)
        zk = z.reshape(batch, seq, Z_WIDTH)[:, seq - keep:]
        k_rows = zk[:, :, Z_KA:Z_KA + W_A].reshape(batch, keep, H_A, DH_A)
        v_rows = zk[:, :, Z_VA:Z_VA + W_A].reshape(batch, keep, H_A, DH_A)

    x1, h2t = _merge(oa, ob, z, xf, gt1, sc2, sh2, g2, wa, wb, wo, tm_mg, is_sample, seq)
    rank1, b1, cnt0, a0 = _route(h2t, wqk_t)
    y = _peer(h2t, u_bf, vt_bf, rank1, b1, cnt0, a0, x1, gt2, gfin, is_sample, seq)
    return y.reshape(batch, seq, D_MODEL), k_rows, v_rows, s_new


def kernel(x_prompt, x_sample, cache_a_k, cache_a_v, state_gla, c_prompt, c_sample, w_ada, b_ada, g_norm1, w_in, rel_bias, w_alpha_up, b_alpha, g_gla_norm, w_branch_a, w_branch_b, w_out, g_norm2, w_peer_q, peer_subkeys, peer_u, peer_v, g_final):
    depth = w_in.shape[0]
    assert depth == 1, "the final rmsnorm is fused into the PEER kernel of the only layer"
    nbp = c_prompt.shape[0]
    l = 0
    nbs = c_sample.shape[0]
    pad = -(nbp + nbs) % 16
    c_all = jnp.concatenate([c_prompt, c_sample, jnp.zeros((pad, D_MODEL), F32)], axis=0)
    def only(w):
        return w.reshape(w.shape[1:])

    mod = _ada(c_all, only(w_ada), b_ada[l])

    alr_lo = Z_GL
    w_in0 = only(w_in)
    w_main = jnp.concatenate([w_in0[:, :alr_lo].astype(BF16), w_in0[:, alr_lo + ALPHA_RANK:].astype(BF16)], axis=1)
    w_alr = jnp.pad(w_in0[:, alr_lo:alr_lo + ALPHA_RANK], ((0, 0), (0, ALR_PAD - ALPHA_RANK))).astype(BF16)
    wup = jnp.pad(w_alpha_up[l], ((0, ALR_PAD - ALPHA_RANK), (0, 0))).astype(BF16)
    wqk = _qkfold(only(w_peer_q).astype(BF16), peer_subkeys[l].astype(BF16))
    wqk_t = wqk.reshape(D_MODEL, PEER_HEADS, 2, N_KEYS).transpose(2, 3, 1, 0).reshape(2 * PEER_HEADS * N_KEYS, D_MODEL)
    wts = (
        g_norm1[l].reshape(1, D_MODEL), w_main, w_alr, rel_bias[l], wup,
        b_alpha[l].reshape(1, KW_B), g_gla_norm[l].reshape(1, DV_B),
        only(w_branch_a).astype(BF16), only(w_branch_b).astype(BF16), only(w_out).astype(BF16),
        g_norm2[l].reshape(1, D_MODEL), wqk_t.astype(BF16),
        only(peer_u).astype(BF16),
        jnp.transpose(peer_v.astype(BF16).reshape(N_EXPERTS // PEER_E, PEER_E, D_MODEL), (0, 2, 1)),
        g_final.reshape(1, D_MODEL),
    )
    yp, kp, vp, sp = _layer(x_prompt, mod[:nbp], None, None, None, False, wts)
    ys, ks, vs, ss = _layer(x_sample, mod[nbp:nbp + nbs], only(cache_a_k), only(cache_a_v), only(state_gla), True, wts)
    return (yp, ys, kp[None], vp[None], sp[None], ks[None], vs[None], ss[None])
```

```python
import functools
import math

import jax
import jax.numpy as jnp
from jax import lax
from jax.experimental import pallas as pl
from jax.experimental.pallas import tpu as pltpu

F32 = jnp.float32
BF16 = jnp.bfloat16

D_MODEL = 2048
CHUNK = 64
N_LEFT_CHUNKS = 8
BAND_LEFT = N_LEFT_CHUNKS * CHUNK
H_A = 8
DH_A = 128
W_A = H_A * DH_A
REL_CLIP = 128
H_B = 4
DK_B = 128
DV_B = 256
KW_B = H_B * DK_B
VW_B = H_B * DV_B
ALPHA_RANK = 16
GATE_TAU = 16.0
PEER_HEADS = 8
N_KEYS = 128
N_EXPERTS = N_KEYS * N_KEYS
PEER_TOPK = 16
EPS = 1e-6
NEG_INF = -1e30

Z_QA, Z_KA, Z_VA = 0, W_A, 2 * W_A
Z_QB = 3 * W_A
Z_KB = Z_QB + KW_B
Z_VB = Z_KB + KW_B
Z_RB = Z_VB + VW_B
Z_GL = Z_RB + VW_B
Z_WIDTH = Z_GL + 2 * D_MODEL
ALR_PAD = 128

VMEM_LIMIT_BIG = 56 * 1024 * 1024
VMEM_LIMIT_MID = 40 * 1024 * 1024

Q_TILE = 512
Q_SUB = 128
K_WIN = Q_SUB + BAND_LEFT


def _cparams(sem, vmem=None):
    return pltpu.CompilerParams(dimension_semantics=sem, vmem_limit_bytes=vmem)


def _rms(xf):
    return xf * lax.rsqrt(jnp.mean(xf * xf, axis=-1, keepdims=True) + EPS)


def _sigmoid(x):
    return 1.0 / (1.0 + jnp.exp(-x))


def _ada_kernel(c_ref, w_ref, b_ref, o_ref):
    c = c_ref[...]
    a = (c * _sigmoid(c)).astype(BF16)
    o_ref[...] = jnp.dot(a, w_ref[...].astype(BF16), preferred_element_type=F32) + b_ref[...]


def _ada(c_all, w_ada, b_ada):
    nb = c_all.shape[0]
    n_out = w_ada.shape[1]
    tn = 1024
    return pl.pallas_call(
        _ada_kernel,
        grid=(n_out // tn,),
        in_specs=[
            pl.BlockSpec((nb, D_MODEL), lambda j: (0, 0)),
            pl.BlockSpec((D_MODEL, tn), lambda j: (0, j)),
            pl.BlockSpec((1, tn), lambda j: (0, j)),
        ],
        out_specs=pl.BlockSpec((nb, tn), lambda j: (0, j)),
        out_shape=jax.ShapeDtypeStruct((nb, n_out), F32),
        compiler_params=_cparams(("arbitrary",), VMEM_LIMIT_MID),
        name="ada",
    )(c_all, w_ada, b_ada.reshape(1, n_out))


def _mod_spec(per_token, tm, tiles_per_batch, grid_rank):
    if per_token:
        if grid_rank == 1:
            return pl.BlockSpec((tm, D_MODEL), lambda m: (m, 0))
        return pl.BlockSpec((tm, D_MODEL), lambda m, n: (m, 0))
    if grid_rank == 1:
        return pl.BlockSpec((None, 1, D_MODEL), lambda m: (m // tiles_per_batch, 0, 0))
    return pl.BlockSpec((None, 1, D_MODEL), lambda m, n: (m // tiles_per_batch, 0, 0))


def _inproj_kernel(x_ref, sh_ref, sc_ref, g_ref, w_ref, walr_ref, z_ref, alr_ref, h_scr):
    @pl.when(pl.program_id(1) == 0)
    def _():
        h = _rms(x_ref[...]) * g_ref[...]
        h = h * (1.0 + sc_ref[...]) + sh_ref[...]
        hb = h.astype(BF16)
        h_scr[...] = hb
        alr_ref[...] = jnp.dot(hb, walr_ref[...], preferred_element_type=F32)

    z_ref[...] = jnp.dot(h_scr[...], w_ref[...], preferred_element_type=F32)


def _inproj(x, sh, sc, g, w_main, w_alr, tm, per_token, rows_per_batch):
    n = x.shape[0]
    tn = 1024
    tpb = max(rows_per_batch // tm, 1)
    return pl.pallas_call(
        _inproj_kernel,
        grid=(n // tm, Z_WIDTH // tn),
        in_specs=[
            pl.BlockSpec((tm, D_MODEL), lambda m, j: (m, 0)),
            _mod_spec(per_token, tm, tpb, 2),
            _mod_spec(per_token, tm, tpb, 2),
            pl.BlockSpec((1, D_MODEL), lambda m, j: (0, 0)),
            pl.BlockSpec((D_MODEL, tn), lambda m, j: (0, j)),
            pl.BlockSpec((D_MODEL, ALR_PAD), lambda m, j: (0, 0)),
        ],
        out_specs=[
            pl.BlockSpec((tm, tn), lambda m, j: (m, j)),
            pl.BlockSpec((tm, ALR_PAD), lambda m, j: (m, 0)),
        ],
        out_shape=[
            jax.ShapeDtypeStruct((n, Z_WIDTH), F32),
            jax.ShapeDtypeStruct((n, ALR_PAD), F32),
        ],
        scratch_shapes=[pltpu.VMEM((tm, D_MODEL), BF16)],
        compiler_params=_cparams(("parallel", "arbitrary"), VMEM_LIMIT_BIG),
        name="inproj",
    )(x, sh, sc, g, w_main, w_alr)


def _attn_prompt_kernel(q_ref, kp_ref, kc_ref, vp_ref, vc_ref, bias_ref, o_ref, kw, vw):
    first = pl.program_id(2) == 0
    kw[0:Q_TILE, :] = kp_ref[...].astype(BF16)
    kw[Q_TILE:2 * Q_TILE, :] = kc_ref[...].astype(BF16)
    vw[0:Q_TILE, :] = vp_ref[...].astype(BF16)
    vw[Q_TILE:2 * Q_TILE, :] = vc_ref[...].astype(BF16)
    scale = DH_A ** -0.5
    probs_ = [(hh, r) for hh in range(ATT_HEADS) for r in range(Q_TILE // Q_SUB)]
    hcol = [slice(hh * DH_A, (hh + 1) * DH_A) for hh in range(ATT_HEADS)]
    scores = [lax.dot_general((q_ref[r * Q_SUB:(r + 1) * Q_SUB, hcol[hh]] * scale).astype(BF16),
                              kw[r * Q_SUB:r * Q_SUB + K_WIN, hcol[hh]],
                              (((1,), (1,)), ((), ())), preferred_element_type=F32) for (hh, r) in probs_]
    probs, sums = [], []
    col = lax.broadcasted_iota(jnp.int32, (1, K_WIN), 1)
    for idx, (hh, r) in enumerate(probs_):
        before = jnp.where(jnp.logical_and(first, col + r * Q_SUB < Q_TILE), NEG_INF, 0.0)
        s = scores[idx] + bias_ref[hh] + before
        m = jnp.max(s, axis=-1, keepdims=True)
        p = jnp.exp(s - m)
        sums.append(jnp.sum(p, axis=-1, keepdims=True))
        probs.append(p.astype(BF16))
    outs = [jnp.dot(probs[idx], vw[r * Q_SUB:r * Q_SUB + K_WIN, hcol[hh]], preferred_element_type=F32)
            for idx, (hh, r) in enumerate(probs_)]
    for idx, (hh, r) in enumerate(probs_):
        o_ref[r * Q_SUB:(r + 1) * Q_SUB, hcol[hh]] = (outs[idx] / sums[idx]).astype(o_ref.dtype)


ATT_HEADS = 4


def _attn_prompt(z, bias, batch, seq):
    n = batch * seq
    tiles = seq // Q_TILE
    width = ATT_HEADS * DH_A

    def cur(col):
        return pl.BlockSpec((Q_TILE, width), lambda h, b, i: (b * tiles + i, col + h))

    def prev(col):
        return pl.BlockSpec((Q_TILE, width),
                            lambda h, b, i: (b * tiles + jnp.maximum(i - 1, 0), col + h))

    kcol, vcol = Z_KA // width, Z_VA // width
    return pl.pallas_call(
        _attn_prompt_kernel,
        grid=(H_A // ATT_HEADS, batch, tiles),
        in_specs=[cur(0), prev(kcol), cur(kcol), prev(vcol), cur(vcol),
                  pl.BlockSpec((ATT_HEADS, Q_SUB, K_WIN), lambda h, b, i: (h, 0, 0))],
        out_specs=pl.BlockSpec((Q_TILE, width), lambda h, b, i: (b * tiles + i, h)),
        out_shape=jax.ShapeDtypeStruct((n, W_A), BF16),
        scratch_shapes=[pltpu.VMEM((2 * Q_TILE, width), BF16), pltpu.VMEM((2 * Q_TILE, width), BF16)],
        compiler_params=_cparams(("arbitrary", "arbitrary", "arbitrary")),
        name="attn_prompt",
    )(z, z, z, z, z, bias)


def _attn_sample_kernel(q_ref, kn_ref, vn_ref, ck_ref, cv_ref, bc_ref, bn_ref, o_ref):
    scale = DH_A ** -0.5
    nt = (((1,), (1,)), ((), ()))
    lc = ck_ref.shape[0] // H_A
    heads = range(H_A)
    cols = [slice(h * DH_A, (h + 1) * DH_A) for h in heads]
    crow = [pl.ds(h, lc, stride=H_A) for h in heads]
    qs = [q_ref[:, cols[h]].astype(BF16) for h in heads]
    s_c = [lax.dot_general(qs[h], ck_ref[crow[h], :].astype(BF16), nt, preferred_element_type=F32) for h in heads]
    s_n = [lax.dot_general(qs[h], kn_ref[:, cols[h]].astype(BF16), nt, preferred_element_type=F32) for h in heads]
    p_c, p_n, sums = [], [], []
    for h in heads:
        sc = s_c[h] * scale + bc_ref[h]
        sn = s_n[h] * scale + bn_ref[h]
        m = jnp.maximum(jnp.max(sc, axis=-1, keepdims=True), jnp.max(sn, axis=-1, keepdims=True))
        pc = jnp.exp(sc - m)
        pn = jnp.exp(sn - m)
        sums.append(jnp.sum(pc, axis=-1, keepdims=True) + jnp.sum(pn, axis=-1, keepdims=True))
        p_c.append(pc.astype(BF16))
        p_n.append(pn.astype(BF16))
    outs = [jnp.dot(p_c[h], cv_ref[crow[h], :].astype(BF16), preferred_element_type=F32)
            + jnp.dot(p_n[h], vn_ref[:, cols[h]].astype(BF16), preferred_element_type=F32) for h in heads]
    for h in heads:
        o_ref[:, cols[h]] = (outs[h] / sums[h]).astype(o_ref.dtype)


def _attn_sample(z, cache_k, cache_v, bias_c, bias_n, batch, seq):
    lc = cache_k.shape[1] // H_A
    return pl.pallas_call(
        _attn_sample_kernel,
        grid=(batch,),
        in_specs=[
            pl.BlockSpec((seq, W_A), lambda b: (b, Z_QA // W_A)),
            pl.BlockSpec((seq, W_A), lambda b: (b, Z_KA // W_A)),
            pl.BlockSpec((seq, W_A), lambda b: (b, Z_VA // W_A)),
            pl.BlockSpec((None, lc * H_A, DH_A), lambda b: (b, 0, 0)),
            pl.BlockSpec((None, lc * H_A, DH_A), lambda b: (b, 0, 0)),
            pl.BlockSpec((H_A, seq, lc), lambda b: (0, 0, 0)),
            pl.BlockSpec((H_A, seq, seq), lambda b: (0, 0, 0)),
        ],
        out_specs=pl.BlockSpec((seq, W_A), lambda b: (b, 0)),
        out_shape=jax.ShapeDtypeStruct((batch * seq, W_A), BF16),
        compiler_params=_cparams(("arbitrary",), VMEM_LIMIT_MID),
        name="attn_sample",
    )(z, z, z, cache_k, cache_v, bias_c, bias_n)


def _split_bf16(x):
    hi = x.astype(BF16)
    lo = (x - hi.astype(F32)).astype(BF16)
    return hi, lo


_TN = (((0,), (0,)), ((), ()))
_NT = (((1,), (1,)), ((), ()))


def _gla_state_free(units, tri, causal, c):
    scale = DK_B ** -0.5
    ones = jnp.ones((c, DK_B), BF16)
    split = [_split_bf16(g) for (_, _, _, g) in units]
    cum = [jnp.dot(tri, hi, preferred_element_type=F32) + jnp.dot(tri, lo, preferred_element_type=F32)
           for (hi, lo) in split]
    tot = [lax.dot_general(hi, ones, _TN, preferred_element_type=F32)
           + lax.dot_general(lo, ones, _TN, preferred_element_type=F32) for (hi, lo) in split]
    qd, kd, vb, att = [], [], [], []
    for (q, k, v, _), b in zip(units, cum):
        mid = b[c // 2:c // 2 + 1, :]
        blast = b[c - 1:c, :]
        qs = q * scale
        ks = k * scale
        att.append(lax.dot_general((qs * jnp.exp(b - mid)).astype(BF16), (ks * jnp.exp(mid - b)).astype(BF16),
                                   _NT, preferred_element_type=F32))
        qd.append((qs * jnp.exp(b)).astype(BF16))
        kd.append((ks * jnp.exp(blast - b)).astype(BF16))
        vb.append(v.astype(BF16))
    intra = [jnp.dot(jnp.where(causal, a, 0.0).astype(BF16), v, preferred_element_type=F32)
             for a, v in zip(att, vb)]
    out = []
    for i in range(len(units)):
        dec = jnp.exp(tot[i])
        out.append((intra[i], qd[i], kd[i], vb[i], jnp.concatenate([dec, dec], axis=1)))
    return out


def _gla_read_state(pre, s_prev):
    o_intra, qd, _, _, _ = pre
    return o_intra + jnp.dot(qd, s_prev.astype(BF16), preferred_element_type=F32)


def _gla_next_state(pre, s_prev):
    _, _, kd, vb, dec = pre
    return dec * s_prev + lax.dot_general(kd, vb, _TN, preferred_element_type=F32)


def _log_decay(alr, wup, balpha):
    x = jnp.dot(alr.astype(BF16), wup, preferred_element_type=F32) + balpha
    return (jnp.minimum(x, 0.0) - jnp.log1p(jnp.exp(-jnp.abs(x)))) * (1.0 / GATE_TAU)


def _gla_epilogue(o, rb, gnorm):
    y = _rms(o) * gnorm
    return y * (rb * _sigmoid(rb))


def _tri_and_causal(c):
    row = lax.broadcasted_iota(jnp.int32, (c, c), 0)
    col = lax.broadcasted_iota(jnp.int32, (c, c), 1)
    causal = col <= row
    return jnp.where(causal, 1.0, 0.0).astype(BF16), causal


def _gla_prompt_kernel(q_ref, k_ref, v_ref, rb_ref, alr_ref, wup_ref, balpha_ref, gnorm_ref,
                       ob_ref, st_ref, s_scr, *, batch, blocks):
    step = pl.program_id(0)

    @pl.when(step == 0)
    def _():
        s_scr[...] = jnp.zeros_like(s_scr)

    tri, causal = _tri_and_causal(CHUNK)
    wup = wup_ref[...]
    balpha = balpha_ref[...]
    gnorm = gnorm_ref[...]
    ksl = [slice(h * DK_B, (h + 1) * DK_B) for h in range(H_B)]
    vsl = [slice(h * DV_B, (h + 1) * DV_B) for h in range(H_B)]
    streams = [(bi, h) for bi in range(batch) for h in range(H_B)]
    rows = [slice(blk * CHUNK, (blk + 1) * CHUNK) for blk in range(blocks)]
    logg = {(blk, bi): _log_decay(alr_ref[bi, rows[blk], :], wup, balpha)
            for blk in range(blocks) for bi in range(batch)}
    units = [(q_ref[bi, rows[blk], ksl[h]], k_ref[bi, rows[blk], ksl[h]], v_ref[bi, rows[blk], vsl[h]],
              logg[(blk, bi)][:, ksl[h]]) for blk in range(blocks) for (bi, h) in streams]
    pre = _gla_state_free(units, tri, causal, CHUNK)
    for blk in range(blocks):
        mine = pre[blk * len(streams):(blk + 1) * len(streams)]
        outs = [_gla_read_state(p, s_scr[si]) for si, p in enumerate(mine)]
        nxt = [_gla_next_state(p, s_scr[si]) for si, p in enumerate(mine)]
        for si, (bi, h) in enumerate(streams):
            s_scr[si] = nxt[si]
            ob_ref[bi, rows[blk], vsl[h]] = _gla_epilogue(outs[si], rb_ref[bi, rows[blk], vsl[h]],
                                                          gnorm).astype(ob_ref.dtype)

    @pl.when(step == pl.num_programs(0) - 1)
    def _():
        st_ref[...] = s_scr[...]


def _gla_prompt(z, alr, wup, balpha, gnorm, batch, seq):
    blocks = 4
    rows = blocks * CHUNK
    z3 = z.reshape(batch, seq, Z_WIDTH)
    alr3 = alr.reshape(batch, seq, ALR_PAD)
    ob, st = pl.pallas_call(
        functools.partial(_gla_prompt_kernel, batch=batch, blocks=blocks),
        grid=(seq // rows,),
        in_specs=[
            pl.BlockSpec((batch, rows, KW_B), lambda j: (0, j, Z_QB // KW_B)),
            pl.BlockSpec((batch, rows, KW_B), lambda j: (0, j, Z_KB // KW_B)),
            pl.BlockSpec((batch, rows, VW_B), lambda j: (0, j, Z_VB // VW_B)),
            pl.BlockSpec((batch, rows, VW_B), lambda j: (0, j, Z_RB // VW_B)),
            pl.BlockSpec((batch, rows, ALR_PAD), lambda j: (0, j, 0)),
            pl.BlockSpec((ALR_PAD, KW_B), lambda j: (0, 0)),
            pl.BlockSpec((1, KW_B), lambda j: (0, 0)),
            pl.BlockSpec((1, DV_B), lambda j: (0, 0)),
        ],
        out_specs=[
            pl.BlockSpec((batch, rows, VW_B), lambda j: (0, j, 0)),
            pl.BlockSpec((batch * H_B, DK_B, DV_B), lambda j: (0, 0, 0)),
        ],
        out_shape=[
            jax.ShapeDtypeStruct((batch, seq, VW_B), BF16),
            jax.ShapeDtypeStruct((batch * H_B, DK_B, DV_B), F32),
        ],
        scratch_shapes=[pltpu.VMEM((batch * H_B, DK_B, DV_B), F32)],
        compiler_params=_cparams(("arbitrary",), VMEM_LIMIT_MID),
        name="gla_prompt",
    )(z3, z3, z3, z3, alr3, wup, balpha, gnorm)
    return ob.reshape(batch * seq, VW_B), st.reshape(batch, H_B, DK_B, DV_B)


def _gla_sample_kernel(q_ref, k_ref, v_ref, rb_ref, alr_ref, s0_ref, wup_ref, balpha_ref, gnorm_ref,
                       ob_ref, st_ref, *, seq):
    tri, causal = _tri_and_causal(seq)
    logg = _log_decay(alr_ref[...], wup_ref[...], balpha_ref[...])
    gnorm = gnorm_ref[...]
    ksl = [slice(h * DK_B, (h + 1) * DK_B) for h in range(H_B)]
    vsl = [slice(h * DV_B, (h + 1) * DV_B) for h in range(H_B)]
    pre = _gla_state_free([(q_ref[:, ksl[h]], k_ref[:, ksl[h]], v_ref[:, vsl[h]], logg[:, ksl[h]])
                           for h in range(H_B)], tri, causal, seq)
    outs = [_gla_read_state(pre[h], s0_ref[h]) for h in range(H_B)]
    nxt = [_gla_next_state(pre[h], s0_ref[h]) for h in range(H_B)]
    for h in range(H_B):
        st_ref[h] = nxt[h]
        ob_ref[:, vsl[h]] = _gla_epilogue(outs[h], rb_ref[:, vsl[h]], gnorm).astype(ob_ref.dtype)


def _gla_sample(z, alr, state, wup, balpha, gnorm, batch, seq):
    return pl.pallas_call(
        functools.partial(_gla_sample_kernel, seq=seq),
        grid=(batch,),
        in_specs=[
            pl.BlockSpec((seq, KW_B), lambda b: (b, Z_QB // KW_B)),
            pl.BlockSpec((seq, KW_B), lambda b: (b, Z_KB // KW_B)),
            pl.BlockSpec((seq, VW_B), lambda b: (b, Z_VB // VW_B)),
            pl.BlockSpec((seq, VW_B), lambda b: (b, Z_RB // VW_B)),
            pl.BlockSpec((seq, ALR_PAD), lambda b: (b, 0)),
            pl.BlockSpec((None, H_B, DK_B, DV_B), lambda b: (b, 0, 0, 0)),
            pl.BlockSpec((ALR_PAD, KW_B), lambda b: (0, 0)),
            pl.BlockSpec((1, KW_B), lambda b: (0, 0)),
            pl.BlockSpec((1, DV_B), lambda b: (0, 0)),
        ],
        out_specs=[
            pl.BlockSpec((seq, VW_B), lambda b: (b, 0)),
            pl.BlockSpec((None, H_B, DK_B, DV_B), lambda b: (b, 0, 0, 0)),
        ],
        out_shape=[
            jax.ShapeDtypeStruct((batch * seq, VW_B), BF16),
            jax.ShapeDtypeStruct((batch, H_B, DK_B, DV_B), F32),
        ],
        compiler_params=_cparams(("arbitrary",)),
        name="gla_sample",
    )(z, z, z, z, alr, state, wup, balpha, gnorm)


def _merge_kernel(oa_ref, ob_ref, gla_ref, glb_ref, x_ref, gt1_ref, sc2_ref, sh2_ref, g2_ref,
                  wa_ref, wb_ref, wo_ref, x1_ref, h2t_ref):
    ya = jnp.dot(oa_ref[...], wa_ref[...], preferred_element_type=F32)
    yb = jnp.dot(ob_ref[...], wb_ref[...], preferred_element_type=F32)
    merged = _sigmoid(gla_ref[...]) * ya + _sigmoid(glb_ref[...]) * yb
    x1 = x_ref[...] + gt1_ref[...] * jnp.dot(merged.astype(BF16), wo_ref[...], preferred_element_type=F32)
    x1_ref[...] = x1
    h2 = _rms(x1) * g2_ref[...]
    h2 = h2 * (1.0 + sc2_ref[...]) + sh2_ref[...]
    h2t_ref[...] = jnp.transpose(h2).astype(BF16)


def _merge(oa, ob, z, x, gt1, sc2, sh2, g2, wa, wb, wo, tm, per_token, rows_per_batch):
    n = x.shape[0]
    tpb = max(rows_per_batch // tm, 1)
    const = lambda shape: pl.BlockSpec(shape, lambda m: (0, 0))
    mod = _mod_spec(per_token, tm, tpb, 1)
    return pl.pallas_call(
        _merge_kernel,
        grid=(n // tm,),
        in_specs=[
            pl.BlockSpec((tm, W_A), lambda m: (m, 0)),
            pl.BlockSpec((tm, VW_B), lambda m: (m, 0)),
            pl.BlockSpec((tm, D_MODEL), lambda m: (m, Z_GL // D_MODEL)),
            pl.BlockSpec((tm, D_MODEL), lambda m: (m, Z_GL // D_MODEL + 1)),
            pl.BlockSpec((tm, D_MODEL), lambda m: (m, 0)),
            mod, mod, mod,
            const((1, D_MODEL)),
            const((W_A, D_MODEL)), const((VW_B, D_MODEL)), const((D_MODEL, D_MODEL)),
        ],
        out_specs=[
            pl.BlockSpec((tm, D_MODEL), lambda m: (m, 0)),
            pl.BlockSpec((D_MODEL, tm), lambda m: (0, m)),
        ],
        out_shape=[
            jax.ShapeDtypeStruct((n, D_MODEL), F32),
            jax.ShapeDtypeStruct((D_MODEL, n), BF16),
        ],
        compiler_params=_cparams(("arbitrary",), VMEM_LIMIT_BIG),
        name="merge",
    )(oa, ob, z, z, x, gt1, sc2, sh2, g2, wa, wb, wo)


def _qkfold_kernel(wq_ref, sk_ref, o_ref):
    o_ref[...] = lax.dot_general(wq_ref[...], sk_ref[...], (((1,), (1,)), ((), ())),
                                 preferred_element_type=F32)


def _qkfold(w_q, subkeys):
    half = N_KEYS
    return pl.pallas_call(
        _qkfold_kernel,
        grid=(2 * PEER_HEADS,),
        in_specs=[
            pl.BlockSpec((D_MODEL, half), lambda c: (0, c)),
            pl.BlockSpec((None, N_KEYS, half), lambda c: (c % 2, 0, 0)),
        ],
        out_specs=pl.BlockSpec((D_MODEL, N_KEYS), lambda c: (0, c)),
        out_shape=jax.ShapeDtypeStruct((D_MODEL, 2 * PEER_HEADS * N_KEYS), F32),
        compiler_params=_cparams(("arbitrary",)),
        name="qkfold",
    )(w_q, subkeys)


ROUTE_T = 256
LANES = 128
_SET_ROWS = N_KEYS * PEER_HEADS
_PAIRS = [(r, q) for r in range(PEER_TOPK) for q in range(PEER_TOPK) if (r + 1) * (q + 1) <= PEER_TOPK]


def _sort_desc(x):
    x = list(x)
    n = len(x)
    k = 2
    while k <= n:
        j = k // 2
        while j >= 1:
            for i in range(n):
                l = i ^ j
                if l > i:
                    hi, lo = jnp.maximum(x[i], x[l]), jnp.minimum(x[i], x[l])
                    x[i], x[l] = (hi, lo) if (i & k) == 0 else (lo, hi)
            j //= 2
        k *= 2
    return x


def _merge_top(a, b):
    n = len(a)
    x = [jnp.maximum(a[i], b[n - 1 - i]) for i in range(n)]
    j = n // 2
    while j >= 1:
        for i in range(n):
            l = i ^ j
            if l > i:
                x[i], x[l] = jnp.maximum(x[i], x[l]), jnp.minimum(x[i], x[l])
        j //= 2
    return x


def _top_sorted(vals, top):
    groups = [_sort_desc(vals[g:g + top]) for g in range(0, len(vals), top)]
    while len(groups) > 1:
        groups = [_merge_top(groups[g], groups[g + 1]) for g in range(0, len(groups), 2)]
    return groups[0]


def _route_fast(s0, s1):
    shp = s0[0].shape
    one = jnp.ones(shp, F32)
    zero = jnp.zeros(shp, F32)
    top0 = _top_sorted(s0, PEER_TOPK)
    top1 = _top_sorted(s1, PEER_TOPK)
    cand = [top0[r] + top1[q] for (r, q) in _PAIRS]
    pad = [jnp.full(shp, -jnp.inf, F32)] * (-len(cand) % PEER_TOPK)
    topc = _top_sorted(cand + pad, PEER_TOPK)
    tau = topc[PEER_TOPK - 1]
    cnt = [zero] * PEER_TOPK
    for c, (r, q) in enumerate(_PAIRS):
        cnt[r] = cnt[r] + jnp.where(cand[c] >= tau, one, zero)
    picked = cnt[0]
    for r in range(1, PEER_TOPK):
        picked = picked + cnt[r]
    zsum = zero
    for r in range(PEER_TOPK):
        zsum = zsum + jnp.exp(topc[r] - topc[0])
    zinv = 1.0 / zsum
    tie = jnp.where(picked != float(PEER_TOPK), one, zero)
    for t in (top0, top1):
        for r in range(PEER_TOPK - 1):
            tie = jnp.where(t[r] == t[r + 1], one, tie)
    in0, in1 = zero, zero
    rank1, b1, cnt0, a0 = [], [], [], []
    for k in range(N_KEYS):
        v0, v1 = s0[k], s1[k]
        c0 = zero
        rk = jnp.full(shp, float(PEER_TOPK), F32)
        for r in range(PEER_TOPK - 1, -1, -1):
            c0 = jnp.where(v0 >= top0[r], cnt[r], c0)
            rk = jnp.where(v1 >= top1[r], float(r), rk)
        in0 = in0 + jnp.where(v0 >= top0[PEER_TOPK - 1], one, zero)
        in1 = in1 + jnp.where(v1 >= top1[PEER_TOPK - 1], one, zero)
        cnt0.append(c0)
        rank1.append(rk)
        a0.append(jnp.exp(v0 - top0[0]) * zinv)
        b1.append(jnp.exp(v1 - top1[0]))
    tie = jnp.where(in0 != float(PEER_TOPK), one, tie)
    tie = jnp.where(in1 != float(PEER_TOPK), one, tie)
    return rank1, b1, cnt0, a0, tie


def _route_kernel(h2t_ref, wqk_ref, rank1_ref, b1_ref, cnt0_ref, a0_ref,
                  s_scr, sw_scr, rk_scr, val_scr, idx_scr, tmp_scr):
    halves = h2t_ref.shape[1] // LANES
    shp = (PEER_HEADS, LANES)
    s = jnp.dot(wqk_ref[...], h2t_ref[...], preferred_element_type=F32)
    for hf in range(halves):
        s_scr[hf] = s[:, hf * LANES:(hf + 1) * LANES]
    neg = jnp.full(shp, -jnp.inf, F32)

    def rows(p, k):
        return pl.ds(p * _SET_ROWS + k * PEER_HEADS, PEER_HEADS)

    def one_half(hf, carry0):
        f_rank1, f_b1, f_cnt0, f_a0, tie = _route_fast([s_scr[hf, rows(0, k), :] for k in range(N_KEYS)],
                                                       [s_scr[hf, rows(1, k), :] for k in range(N_KEYS)])
        for k in range(N_KEYS):
            kr = pl.ds(k * PEER_HEADS, PEER_HEADS)
            tmp_scr[hf, 0, kr, :] = f_cnt0[k]
            tmp_scr[hf, 1, kr, :] = f_a0[k]
            tmp_scr[hf, 2, kr, :] = f_b1[k]
            rk_scr[hf, rows(1, k), :] = f_rank1[k]

        @pl.when(jnp.max(tie) > 0.0)
        def _():
            exact_half(hf)

        return carry0

    def exact_half(hf):
        sw_scr[hf] = s_scr[hf]
        rk_scr[hf] = jnp.full(rk_scr.shape[1:], float(PEER_TOPK), F32)

        def extract(r, carry):
            rf = jnp.full(shp, r, jnp.int32).astype(F32)
            for p in range(2):
                best, bidx = None, None
                for part in range(4):
                    m = neg
                    ix = jnp.zeros(shp, jnp.int32)
                    for k in range(part * 32, part * 32 + 32):
                        v = sw_scr[hf, rows(p, k), :]
                        gt = v > m
                        m = jnp.where(gt, v, m)
                        ix = jnp.where(gt, k, ix)
                    if best is None:
                        best, bidx = m, ix
                    else:
                        gt = m > best
                        best = jnp.where(gt, m, best)
                        bidx = jnp.where(gt, ix, bidx)
                val_scr[p, r] = best
                idx_scr[p, r] = bidx
                for k in range(N_KEYS):
                    hit = bidx == k
                    sw_scr[hf, rows(p, k), :] = jnp.where(hit, -jnp.inf, sw_scr[hf, rows(p, k), :])
                    rk_scr[hf, rows(p, k), :] = jnp.where(hit, rf, rk_scr[hf, rows(p, k), :])
            return carry

        lax.fori_loop(0, PEER_TOPK, extract, 0)

        v0 = [val_scr[0, r] for r in range(PEER_TOPK)]
        v1 = [val_scr[1, q] for q in range(PEER_TOPK)]
        cand = [v0[r] + v1[q] for (r, q) in _PAIRS]
        top = v0[0] + v1[0]
        cnt = [jnp.zeros(shp, F32) for _ in range(PEER_TOPK)]
        zsum = jnp.zeros(shp, F32)
        for _ in range(PEER_TOPK):
            m = neg
            ix = jnp.zeros(shp, jnp.int32)
            for c, cv in enumerate(cand):
                gt = cv > m
                m = jnp.where(gt, cv, m)
                ix = jnp.where(gt, c, ix)
            zsum = zsum + jnp.exp(m - top)
            for c, (r, q) in enumerate(_PAIRS):
                hit = ix == c
                cand[c] = jnp.where(hit, -jnp.inf, cand[c])
                cnt[r] = cnt[r] + jnp.where(hit, 1.0, 0.0)
        zinv = 1.0 / zsum

        idx0 = [idx_scr[0, r] for r in range(PEER_TOPK)]
        for k in range(N_KEYS):
            c0 = jnp.zeros(shp, F32)
            for r in range(PEER_TOPK):
                c0 = jnp.where(idx0[r] == k, cnt[r], c0)
            kr = pl.ds(k * PEER_HEADS, PEER_HEADS)
            tmp_scr[hf, 0, kr, :] = c0
            tmp_scr[hf, 1, kr, :] = jnp.exp(s_scr[hf, rows(0, k), :] - v0[0]) * zinv
            tmp_scr[hf, 2, kr, :] = jnp.exp(s_scr[hf, rows(1, k), :] - v1[0])

    lax.fori_loop(0, halves, one_half, 0)

    for hf in range(halves):
        ls = slice(hf * LANES, (hf + 1) * LANES)
        for h in range(PEER_HEADS):
            dst = slice(h * N_KEYS, (h + 1) * N_KEYS)
            src = pl.ds(h, N_KEYS, stride=PEER_HEADS)
            cnt0_ref[dst, ls] = _pack_pair(tmp_scr[hf, 0, src, :])
            a0_ref[dst, ls] = _pack_pair(tmp_scr[hf, 1, src, :])
            b1_ref[dst, ls] = tmp_scr[hf, 2, src, :].astype(BF16)
            rank1_ref[dst, ls] = rk_scr[hf, pl.ds(_SET_ROWS + h, N_KEYS, stride=PEER_HEADS), :].astype(BF16)


def _pack_pair(x):
    bits = pltpu.bitcast(x.astype(BF16).astype(F32), jnp.uint32)
    return bits | (bits >> 16)


def _route(h2t, wqk_t):
    n = h2t.shape[1]
    t = ROUTE_T
    rows = PEER_HEADS * N_KEYS
    ospec = pl.BlockSpec((rows, t), lambda m: (0, m))
    out_bf = jax.ShapeDtypeStruct((rows, n), BF16)
    out_pk = jax.ShapeDtypeStruct((rows, n), jnp.uint32)
    return pl.pallas_call(
        _route_kernel,
        grid=(n // t,),
        in_specs=[
            pl.BlockSpec((D_MODEL, t), lambda m: (0, m)),
            pl.BlockSpec((2 * rows, D_MODEL), lambda m: (0, 0)),
        ],
        out_specs=[ospec, ospec, ospec, ospec],
        out_shape=[out_bf, out_bf, out_pk, out_pk],
        scratch_shapes=[
            pltpu.VMEM((t // LANES, 2 * rows, LANES), F32),
            pltpu.VMEM((t // LANES, 2 * rows, LANES), F32),
            pltpu.VMEM((t // LANES, 2 * rows, LANES), F32),
            pltpu.VMEM((2, PEER_TOPK, PEER_HEADS, LANES), F32),
            pltpu.VMEM((2, PEER_TOPK, PEER_HEADS, LANES), jnp.int32),
            pltpu.VMEM((t // LANES, 3, rows, LANES), F32),
        ],
        compiler_params=_cparams(("arbitrary",), VMEM_LIMIT_MID),
        name="route",
    )(h2t, wqk_t)


PEER_T = 512
PEER_E = 512


def _gelu(x):
    return 0.5 * x * (1.0 + lax.erf(x * (2.0 ** -0.5)))


def _bcast_pair_rows(ref, row, t):
    word = jnp.broadcast_to(ref[pl.ds(row, 1), :], (8, t))
    pair = pltpu.bitcast(word, BF16)
    return jnp.broadcast_to(pair[None], (N_KEYS // 16, 16, t)).reshape(N_KEYS, t)


def _peer_coef(tile, u_ref, row0, h2t_ref, rank1_ref, b1_ref, cnt0_ref, a0_ref, coef_ref):
    t = h2t_ref.shape[1]
    act = jnp.dot(u_ref[row0:row0 + PEER_E, :], h2t_ref[...], preferred_element_type=F32)
    per = PEER_E // N_KEYS
    for ii in range(per):
        i = tile * per + ii
        w = None
        for h in range(PEER_HEADS):
            hs = slice(h * N_KEYS, (h + 1) * N_KEYS)
            cnt = _bcast_pair_rows(cnt0_ref, h * N_KEYS + i, t)
            a = _bcast_pair_rows(a0_ref, h * N_KEYS + i, t)
            term = jnp.where(rank1_ref[hs, :] < cnt, b1_ref[hs, :] * a, jnp.zeros((), BF16))
            w = term if w is None else w + term
        rs = slice(ii * N_KEYS, (ii + 1) * N_KEYS)
        coef_ref[rs, :] = _gelu(act[rs, :]).astype(BF16) * w


def _peer_kernel(h2t_ref, u_ref, vtp_ref, vtc_ref, rank1_ref, b1_ref, cnt0_ref, a0_ref, x1_ref, gt2_ref, gf_ref,
                 y_ref, acc, coef_a, coef_b):
    k = pl.program_id(1)
    last = pl.num_programs(1) - 1
    route = (rank1_ref, b1_ref, cnt0_ref, a0_ref)

    def two_tiles():
        _peer_coef(2 * k, u_ref, 0, h2t_ref, *route, coef_a)
        _peer_coef(2 * k + 1, u_ref, PEER_E, h2t_ref, *route, coef_b)
        return jnp.dot(vtc_ref[...], coef_a[...], preferred_element_type=F32)

    @pl.when(k == 0)
    def _():
        acc[...] = two_tiles()

    @pl.when(jnp.logical_and(k > 0, k < last))
    def _():
        acc[...] += jnp.dot(vtp_ref[...], coef_b[...], preferred_element_type=F32)
        acc[...] += two_tiles()

    @pl.when(k == last)
    def _():
        tot = acc[...] + jnp.dot(vtp_ref[...], coef_b[...], preferred_element_type=F32)
        x2 = x1_ref[...] + gt2_ref[...] * jnp.transpose(tot)
        y_ref[...] = _rms(x2) * gf_ref[...]


def _peer(h2t, u_bf, vt_bf, rank1, b1, cnt0, a0, x1, gt2, gfin, per_token, rows_per_batch):
    n = x1.shape[0]
    t = PEER_T
    tpb = max(rows_per_batch // t, 1)
    rows = PEER_HEADS * N_KEYS
    steps = N_EXPERTS // (2 * PEER_E)
    once = pl.Buffered(1)
    tok = lambda r: pl.BlockSpec((r, t), lambda m, k: (0, m), pipeline_mode=once)
    if per_token:
        gt_spec = pl.BlockSpec((t, D_MODEL), lambda m, k: (m, 0), pipeline_mode=once)
    else:
        gt_spec = pl.BlockSpec((None, 1, D_MODEL), lambda m, k: (m // tpb, 0, 0))
    return pl.pallas_call(
        _peer_kernel,
        grid=(n // t, steps + 1),
        in_specs=[
            pl.BlockSpec((D_MODEL, t), lambda m, k: (0, m)),
            pl.BlockSpec((2 * PEER_E, D_MODEL), lambda m, k: (jnp.minimum(k, steps - 1), 0)),
            pl.BlockSpec((None, D_MODEL, PEER_E), lambda m, k: (jnp.maximum(2 * k - 1, 0), 0, 0)),
            pl.BlockSpec((None, D_MODEL, PEER_E), lambda m, k: (2 * jnp.minimum(k, steps - 1), 0, 0)),
            tok(rows), tok(rows), tok(rows), tok(rows),
            pl.BlockSpec((t, D_MODEL), lambda m, k: (m, 0), pipeline_mode=once),
            gt_spec,
            pl.BlockSpec((1, D_MODEL), lambda m, k: (0, 0)),
        ],
        out_specs=pl.BlockSpec((t, D_MODEL), lambda m, k: (m, 0)),
        out_shape=jax.ShapeDtypeStruct((n, D_MODEL), F32),
        scratch_shapes=[pltpu.VMEM((D_MODEL, t), F32), pltpu.VMEM((PEER_E, t), BF16), pltpu.VMEM((PEER_E, t), BF16)],
        compiler_params=_cparams(("parallel", "arbitrary"), VMEM_LIMIT_BIG),
        name="peer",
    )(h2t, u_bf, vt_bf, vt_bf, rank1, b1, cnt0, a0, x1, gt2, gfin)


def _rel_bias_tile(rel_bias, rows, cols, offset):
    rb = rel_bias.astype(F32)
    heads = rb.shape[0]
    rel_max = offset + rows - 1
    rel_min = offset - (cols - 1)
    lo, hi = max(rel_min, -REL_CLIP), min(rel_max, REL_CLIP)
    parts = []
    if rel_min < -REL_CLIP:
        parts.append(jnp.broadcast_to(rb[:, :1], (heads, -REL_CLIP - rel_min)))
    parts.append(rb[:, lo + REL_CLIP:hi + REL_CLIP + 1])
    if rel_max > REL_CLIP:
        parts.append(jnp.broadcast_to(rb[:, -1:], (heads, rel_max - REL_CLIP)))
    g = jnp.flip(jnp.concatenate(parts, axis=1), axis=1)
    period = rows + cols
    gp = jnp.pad(g, ((0, 0), (0, 1)))
    shifted = jnp.tile(gp, (1, rows))[:, :rows * (period - 1)].reshape(heads, rows, period - 1)
    return shifted[:, :, rows - 1:rows - 1 + cols]


def _prompt_bias(rel_bias):
    a = jnp.arange(Q_SUB, dtype=jnp.int32)[:, None]
    c = jnp.arange(K_WIN, dtype=jnp.int32)[None, :]
    cq = a // CHUNK
    ck = c // CHUNK
    ok = (ck >= cq) & (ck <= cq + N_LEFT_CHUNKS)
    return jnp.where(ok[None], _rel_bias_tile(rel_bias, Q_SUB, K_WIN, BAND_LEFT), NEG_INF)


def _sample_bias(rel_bias, seq, lc):
    return _rel_bias_tile(rel_bias, seq, lc, lc), _rel_bias_tile(rel_bias, seq, seq, 0)


def _layer(x, mod, cache_k, cache_v, state, is_sample, wts):
    (g1, w_main, w_alr, rel_bias, wup, balpha, gnorm, wa, wb, wo, g2, wqk_t, u_bf, vt_bf, gfin) = wts
    batch, seq, _ = x.shape
    n = batch * seq
    xf = x.reshape(n, D_MODEL)
    parts = [mod[:, i * D_MODEL:(i + 1) * D_MODEL] for i in range(6)]
    if is_sample:
        sh1, sc1, gt1, sh2, sc2, gt2 = [jnp.repeat(p, seq, axis=0) for p in parts]
        tm_in, tm_mg = n, 256
    else:
        sh1, sc1, gt1, sh2, sc2, gt2 = [p.reshape(batch, 1, D_MODEL) for p in parts]
        tm_in, tm_mg = 1024, 256

    z, alr = _inproj(xf, sh1, sc1, g1, w_main, w_alr, tm_in, is_sample, seq)

    if is_sample:
        lc = cache_k.shape[1]
        bias_c, bias_n = _sample_bias(rel_bias, seq, lc)
        oa = _attn_sample(z, cache_k.reshape(batch, lc * H_A, DH_A), cache_v.reshape(batch, lc * H_A, DH_A),
                          bias_c, bias_n, batch, seq)
        ob, s_new = _gla_sample(z, alr, state, wup, balpha, gnorm, batch, seq)
        k_rows = z[:, Z_KA:Z_KA + W_A].reshape(batch, seq, H_A, DH_A)
        v_rows = z[:, Z_VA:Z_VA + W_A].reshape(batch, seq, H_A, DH_A)
    else:
        oa = _attn_prompt(z, _prompt_bias(rel_bias), batch, seq)
        ob, s_new = _gla_prompt(z, alr, wup, balpha, gnorm, batch, seq)
        keep = min(BAND_LEFT, seq)
        zk = z.reshape(batch, seq, Z_WIDTH)[:, seq - keep:]
        k_rows = zk[:, :, Z_KA:Z_KA + W_A].reshape(batch, keep, H_A, DH_A)
        v_rows = zk[:, :, Z_VA:Z_VA + W_A].reshape(batch, keep, H_A, DH_A)

    x1, h2t = _merge(oa, ob, z, xf, gt1, sc2, sh2, g2, wa, wb, wo, tm_mg, is_sample, seq)
    rank1, b1, cnt0, a0 = _route(h2t, wqk_t)
    y = _peer(h2t, u_bf, vt_bf, rank1, b1, cnt0, a0, x1, gt2, gfin, is_sample, seq)
    return y.reshape(batch, seq, D_MODEL), k_rows, v_rows, s_new


def kernel(x_prompt, x_sample, cache_a_k, cache_a_v, state_gla, c_prompt, c_sample, w_ada, b_ada, g_norm1, w_in, rel_bias, w_alpha_up, b_alpha, g_gla_norm, w_branch_a, w_branch_b, w_out, g_norm2, w_peer_q, peer_subkeys, peer_u, peer_v, g_final):
    depth = w_in.shape[0]
    assert depth == 1, "the final rmsnorm is fused into the PEER kernel of the only layer"
    nbp = c_prompt.shape[0]
    l = 0
    nbs = c_sample.shape[0]
    pad = -(nbp + nbs) % 16
    c_all = jnp.concatenate([c_prompt, c_sample, jnp.zeros((pad, D_MODEL), F32)], axis=0)
    def only(w):
        return w.reshape(w.shape[1:])

    mod = _ada(c_all, only(w_ada), b_ada[l])

    alr_lo = Z_GL
    w_in0 = only(w_in)
    w_main = jnp.concatenate([w_in0[:, :alr_lo].astype(BF16), w_in0[:, alr_lo + ALPHA_RANK:].astype(BF16)], axis=1)
    w_alr = jnp.pad(w_in0[:, alr_lo:alr_lo + ALPHA_RANK], ((0, 0), (0, ALR_PAD - ALPHA_RANK))).astype(BF16)
    wup = jnp.pad(w_alpha_up[l], ((0, ALR_PAD - ALPHA_RANK), (0, 0))).astype(BF16)
    wqk = _qkfold(only(w_peer_q).astype(BF16), peer_subkeys[l].astype(BF16))
    wqk_t = wqk.reshape(D_MODEL, PEER_HEADS, 2, N_KEYS).transpose(2, 3, 1, 0).reshape(2 * PEER_HEADS * N_KEYS, D_MODEL)
    wts = (
        g_norm1[l].reshape(1, D_MODEL), w_main, w_alr, rel_bias[l], wup,
        b_alpha[l].reshape(1, KW_B), g_gla_norm[l].reshape(1, DV_B),
        only(w_branch_a).astype(BF16), only(w_branch_b).astype(BF16), only(w_out).astype(BF16),
        g_norm2[l].reshape(1, D_MODEL), wqk_t.astype(BF16),
        only(peer_u).astype(BF16),
        jnp.transpose(peer_v.astype(BF16).reshape(N_EXPERTS // PEER_E, PEER_E, D_MODEL), (0, 2, 1)),
        g_final.reshape(1, D_MODEL),
    )
    yp, kp, vp, sp = _layer(x_prompt, mod[:nbp], None, None, None, False, wts)
    ys, ks, vs, ss = _layer(x_sample, mod[nbp:nbp + nbs], only(cache_a_k), only(cache_a_v), only(state_gla), True, wts)
    return (yp, ys, kp[None], vp[None], sp[None], ks[None], vs[None], ss[None])
```

```python
import functools
import math

import jax
import jax.numpy as jnp
from jax import lax
from jax.experimental import pallas as pl
from jax.experimental.pallas import tpu as pltpu

F32 = jnp.float32
BF16 = jnp.bfloat16

D_MODEL = 2048
CHUNK = 64
N_LEFT_CHUNKS = 8
BAND_LEFT = N_LEFT_CHUNKS * CHUNK
H_A = 8
DH_A = 128
W_A = H_A * DH_A
REL_CLIP = 128
H_B = 4
DK_B = 128
DV_B = 256
KW_B = H_B * DK_B
VW_B = H_B * DV_B
ALPHA_RANK = 16
GATE_TAU = 16.0
PEER_HEADS = 8
N_KEYS = 128
N_EXPERTS = N_KEYS * N_KEYS
PEER_TOPK = 16
EPS = 1e-6
NEG_INF = -1e30

Z_QA, Z_KA, Z_VA = 0, W_A, 2 * W_A
Z_QB = 3 * W_A
Z_KB = Z_QB + KW_B
Z_VB = Z_KB + KW_B
Z_RB = Z_VB + VW_B
Z_GL = Z_RB + VW_B
Z_WIDTH = Z_GL + 2 * D_MODEL
ALR_PAD = 128

VMEM_LIMIT_BIG = 56 * 1024 * 1024
VMEM_LIMIT_MID = 40 * 1024 * 1024

Q_TILE = 512
Q_SUB = 128
K_WIN = Q_SUB + BAND_LEFT


def _cparams(sem, vmem=None):
    return pltpu.CompilerParams(dimension_semantics=sem, vmem_limit_bytes=vmem)


def _rms(xf):
    return xf * lax.rsqrt(jnp.mean(xf * xf, axis=-1, keepdims=True) + EPS)


def _sigmoid(x):
    return 1.0 / (1.0 + jnp.exp(-x))


def _ada_kernel(c_ref, w_ref, b_ref, o_ref):
    c = c_ref[...]
    a = (c * _sigmoid(c)).astype(BF16)
    o_ref[...] = jnp.dot(a, w_ref[...].astype(BF16), preferred_element_type=F32) + b_ref[...]


def _ada(c_all, w_ada, b_ada):
    nb = c_all.shape[0]
    n_out = w_ada.shape[1]
    tn = 1024
    return pl.pallas_call(
        _ada_kernel,
        grid=(n_out // tn,),
        in_specs=[
            pl.BlockSpec((nb, D_MODEL), lambda j: (0, 0)),
            pl.BlockSpec((D_MODEL, tn), lambda j: (0, j)),
            pl.BlockSpec((1, tn), lambda j: (0, j)),
        ],
        out_specs=pl.BlockSpec((nb, tn), lambda j: (0, j)),
        out_shape=jax.ShapeDtypeStruct((nb, n_out), F32),
        compiler_params=_cparams(("arbitrary",), VMEM_LIMIT_MID),
        name="ada",
    )(c_all, w_ada, b_ada.reshape(1, n_out))


def _mod_spec(per_token, tm, tiles_per_batch, grid_rank):
    if per_token:
        if grid_rank == 1:
            return pl.BlockSpec((tm, D_MODEL), lambda m: (m, 0))
        return pl.BlockSpec((tm, D_MODEL), lambda m, n: (m, 0))
    if grid_rank == 1:
        return pl.BlockSpec((None, 1, D_MODEL), lambda m: (m // tiles_per_batch, 0, 0))
    return pl.BlockSpec((None, 1, D_MODEL), lambda m, n: (m // tiles_per_batch, 0, 0))


def _inproj_kernel(x_ref, sh_ref, sc_ref, g_ref, wlo_ref, whi_ref, walr_ref, z_ref, alr_ref, h_scr, *, lo_tiles):
    j = pl.program_id(1)

    @pl.when(j == 0)
    def _():
        h = _rms(x_ref[...]) * g_ref[...]
        h = h * (1.0 + sc_ref[...]) + sh_ref[...]
        hb = h.astype(BF16)
        h_scr[...] = hb
        alr_ref[...] = jnp.dot(hb, walr_ref[...], preferred_element_type=F32)

    @pl.when(j < lo_tiles)
    def _():
        z_ref[...] = jnp.dot(h_scr[...], wlo_ref[...], preferred_element_type=F32)

    @pl.when(j >= lo_tiles)
    def _():
        z_ref[...] = jnp.dot(h_scr[...], whi_ref[...], preferred_element_type=F32)


def _inproj(x, sh, sc, g, w_lo, w_hi, w_alr, tm, per_token, rows_per_batch):
    n = x.shape[0]
    tn = 1024
    tpb = max(rows_per_batch // tm, 1)
    lo_tiles = w_lo.shape[1] // tn
    return pl.pallas_call(
        functools.partial(_inproj_kernel, lo_tiles=lo_tiles),
        grid=(n // tm, Z_WIDTH // tn),
        in_specs=[
            pl.BlockSpec((tm, D_MODEL), lambda m, j: (m, 0)),
            _mod_spec(per_token, tm, tpb, 2),
            _mod_spec(per_token, tm, tpb, 2),
            pl.BlockSpec((1, D_MODEL), lambda m, j: (0, 0)),
            pl.BlockSpec((D_MODEL, tn), lambda m, j: (0, jnp.minimum(j, lo_tiles - 1))),
            pl.BlockSpec((D_MODEL, tn), lambda m, j: (0, jnp.maximum(j - lo_tiles, 0))),
            pl.BlockSpec((D_MODEL, ALR_PAD), lambda m, j: (0, 0)),
        ],
        out_specs=[
            pl.BlockSpec((tm, tn), lambda m, j: (m, j)),
            pl.BlockSpec((tm, ALR_PAD), lambda m, j: (m, 0)),
        ],
        out_shape=[
            jax.ShapeDtypeStruct((n, Z_WIDTH), F32),
            jax.ShapeDtypeStruct((n, ALR_PAD), F32),
        ],
        scratch_shapes=[pltpu.VMEM((tm, D_MODEL), BF16)],
        compiler_params=_cparams(("parallel", "arbitrary"), VMEM_LIMIT_BIG),
        name="inproj",
    )(x, sh, sc, g, w_lo, w_hi, w_alr)


def _attn_prompt_kernel(q_ref, kp_ref, kc_ref, vp_ref, vc_ref, bias_ref, o_ref, kw, vw):
    first = pl.program_id(2) == 0
    kw[0:Q_TILE, :] = kp_ref[...].astype(BF16)
    kw[Q_TILE:2 * Q_TILE, :] = kc_ref[...].astype(BF16)
    vw[0:Q_TILE, :] = vp_ref[...].astype(BF16)
    vw[Q_TILE:2 * Q_TILE, :] = vc_ref[...].astype(BF16)
    scale = DH_A ** -0.5
    probs_ = [(hh, r) for hh in range(ATT_HEADS) for r in range(Q_TILE // Q_SUB)]
    hcol = [slice(hh * DH_A, (hh + 1) * DH_A) for hh in range(ATT_HEADS)]
    scores = [lax.dot_general((q_ref[r * Q_SUB:(r + 1) * Q_SUB, hcol[hh]] * scale).astype(BF16),
                              kw[r * Q_SUB:r * Q_SUB + K_WIN, hcol[hh]],
                              (((1,), (1,)), ((), ())), preferred_element_type=F32) for (hh, r) in probs_]
    probs, sums = [], []
    col = lax.broadcasted_iota(jnp.int32, (1, K_WIN), 1)
    for idx, (hh, r) in enumerate(probs_):
        before = jnp.where(jnp.logical_and(first, col + r * Q_SUB < Q_TILE), NEG_INF, 0.0)
        s = scores[idx] + bias_ref[hh] + before
        m = jnp.max(s, axis=-1, keepdims=True)
        p = jnp.exp(s - m)
        sums.append(jnp.sum(p, axis=-1, keepdims=True))
        probs.append(p.astype(BF16))
    outs = [jnp.dot(probs[idx], vw[r * Q_SUB:r * Q_SUB + K_WIN, hcol[hh]], preferred_element_type=F32)
            for idx, (hh, r) in enumerate(probs_)]
    for idx, (hh, r) in enumerate(probs_):
        o_ref[r * Q_SUB:(r + 1) * Q_SUB, hcol[hh]] = (outs[idx] / sums[idx]).astype(o_ref.dtype)


ATT_HEADS = 4


def _attn_prompt(z, bias, batch, seq):
    n = batch * seq
    tiles = seq // Q_TILE
    width = ATT_HEADS * DH_A

    def cur(col):
        return pl.BlockSpec((Q_TILE, width), lambda h, b, i: (b * tiles + i, col + h))

    def prev(col):
        return pl.BlockSpec((Q_TILE, width),
                            lambda h, b, i: (b * tiles + jnp.maximum(i - 1, 0), col + h))

    kcol, vcol = Z_KA // width, Z_VA // width
    return pl.pallas_call(
        _attn_prompt_kernel,
        grid=(H_A // ATT_HEADS, batch, tiles),
        in_specs=[cur(0), prev(kcol), cur(kcol), prev(vcol), cur(vcol),
                  pl.BlockSpec((ATT_HEADS, Q_SUB, K_WIN), lambda h, b, i: (h, 0, 0))],
        out_specs=pl.BlockSpec((Q_TILE, width), lambda h, b, i: (b * tiles + i, h)),
        out_shape=jax.ShapeDtypeStruct((n, W_A), BF16),
        scratch_shapes=[pltpu.VMEM((2 * Q_TILE, width), BF16), pltpu.VMEM((2 * Q_TILE, width), BF16)],
        compiler_params=_cparams(("arbitrary", "arbitrary", "arbitrary")),
        name="attn_prompt",
    )(z, z, z, z, z, bias)


def _attn_sample_kernel(q_ref, kn_ref, vn_ref, ck_ref, cv_ref, bc_ref, bn_ref, o_ref):
    scale = DH_A ** -0.5
    nt = (((1,), (1,)), ((), ()))
    lc = ck_ref.shape[0] // H_A
    heads = range(H_A)
    cols = [slice(h * DH_A, (h + 1) * DH_A) for h in heads]
    crow = [pl.ds(h, lc, stride=H_A) for h in heads]
    qs = [q_ref[:, cols[h]].astype(BF16) for h in heads]
    s_c = [lax.dot_general(qs[h], ck_ref[crow[h], :].astype(BF16), nt, preferred_element_type=F32) for h in heads]
    s_n = [lax.dot_general(qs[h], kn_ref[:, cols[h]].astype(BF16), nt, preferred_element_type=F32) for h in heads]
    p_c, p_n, sums = [], [], []
    for h in heads:
        sc = s_c[h] * scale + bc_ref[h]
        sn = s_n[h] * scale + bn_ref[h]
        m = jnp.maximum(jnp.max(sc, axis=-1, keepdims=True), jnp.max(sn, axis=-1, keepdims=True))
        pc = jnp.exp(sc - m)
        pn = jnp.exp(sn - m)
        sums.append(jnp.sum(pc, axis=-1, keepdims=True) + jnp.sum(pn, axis=-1, keepdims=True))
        p_c.append(pc.astype(BF16))
        p_n.append(pn.astype(BF16))
    outs = [jnp.dot(p_c[h], cv_ref[crow[h], :].astype(BF16), preferred_element_type=F32)
            + jnp.dot(p_n[h], vn_ref[:, cols[h]].astype(BF16), preferred_element_type=F32) for h in heads]
    for h in heads:
        o_ref[:, cols[h]] = (outs[h] / sums[h]).astype(o_ref.dtype)


def _attn_sample(z, cache_k, cache_v, bias_c, bias_n, batch, seq):
    lc = cache_k.shape[1] // H_A
    return pl.pallas_call(
        _attn_sample_kernel,
        grid=(batch,),
        in_specs=[
            pl.BlockSpec((seq, W_A), lambda b: (b, Z_QA // W_A)),
            pl.BlockSpec((seq, W_A), lambda b: (b, Z_KA // W_A)),
            pl.BlockSpec((seq, W_A), lambda b: (b, Z_VA // W_A)),
            pl.BlockSpec((None, lc * H_A, DH_A), lambda b: (b, 0, 0)),
            pl.BlockSpec((None, lc * H_A, DH_A), lambda b: (b, 0, 0)),
            pl.BlockSpec((H_A, seq, lc), lambda b: (0, 0, 0)),
            pl.BlockSpec((H_A, seq, seq), lambda b: (0, 0, 0)),
        ],
        out_specs=pl.BlockSpec((seq, W_A), lambda b: (b, 0)),
        out_shape=jax.ShapeDtypeStruct((batch * seq, W_A), BF16),
        compiler_params=_cparams(("arbitrary",), VMEM_LIMIT_MID),
        name="attn_sample",
    )(z, z, z, cache_k, cache_v, bias_c, bias_n)


def _split_bf16(x):
    hi = x.astype(BF16)
    lo = (x - hi.astype(F32)).astype(BF16)
    return hi, lo


_TN = (((0,), (0,)), ((), ()))
_NT = (((1,), (1,)), ((), ()))


def _gla_state_free(units, tri, causal, c):
    scale = DK_B ** -0.5
    ones = jnp.ones((c, DK_B), BF16)
    split = [_split_bf16(g) for (_, _, _, g) in units]
    cum = [jnp.dot(tri, hi, preferred_element_type=F32) + jnp.dot(tri, lo, preferred_element_type=F32)
           for (hi, lo) in split]
    tot = [lax.dot_general(hi, ones, _TN, preferred_element_type=F32)
           + lax.dot_general(lo, ones, _TN, preferred_element_type=F32) for (hi, lo) in split]
    qd, kd, vb, att = [], [], [], []
    for (q, k, v, _), b in zip(units, cum):
        mid = b[c // 2:c // 2 + 1, :]
        blast = b[c - 1:c, :]
        qs = q * scale
        ks = k * scale
        att.append(lax.dot_general((qs * jnp.exp(b - mid)).astype(BF16), (ks * jnp.exp(mid - b)).astype(BF16),
                                   _NT, preferred_element_type=F32))
        qd.append((qs * jnp.exp(b)).astype(BF16))
        kd.append((ks * jnp.exp(blast - b)).astype(BF16))
        vb.append(v.astype(BF16))
    intra = [jnp.dot(jnp.where(causal, a, 0.0).astype(BF16), v, preferred_element_type=F32)
             for a, v in zip(att, vb)]
    out = []
    for i in range(len(units)):
        dec = jnp.exp(tot[i])
        out.append((intra[i], qd[i], kd[i], vb[i], jnp.concatenate([dec, dec], axis=1)))
    return out


def _gla_read_state(pre, s_prev):
    o_intra, qd, _, _, _ = pre
    return o_intra + jnp.dot(qd, s_prev.astype(BF16), preferred_element_type=F32)


def _gla_next_state(pre, s_prev):
    _, _, kd, vb, dec = pre
    return dec * s_prev + lax.dot_general(kd, vb, _TN, preferred_element_type=F32)


def _log_decay(alr, wup, balpha):
    x = jnp.dot(alr.astype(BF16), wup, preferred_element_type=F32) + balpha
    return (jnp.minimum(x, 0.0) - jnp.log1p(jnp.exp(-jnp.abs(x)))) * (1.0 / GATE_TAU)


def _gla_epilogue(o, rb, gnorm):
    y = _rms(o) * gnorm
    return y * (rb * _sigmoid(rb))


def _tri_and_causal(c):
    row = lax.broadcasted_iota(jnp.int32, (c, c), 0)
    col = lax.broadcasted_iota(jnp.int32, (c, c), 1)
    causal = col <= row
    return jnp.where(causal, 1.0, 0.0).astype(BF16), causal


def _gla_prompt_kernel(q_ref, k_ref, v_ref, rb_ref, alr_ref, wup_ref, balpha_ref, gnorm_ref,
                       ob_ref, st_ref, s_scr, *, batch, blocks):
    step = pl.program_id(0)

    @pl.when(step == 0)
    def _():
        s_scr[...] = jnp.zeros_like(s_scr)

    tri, causal = _tri_and_causal(CHUNK)
    wup = wup_ref[...]
    balpha = balpha_ref[...]
    gnorm = gnorm_ref[...]
    ksl = [slice(h * DK_B, (h + 1) * DK_B) for h in range(H_B)]
    vsl = [slice(h * DV_B, (h + 1) * DV_B) for h in range(H_B)]
    streams = [(bi, h) for bi in range(batch) for h in range(H_B)]
    rows = [slice(blk * CHUNK, (blk + 1) * CHUNK) for blk in range(blocks)]
    logg = {(blk, bi): _log_decay(alr_ref[bi, rows[blk], :], wup, balpha)
            for blk in range(blocks) for bi in range(batch)}
    units = [(q_ref[bi, rows[blk], ksl[h]], k_ref[bi, rows[blk], ksl[h]], v_ref[bi, rows[blk], vsl[h]],
              logg[(blk, bi)][:, ksl[h]]) for blk in range(blocks) for (bi, h) in streams]
    pre = _gla_state_free(units, tri, causal, CHUNK)
    for blk in range(blocks):
        mine = pre[blk * len(streams):(blk + 1) * len(streams)]
        outs = [_gla_read_state(p, s_scr[si]) for si, p in enumerate(mine)]
        nxt = [_gla_next_state(p, s_scr[si]) for si, p in enumerate(mine)]
        for si, (bi, h) in enumerate(streams):
            s_scr[si] = nxt[si]
            ob_ref[bi, rows[blk], vsl[h]] = _gla_epilogue(outs[si], rb_ref[bi, rows[blk], vsl[h]],
                                                          gnorm).astype(ob_ref.dtype)

    @pl.when(step == pl.num_programs(0) - 1)
    def _():
        st_ref[...] = s_scr[...]


def _gla_prompt(z, alr, wup, balpha, gnorm, batch, seq):
    blocks = 4
    rows = blocks * CHUNK
    z3 = z.reshape(batch, seq, Z_WIDTH)
    alr3 = alr.reshape(batch, seq, ALR_PAD)
    ob, st = pl.pallas_call(
        functools.partial(_gla_prompt_kernel, batch=batch, blocks=blocks),
        grid=(seq // rows,),
        in_specs=[
            pl.BlockSpec((batch, rows, KW_B), lambda j: (0, j, Z_QB // KW_B)),
            pl.BlockSpec((batch, rows, KW_B), lambda j: (0, j, Z_KB // KW_B)),
            pl.BlockSpec((batch, rows, VW_B), lambda j: (0, j, Z_VB // VW_B)),
            pl.BlockSpec((batch, rows, VW_B), lambda j: (0, j, Z_RB // VW_B)),
            pl.BlockSpec((batch, rows, ALR_PAD), lambda j: (0, j, 0)),
            pl.BlockSpec((ALR_PAD, KW_B), lambda j: (0, 0)),
            pl.BlockSpec((1, KW_B), lambda j: (0, 0)),
            pl.BlockSpec((1, DV_B), lambda j: (0, 0)),
        ],
        out_specs=[
            pl.BlockSpec((batch, rows, VW_B), lambda j: (0, j, 0)),
            pl.BlockSpec((batch * H_B, DK_B, DV_B), lambda j: (0, 0, 0)),
        ],
        out_shape=[
            jax.ShapeDtypeStruct((batch, seq, VW_B), BF16),
            jax.ShapeDtypeStruct((batch * H_B, DK_B, DV_B), F32),
        ],
        scratch_shapes=[pltpu.VMEM((batch * H_B, DK_B, DV_B), F32)],
        compiler_params=_cparams(("arbitrary",), VMEM_LIMIT_MID),
        name="gla_prompt",
    )(z3, z3, z3, z3, alr3, wup, balpha, gnorm)
    return ob.reshape(batch * seq, VW_B), st.reshape(batch, H_B, DK_B, DV_B)


def _gla_sample_kernel(q_ref, k_ref, v_ref, rb_ref, alr_ref, s0_ref, wup_ref, balpha_ref, gnorm_ref,
                       ob_ref, st_ref, *, seq):
    tri, causal = _tri_and_causal(seq)
    logg = _log_decay(alr_ref[...], wup_ref[...], balpha_ref[...])
    gnorm = gnorm_ref[...]
    ksl = [slice(h * DK_B, (h + 1) * DK_B) for h in range(H_B)]
    vsl = [slice(h * DV_B, (h + 1) * DV_B) for h in range(H_B)]
    pre = _gla_state_free([(q_ref[:, ksl[h]], k_ref[:, ksl[h]], v_ref[:, vsl[h]], logg[:, ksl[h]])
                           for h in range(H_B)], tri, causal, seq)
    outs = [_gla_read_state(pre[h], s0_ref[h]) for h in range(H_B)]
    nxt = [_gla_next_state(pre[h], s0_ref[h]) for h in range(H_B)]
    for h in range(H_B):
        st_ref[h] = nxt[h]
        ob_ref[:, vsl[h]] = _gla_epilogue(outs[h], rb_ref[:, vsl[h]], gnorm).astype(ob_ref.dtype)


def _gla_sample(z, alr, state, wup, balpha, gnorm, batch, seq):
    return pl.pallas_call(
        functools.partial(_gla_sample_kernel, seq=seq),
        grid=(batch,),
        in_specs=[
            pl.BlockSpec((seq, KW_B), lambda b: (b, Z_QB // KW_B)),
            pl.BlockSpec((seq, KW_B), lambda b: (b, Z_KB // KW_B)),
            pl.BlockSpec((seq, VW_B), lambda b: (b, Z_VB // VW_B)),
            pl.BlockSpec((seq, VW_B), lambda b: (b, Z_RB // VW_B)),
            pl.BlockSpec((seq, ALR_PAD), lambda b: (b, 0)),
            pl.BlockSpec((None, H_B, DK_B, DV_B), lambda b: (b, 0, 0, 0)),
            pl.BlockSpec((ALR_PAD, KW_B), lambda b: (0, 0)),
            pl.BlockSpec((1, KW_B), lambda b: (0, 0)),
            pl.BlockSpec((1, DV_B), lambda b: (0, 0)),
        ],
        out_specs=[
            pl.BlockSpec((seq, VW_B), lambda b: (b, 0)),
            pl.BlockSpec((None, H_B, DK_B, DV_B), lambda b: (b, 0, 0, 0)),
        ],
        out_shape=[
            jax.ShapeDtypeStruct((batch * seq, VW_B), BF16),
            jax.ShapeDtypeStruct((batch, H_B, DK_B, DV_B), F32),
        ],
        compiler_params=_cparams(("arbitrary",)),
        name="gla_sample",
    )(z, z, z, z, alr, state, wup, balpha, gnorm)


def _merge_kernel(oa_ref, ob_ref, gla_ref, glb_ref, x_ref, gt1_ref, sc2_ref, sh2_ref, g2_ref,
                  wa_ref, wb_ref, wo_ref, x1_ref, h2t_ref):
    ya = jnp.dot(oa_ref[...], wa_ref[...], preferred_element_type=F32)
    yb = jnp.dot(ob_ref[...], wb_ref[...], preferred_element_type=F32)
    merged = _sigmoid(gla_ref[...]) * ya + _sigmoid(glb_ref[...]) * yb
    x1 = x_ref[...] + gt1_ref[...] * jnp.dot(merged.astype(BF16), wo_ref[...], preferred_element_type=F32)
    x1_ref[...] = x1
    h2 = _rms(x1) * g2_ref[...]
    h2 = h2 * (1.0 + sc2_ref[...]) + sh2_ref[...]
    h2t_ref[...] = jnp.transpose(h2).astype(BF16)


def _merge(oa, ob, z, x, gt1, sc2, sh2, g2, wa, wb, wo, tm, per_token, rows_per_batch):
    n = x.shape[0]
    tpb = max(rows_per_batch // tm, 1)
    const = lambda shape: pl.BlockSpec(shape, lambda m: (0, 0))
    mod = _mod_spec(per_token, tm, tpb, 1)
    return pl.pallas_call(
        _merge_kernel,
        grid=(n // tm,),
        in_specs=[
            pl.BlockSpec((tm, W_A), lambda m: (m, 0)),
            pl.BlockSpec((tm, VW_B), lambda m: (m, 0)),
            pl.BlockSpec((tm, D_MODEL), lambda m: (m, Z_GL // D_MODEL)),
            pl.BlockSpec((tm, D_MODEL), lambda m: (m, Z_GL // D_MODEL + 1)),
            pl.BlockSpec((tm, D_MODEL), lambda m: (m, 0)),
            mod, mod, mod,
            const((1, D_MODEL)),
            const((W_A, D_MODEL)), const((VW_B, D_MODEL)), const((D_MODEL, D_MODEL)),
        ],
        out_specs=[
            pl.BlockSpec((tm, D_MODEL), lambda m: (m, 0)),
            pl.BlockSpec((D_MODEL, tm), lambda m: (0, m)),
        ],
        out_shape=[
            jax.ShapeDtypeStruct((n, D_MODEL), F32),
            jax.ShapeDtypeStruct((D_MODEL, n), BF16),
        ],
        compiler_params=_cparams(("arbitrary",), VMEM_LIMIT_BIG),
        name="merge",
    )(oa, ob, z, z, x, gt1, sc2, sh2, g2, wa, wb, wo)


def _qkfold_kernel(wq_ref, sk_ref, o_ref):
    o_ref[...] = lax.dot_general(wq_ref[...], sk_ref[...], (((1,), (1,)), ((), ())),
                                 preferred_element_type=F32)


def _qkfold(w_q, subkeys):
    half = N_KEYS
    return pl.pallas_call(
        _qkfold_kernel,
        grid=(2 * PEER_HEADS,),
        in_specs=[
            pl.BlockSpec((D_MODEL, half), lambda c: (0, c)),
            pl.BlockSpec((None, N_KEYS, half), lambda c: (c % 2, 0, 0)),
        ],
        out_specs=pl.BlockSpec((D_MODEL, N_KEYS), lambda c: (0, c)),
        out_shape=jax.ShapeDtypeStruct((D_MODEL, 2 * PEER_HEADS * N_KEYS), F32),
        compiler_params=_cparams(("arbitrary",)),
        name="qkfold",
    )(w_q, subkeys)


ROUTE_T = 256
LANES = 128
_SET_ROWS = N_KEYS * PEER_HEADS
_PAIRS = [(r, q) for r in range(PEER_TOPK) for q in range(PEER_TOPK) if (r + 1) * (q + 1) <= PEER_TOPK]


def _sort_desc(x):
    x = list(x)
    n = len(x)
    k = 2
    while k <= n:
        j = k // 2
        while j >= 1:
            for i in range(n):
                l = i ^ j
                if l > i:
                    hi, lo = jnp.maximum(x[i], x[l]), jnp.minimum(x[i], x[l])
                    x[i], x[l] = (hi, lo) if (i & k) == 0 else (lo, hi)
            j //= 2
        k *= 2
    return x


def _merge_top(a, b):
    n = len(a)
    x = [jnp.maximum(a[i], b[n - 1 - i]) for i in range(n)]
    j = n // 2
    while j >= 1:
        for i in range(n):
            l = i ^ j
            if l > i:
                x[i], x[l] = jnp.maximum(x[i], x[l]), jnp.minimum(x[i], x[l])
        j //= 2
    return x


def _top_sorted(vals, top):
    groups = [_sort_desc(vals[g:g + top]) for g in range(0, len(vals), top)]
    while len(groups) > 1:
        groups = [_merge_top(groups[g], groups[g + 1]) for g in range(0, len(groups), 2)]
    return groups[0]


def _route_fast(s0, s1):
    shp = s0[0].shape
    one = jnp.ones(shp, F32)
    zero = jnp.zeros(shp, F32)
    top0 = _top_sorted(s0, PEER_TOPK)
    top1 = _top_sorted(s1, PEER_TOPK)
    cand = [top0[r] + top1[q] for (r, q) in _PAIRS]
    pad = [jnp.full(shp, -jnp.inf, F32)] * (-len(cand) % PEER_TOPK)
    topc = _top_sorted(cand + pad, PEER_TOPK)
    tau = topc[PEER_TOPK - 1]
    cnt = [zero] * PEER_TOPK
    for c, (r, q) in enumerate(_PAIRS):
        cnt[r] = cnt[r] + jnp.where(cand[c] >= tau, one, zero)
    picked = cnt[0]
    for r in range(1, PEER_TOPK):
        picked = picked + cnt[r]
    zsum = zero
    for r in range(PEER_TOPK):
        zsum = zsum + jnp.exp(topc[r] - topc[0])
    zinv = 1.0 / zsum
    tie = jnp.where(picked != float(PEER_TOPK), one, zero)
    for t in (top0, top1):
        for r in range(PEER_TOPK - 1):
            tie = jnp.where(t[r] == t[r + 1], one, tie)
    in0, in1 = zero, zero
    rank1, b1, cnt0, a0 = [], [], [], []
    for k in range(N_KEYS):
        v0, v1 = s0[k], s1[k]
        c0 = zero
        rk = jnp.full(shp, float(PEER_TOPK), F32)
        for r in range(PEER_TOPK - 1, -1, -1):
            c0 = jnp.where(v0 >= top0[r], cnt[r], c0)
            rk = jnp.where(v1 >= top1[r], float(r), rk)
        in0 = in0 + jnp.where(v0 >= top0[PEER_TOPK - 1], one, zero)
        in1 = in1 + jnp.where(v1 >= top1[PEER_TOPK - 1], one, zero)
        cnt0.append(c0)
        rank1.append(rk)
        a0.append(jnp.exp(v0 - top0[0]) * zinv)
        b1.append(jnp.exp(v1 - top1[0]))
    tie = jnp.where(in0 != float(PEER_TOPK), one, tie)
    tie = jnp.where(in1 != float(PEER_TOPK), one, tie)
    return rank1, b1, cnt0, a0, tie


def _route_kernel(h2t_ref, wqk_ref, rank1_ref, b1_ref, cnt0_ref, a0_ref,
                  s_scr, sw_scr, rk_scr, val_scr, idx_scr, tmp_scr):
    halves = h2t_ref.shape[1] // LANES
    shp = (PEER_HEADS, LANES)
    s = jnp.dot(wqk_ref[...], h2t_ref[...], preferred_element_type=F32)
    for hf in range(halves):
        s_scr[hf] = s[:, hf * LANES:(hf + 1) * LANES]
    neg = jnp.full(shp, -jnp.inf, F32)

    def rows(p, k):
        return pl.ds(p * _SET_ROWS + k * PEER_HEADS, PEER_HEADS)

    def one_half(hf, carry0):
        f_rank1, f_b1, f_cnt0, f_a0, tie = _route_fast([s_scr[hf, rows(0, k), :] for k in range(N_KEYS)],
                                                       [s_scr[hf, rows(1, k), :] for k in range(N_KEYS)])
        for k in range(N_KEYS):
            kr = pl.ds(k * PEER_HEADS, PEER_HEADS)
            tmp_scr[hf, 0, kr, :] = f_cnt0[k]
            tmp_scr[hf, 1, kr, :] = f_a0[k]
            tmp_scr[hf, 2, kr, :] = f_b1[k]
            rk_scr[hf, rows(1, k), :] = f_rank1[k]

        @pl.when(jnp.max(tie) > 0.0)
        def _():
            exact_half(hf)

        return carry0

    def exact_half(hf):
        sw_scr[hf] = s_scr[hf]
        rk_scr[hf] = jnp.full(rk_scr.shape[1:], float(PEER_TOPK), F32)

        def extract(r, carry):
            rf = jnp.full(shp, r, jnp.int32).astype(F32)
            for p in range(2):
                best, bidx = None, None
                for part in range(4):
                    m = neg
                    ix = jnp.zeros(shp, jnp.int32)
                    for k in range(part * 32, part * 32 + 32):
                        v = sw_scr[hf, rows(p, k), :]
                        gt = v > m
                        m = jnp.where(gt, v, m)
                        ix = jnp.where(gt, k, ix)
                    if best is None:
                        best, bidx = m, ix
                    else:
                        gt = m > best
                        best = jnp.where(gt, m, best)
                        bidx = jnp.where(gt, ix, bidx)
                val_scr[p, r] = best
                idx_scr[p, r] = bidx
                for k in range(N_KEYS):
                    hit = bidx == k
                    sw_scr[hf, rows(p, k), :] = jnp.where(hit, -jnp.inf, sw_scr[hf, rows(p, k), :])
                    rk_scr[hf, rows(p, k), :] = jnp.where(hit, rf, rk_scr[hf, rows(p, k), :])
            return carry

        lax.fori_loop(0, PEER_TOPK, extract, 0)

        v0 = [val_scr[0, r] for r in range(PEER_TOPK)]
        v1 = [val_scr[1, q] for q in range(PEER_TOPK)]
        cand = [v0[r] + v1[q] for (r, q) in _PAIRS]
        top = v0[0] + v1[0]
        cnt = [jnp.zeros(shp, F32) for _ in range(PEER_TOPK)]
        zsum = jnp.zeros(shp, F32)
        for _ in range(PEER_TOPK):
            m = neg
            ix = jnp.zeros(shp, jnp.int32)
            for c, cv in enumerate(cand):
                gt = cv > m
                m = jnp.where(gt, cv, m)
                ix = jnp.where(gt, c, ix)
            zsum = zsum + jnp.exp(m - top)
            for c, (r, q) in enumerate(_PAIRS):
                hit = ix == c
                cand[c] = jnp.where(hit, -jnp.inf, cand[c])
                cnt[r] = cnt[r] + jnp.where(hit, 1.0, 0.0)
        zinv = 1.0 / zsum

        idx0 = [idx_scr[0, r] for r in range(PEER_TOPK)]
        for k in range(N_KEYS):
            c0 = jnp.zeros(shp, F32)
            for r in range(PEER_TOPK):
                c0 = jnp.where(idx0[r] == k, cnt[r], c0)
            kr = pl.ds(k * PEER_HEADS, PEER_HEADS)
            tmp_scr[hf, 0, kr, :] = c0
            tmp_scr[hf, 1, kr, :] = jnp.exp(s_scr[hf, rows(0, k), :] - v0[0]) * zinv
            tmp_scr[hf, 2, kr, :] = jnp.exp(s_scr[hf, rows(1, k), :] - v1[0])

    lax.fori_loop(0, halves, one_half, 0)

    for hf in range(halves):
        ls = slice(hf * LANES, (hf + 1) * LANES)
        for h in range(PEER_HEADS):
            dst = slice(h * N_KEYS, (h + 1) * N_KEYS)
            src = pl.ds(h, N_KEYS, stride=PEER_HEADS)
            cnt0_ref[dst, ls] = _pack_pair(tmp_scr[hf, 0, src, :])
            a0_ref[dst, ls] = _pack_pair(tmp_scr[hf, 1, src, :])
            b1_ref[dst, ls] = tmp_scr[hf, 2, src, :].astype(BF16)
            rank1_ref[dst, ls] = rk_scr[hf, pl.ds(_SET_ROWS + h, N_KEYS, stride=PEER_HEADS), :].astype(BF16)


def _pack_pair(x):
    bits = pltpu.bitcast(x.astype(BF16).astype(F32), jnp.uint32)
    return bits | (bits >> 16)


def _route(h2t, wqk_t):
    n = h2t.shape[1]
    t = ROUTE_T
    rows = PEER_HEADS * N_KEYS
    ospec = pl.BlockSpec((rows, t), lambda m: (0, m))
    out_bf = jax.ShapeDtypeStruct((rows, n), BF16)
    out_pk = jax.ShapeDtypeStruct((rows, n), jnp.uint32)
    return pl.pallas_call(
        _route_kernel,
        grid=(n // t,),
        in_specs=[
            pl.BlockSpec((D_MODEL, t), lambda m: (0, m)),
            pl.BlockSpec((2 * rows, D_MODEL), lambda m: (0, 0)),
        ],
        out_specs=[ospec, ospec, ospec, ospec],
        out_shape=[out_bf, out_bf, out_pk, out_pk],
        scratch_shapes=[
            pltpu.VMEM((t // LANES, 2 * rows, LANES), F32),
            pltpu.VMEM((t // LANES, 2 * rows, LANES), F32),
            pltpu.VMEM((t // LANES, 2 * rows, LANES), F32),
            pltpu.VMEM((2, PEER_TOPK, PEER_HEADS, LANES), F32),
            pltpu.VMEM((2, PEER_TOPK, PEER_HEADS, LANES), jnp.int32),
            pltpu.VMEM((t // LANES, 3, rows, LANES), F32),
        ],
        compiler_params=_cparams(("arbitrary",), VMEM_LIMIT_MID),
        name="route",
    )(h2t, wqk_t)


PEER_T = 512
PEER_E = 512


def _gelu(x):
    return 0.5 * x * (1.0 + lax.erf(x * (2.0 ** -0.5)))


def _bcast_pair_rows(ref, row, t):
    word = jnp.broadcast_to(ref[pl.ds(row, 1), :], (8, t))
    pair = pltpu.bitcast(word, BF16)
    return jnp.broadcast_to(pair[None], (N_KEYS // 16, 16, t)).reshape(N_KEYS, t)


def _peer_coef(tile, u_ref, row0, h2t_ref, rank1_ref, b1_ref, cnt0_ref, a0_ref, coef_ref):
    t = h2t_ref.shape[1]
    act = jnp.dot(u_ref[row0:row0 + PEER_E, :], h2t_ref[...], preferred_element_type=F32)
    per = PEER_E // N_KEYS
    for ii in range(per):
        i = tile * per + ii
        w = None
        for h in range(PEER_HEADS):
            hs = slice(h * N_KEYS, (h + 1) * N_KEYS)
            cnt = _bcast_pair_rows(cnt0_ref, h * N_KEYS + i, t)
            a = _bcast_pair_rows(a0_ref, h * N_KEYS + i, t)
            term = jnp.where(rank1_ref[hs, :] < cnt, b1_ref[hs, :] * a, jnp.zeros((), BF16))
            w = term if w is None else w + term
        rs = slice(ii * N_KEYS, (ii + 1) * N_KEYS)
        coef_ref[rs, :] = _gelu(act[rs, :]).astype(BF16) * w


def _peer_kernel(h2t_ref, u_ref, vtp_ref, vtc_ref, rank1_ref, b1_ref, cnt0_ref, a0_ref, x1_ref, gt2_ref, gf_ref,
                 y_ref, acc, coef_a, coef_b):
    k = pl.program_id(1)
    last = pl.num_programs(1) - 1
    route = (rank1_ref, b1_ref, cnt0_ref, a0_ref)

    def two_tiles():
        _peer_coef(2 * k, u_ref, 0, h2t_ref, *route, coef_a)
        _peer_coef(2 * k + 1, u_ref, PEER_E, h2t_ref, *route, coef_b)
        return jnp.dot(vtc_ref[...], coef_a[...], preferred_element_type=F32)

    @pl.when(k == 0)
    def _():
        acc[...] = two_tiles()

    @pl.when(jnp.logical_and(k > 0, k < last))
    def _():
        acc[...] += jnp.dot(vtp_ref[...], coef_b[...], preferred_element_type=F32)
        acc[...] += two_tiles()

    @pl.when(k == last)
    def _():
        tot = acc[...] + jnp.dot(vtp_ref[...], coef_b[...], preferred_element_type=F32)
        x2 = x1_ref[...] + gt2_ref[...] * jnp.transpose(tot)
        y_ref[...] = _rms(x2) * gf_ref[...]


def _peer(h2t, u_bf, vt_bf, rank1, b1, cnt0, a0, x1, gt2, gfin, per_token, rows_per_batch):
    n = x1.shape[0]
    t = PEER_T
    tpb = max(rows_per_batch // t, 1)
    rows = PEER_HEADS * N_KEYS
    steps = N_EXPERTS // (2 * PEER_E)
    once = pl.Buffered(1)
    tok = lambda r: pl.BlockSpec((r, t), lambda m, k: (0, m), pipeline_mode=once)
    if per_token:
        gt_spec = pl.BlockSpec((t, D_MODEL), lambda m, k: (m, 0), pipeline_mode=once)
    else:
        gt_spec = pl.BlockSpec((None, 1, D_MODEL), lambda m, k: (m // tpb, 0, 0))
    return pl.pallas_call(
        _peer_kernel,
        grid=(n // t, steps + 1),
        in_specs=[
            pl.BlockSpec((D_MODEL, t), lambda m, k: (0, m)),
            pl.BlockSpec((2 * PEER_E, D_MODEL), lambda m, k: (jnp.minimum(k, steps - 1), 0)),
            pl.BlockSpec((None, D_MODEL, PEER_E), lambda m, k: (jnp.maximum(2 * k - 1, 0), 0, 0)),
            pl.BlockSpec((None, D_MODEL, PEER_E), lambda m, k: (2 * jnp.minimum(k, steps - 1), 0, 0)),
            tok(rows), tok(rows), tok(rows), tok(rows),
            pl.BlockSpec((t, D_MODEL), lambda m, k: (m, 0), pipeline_mode=once),
            gt_spec,
            pl.BlockSpec((1, D_MODEL), lambda m, k: (0, 0)),
        ],
        out_specs=pl.BlockSpec((t, D_MODEL), lambda m, k: (m, 0)),
        out_shape=jax.ShapeDtypeStruct((n, D_MODEL), F32),
        scratch_shapes=[pltpu.VMEM((D_MODEL, t), F32), pltpu.VMEM((PEER_E, t), BF16), pltpu.VMEM((PEER_E, t), BF16)],
        compiler_params=_cparams(("parallel", "arbitrary"), VMEM_LIMIT_BIG),
        name="peer",
    )(h2t, u_bf, vt_bf, vt_bf, rank1, b1, cnt0, a0, x1, gt2, gfin)


def _rel_bias_tile(rel_bias, rows, cols, offset):
    rb = rel_bias.astype(F32)
    heads = rb.shape[0]
    rel_max = offset + rows - 1
    rel_min = offset - (cols - 1)
    lo, hi = max(rel_min, -REL_CLIP), min(rel_max, REL_CLIP)
    parts = []
    if rel_min < -REL_CLIP:
        parts.append(jnp.broadcast_to(rb[:, :1], (heads, -REL_CLIP - rel_min)))
    parts.append(rb[:, lo + REL_CLIP:hi + REL_CLIP + 1])
    if rel_max > REL_CLIP:
        parts.append(jnp.broadcast_to(rb[:, -1:], (heads, rel_max - REL_CLIP)))
    g = jnp.flip(jnp.concatenate(parts, axis=1), axis=1)
    period = rows + cols
    gp = jnp.pad(g, ((0, 0), (0, 1)))
    shifted = jnp.tile(gp, (1, rows))[:, :rows * (period - 1)].reshape(heads, rows, period - 1)
    return shifted[:, :, rows - 1:rows - 1 + cols]


def _prompt_bias(rel_bias):
    a = jnp.arange(Q_SUB, dtype=jnp.int32)[:, None]
    c = jnp.arange(K_WIN, dtype=jnp.int32)[None, :]
    cq = a // CHUNK
    ck = c // CHUNK
    ok = (ck >= cq) & (ck <= cq + N_LEFT_CHUNKS)
    return jnp.where(ok[None], _rel_bias_tile(rel_bias, Q_SUB, K_WIN, BAND_LEFT), NEG_INF)


def _sample_bias(rel_bias, seq, lc):
    return _rel_bias_tile(rel_bias, seq, lc, lc), _rel_bias_tile(rel_bias, seq, seq, 0)


def _layer(x, mod, cache_k, cache_v, state, is_sample, wts):
    (g1, w_lo, w_hi, w_alr, rel_bias, wup, balpha, gnorm, wa, wb, wo, g2, wqk_t, u_bf, vt_bf, gfin) = wts
    batch, seq, _ = x.shape
    n = batch * seq
    xf = x.reshape(n, D_MODEL)
    parts = [mod[:, i * D_MODEL:(i + 1) * D_MODEL] for i in range(6)]
    if is_sample:
        sh1, sc1, gt1, sh2, sc2, gt2 = [jnp.repeat(p, seq, axis=0) for p in parts]
        tm_in, tm_mg = n, 256
    else:
        sh1, sc1, gt1, sh2, sc2, gt2 = [p.reshape(batch, 1, D_MODEL) for p in parts]
        tm_in, tm_mg = 1024, 256

    z, alr = _inproj(xf, sh1, sc1, g1, w_lo, w_hi, w_alr, tm_in, is_sample, seq)

    if is_sample:
        lc = cache_k.shape[1]
        bias_c, bias_n = _sample_bias(rel_bias, seq, lc)
        oa = _attn_sample(z, cache_k.reshape(batch, lc * H_A, DH_A), cache_v.reshape(batch, lc * H_A, DH_A),
                          bias_c, bias_n, batch, seq)
        ob, s_new = _gla_sample(z, alr, state, wup, balpha, gnorm, batch, seq)
        k_rows = z[:, Z_KA:Z_KA + W_A].reshape(batch, seq, H_A, DH_A)
        v_rows = z[:, Z_VA:Z_VA + W_A].reshape(batch, seq, H_A, DH_A)
    else:
        oa = _attn_prompt(z, _prompt_bias(rel_bias), batch, seq)
        ob, s_new = _gla_prompt(z, alr, wup, balpha, gnorm, batch, seq)
        keep = min(BAND_LEFT, seq)
        zk = z.reshape(batch, seq, Z_WIDTH)[:, seq - keep:]
        k_rows = zk[:, :, Z_KA:Z_KA + W_A].reshape(batch, keep, H_A, DH_A)
        v_rows = zk[:, :, Z_VA:Z_VA + W_A].reshape(batch, keep, H_A, DH_A)

    x1, h2t = _merge(oa, ob, z, xf, gt1, sc2, sh2, g2, wa, wb, wo, tm_mg, is_sample, seq)
    rank1, b1, cnt0, a0 = _route(h2t, wqk_t)
    y = _peer(h2t, u_bf, vt_bf, rank1, b1, cnt0, a0, x1, gt2, gfin, is_sample, seq)
    return y.reshape(batch, seq, D_MODEL), k_rows, v_rows, s_new


def kernel(x_prompt, x_sample, cache_a_k, cache_a_v, state_gla, c_prompt, c_sample, w_ada, b_ada, g_norm1, w_in, rel_bias, w_alpha_up, b_alpha, g_gla_norm, w_branch_a, w_branch_b, w_out, g_norm2, w_peer_q, peer_subkeys, peer_u, peer_v, g_final):
    depth = w_in.shape[0]
    assert depth == 1, "the final rmsnorm is fused into the PEER kernel of the only layer"
    nbp = c_prompt.shape[0]
    l = 0
    nbs = c_sample.shape[0]
    pad = -(nbp + nbs) % 16
    c_all = jnp.concatenate([c_prompt, c_sample, jnp.zeros((pad, D_MODEL), F32)], axis=0)
    def only(w):
        return w.reshape(w.shape[1:])

    mod = _ada(c_all, only(w_ada), b_ada[l])

    alr_lo = Z_GL
    w_in0 = only(w_in)
    w_lo = w_in0[:, :alr_lo].astype(BF16)
    w_hi = w_in0[:, alr_lo + ALPHA_RANK:].astype(BF16)
    w_alr = jnp.pad(w_in0[:, alr_lo:alr_lo + ALPHA_RANK], ((0, 0), (0, ALR_PAD - ALPHA_RANK))).astype(BF16)
    wup = jnp.pad(w_alpha_up[l], ((0, ALR_PAD - ALPHA_RANK), (0, 0))).astype(BF16)
    wqk = _qkfold(only(w_peer_q).astype(BF16), peer_subkeys[l].astype(BF16))
    wqk_t = wqk.reshape(D_MODEL, PEER_HEADS, 2, N_KEYS).transpose(2, 3, 1, 0).reshape(2 * PEER_HEADS * N_KEYS, D_MODEL)
    wts = (
        g_norm1[l].reshape(1, D_MODEL), w_lo, w_hi, w_alr, rel_bias[l], wup,
        b_alpha[l].reshape(1, KW_B), g_gla_norm[l].reshape(1, DV_B),
        only(w_branch_a).astype(BF16), only(w_branch_b).astype(BF16), only(w_out).astype(BF16),
        g_norm2[l].reshape(1, D_MODEL), wqk_t.astype(BF16),
        only(peer_u).astype(BF16),
        jnp.transpose(peer_v.astype(BF16).reshape(N_EXPERTS // PEER_E, PEER_E, D_MODEL), (0, 2, 1)),
        g_final.reshape(1, D_MODEL),
    )
    yp, kp, vp, sp = _layer(x_prompt, mod[:nbp], None, None, None, False, wts)
    ys, ks, vs, ss = _layer(x_sample, mod[nbp:nbp + nbs], only(cache_a_k), only(cache_a_v), only(state_gla), True, wts)
    return (yp, ys, kp[None], vp[None], sp[None], ks[None], vs[None], ss[None])
```

```python
import functools
import math

import jax
import jax.numpy as jnp
from jax import lax
from jax.experimental import pallas as pl
from jax.experimental.pallas import tpu as pltpu

F32 = jnp.float32
BF16 = jnp.bfloat16

D_MODEL = 2048
CHUNK = 64
N_LEFT_CHUNKS = 8
BAND_LEFT = N_LEFT_CHUNKS * CHUNK
H_A = 8
DH_A = 128
W_A = H_A * DH_A
REL_CLIP = 128
H_B = 4
DK_B = 128
DV_B = 256
KW_B = H_B * DK_B
VW_B = H_B * DV_B
ALPHA_RANK = 16
GATE_TAU = 16.0
PEER_HEADS = 8
N_KEYS = 128
N_EXPERTS = N_KEYS * N_KEYS
PEER_TOPK = 16
EPS = 1e-6
NEG_INF = -1e30

Z_QA, Z_KA, Z_VA = 0, W_A, 2 * W_A
Z_QB = 3 * W_A
Z_KB = Z_QB + KW_B
Z_VB = Z_KB + KW_B
Z_RB = Z_VB + VW_B
Z_GL = Z_RB + VW_B
Z_WIDTH = Z_GL + 2 * D_MODEL
ALR_PAD = 128

VMEM_LIMIT_BIG = 56 * 1024 * 1024
VMEM_LIMIT_MID = 40 * 1024 * 1024

Q_TILE = 512
Q_SUB = 128
K_WIN = Q_SUB + BAND_LEFT


def _cparams(sem, vmem=None):
    return pltpu.CompilerParams(dimension_semantics=sem, vmem_limit_bytes=vmem)


def _rms(xf):
    return xf * lax.rsqrt(jnp.mean(xf * xf, axis=-1, keepdims=True) + EPS)


def _sigmoid(x):
    return 1.0 / (1.0 + jnp.exp(-x))


def _ada_kernel(c_ref, w_ref, b_ref, o_ref):
    c = c_ref[...]
    a = (c * _sigmoid(c)).astype(BF16)
    o_ref[...] = jnp.dot(a, w_ref[...].astype(BF16), preferred_element_type=F32) + b_ref[...]


def _ada(c_all, w_ada, b_ada):
    nb = c_all.shape[0]
    n_out = w_ada.shape[1]
    tn = 1024
    return pl.pallas_call(
        _ada_kernel,
        grid=(n_out // tn,),
        in_specs=[
            pl.BlockSpec((nb, D_MODEL), lambda j: (0, 0)),
            pl.BlockSpec((D_MODEL, tn), lambda j: (0, j)),
            pl.BlockSpec((1, tn), lambda j: (0, j)),
        ],
        out_specs=pl.BlockSpec((nb, tn), lambda j: (0, j)),
        out_shape=jax.ShapeDtypeStruct((nb, n_out), F32),
        compiler_params=_cparams(("arbitrary",), VMEM_LIMIT_MID),
        name="ada",
    )(c_all, w_ada, b_ada.reshape(1, n_out))


def _mod_spec(per_token, tm, tiles_per_batch, grid_rank):
    if per_token:
        if grid_rank == 1:
            return pl.BlockSpec((tm, D_MODEL), lambda m: (m, 0))
        return pl.BlockSpec((tm, D_MODEL), lambda m, n: (m, 0))
    if grid_rank == 1:
        return pl.BlockSpec((None, 1, D_MODEL), lambda m: (m // tiles_per_batch, 0, 0))
    return pl.BlockSpec((None, 1, D_MODEL), lambda m, n: (m // tiles_per_batch, 0, 0))


def _inproj_kernel(x_ref, sh_ref, sc_ref, g_ref, w_ref, walr_ref, z_ref, alr_ref, h_scr):
    @pl.when(pl.program_id(1) == 0)
    def _():
        h = _rms(x_ref[...]) * g_ref[...]
        h = h * (1.0 + sc_ref[...]) + sh_ref[...]
        hb = h.astype(BF16)
        h_scr[...] = hb
        alr_ref[...] = jnp.dot(hb, walr_ref[...], preferred_element_type=F32)

    z_ref[...] = jnp.dot(h_scr[...], w_ref[...], preferred_element_type=F32)


def _inproj(x, sh, sc, g, w_main, w_alr, tm, per_token, rows_per_batch):
    n = x.shape[0]
    tn = 1024
    tpb = max(rows_per_batch // tm, 1)
    return pl.pallas_call(
        _inproj_kernel,
        grid=(n // tm, Z_WIDTH // tn),
        in_specs=[
            pl.BlockSpec((tm, D_MODEL), lambda m, j: (m, 0)),
            _mod_spec(per_token, tm, tpb, 2),
            _mod_spec(per_token, tm, tpb, 2),
            pl.BlockSpec((1, D_MODEL), lambda m, j: (0, 0)),
            pl.BlockSpec((D_MODEL, tn), lambda m, j: (0, j)),
            pl.BlockSpec((D_MODEL, ALR_PAD), lambda m, j: (0, 0)),
        ],
        out_specs=[
            pl.BlockSpec((tm, tn), lambda m, j: (m, j)),
            pl.BlockSpec((tm, ALR_PAD), lambda m, j: (m, 0)),
        ],
        out_shape=[
            jax.ShapeDtypeStruct((n, Z_WIDTH), F32),
            jax.ShapeDtypeStruct((n, ALR_PAD), F32),
        ],
        scratch_shapes=[pltpu.VMEM((tm, D_MODEL), BF16)],
        compiler_params=_cparams(("parallel", "arbitrary"), VMEM_LIMIT_BIG),
        name="inproj",
    )(x, sh, sc, g, w_main, w_alr)


def _attn_prompt_kernel(q_ref, kp_ref, kc_ref, vp_ref, vc_ref, bias_ref, o_ref, kw, vw):
    first = pl.program_id(2) == 0
    kw[0:Q_TILE, :] = kp_ref[...].astype(BF16)
    kw[Q_TILE:2 * Q_TILE, :] = kc_ref[...].astype(BF16)
    vw[0:Q_TILE, :] = vp_ref[...].astype(BF16)
    vw[Q_TILE:2 * Q_TILE, :] = vc_ref[...].astype(BF16)
    scale = DH_A ** -0.5
    probs_ = [(hh, r) for hh in range(ATT_HEADS) for r in range(Q_TILE // Q_SUB)]
    hcol = [slice(hh * DH_A, (hh + 1) * DH_A) for hh in range(ATT_HEADS)]
    scores = [lax.dot_general((q_ref[r * Q_SUB:(r + 1) * Q_SUB, hcol[hh]] * scale).astype(BF16),
                              kw[r * Q_SUB:r * Q_SUB + K_WIN, hcol[hh]],
                              (((1,), (1,)), ((), ())), preferred_element_type=F32) for (hh, r) in probs_]
    probs, sums = [], []
    col = lax.broadcasted_iota(jnp.int32, (1, K_WIN), 1)
    for idx, (hh, r) in enumerate(probs_):
        before = jnp.where(jnp.logical_and(first, col + r * Q_SUB < Q_TILE), NEG_INF, 0.0)
        s = scores[idx] + bias_ref[hh] + before
        m = jnp.max(s, axis=-1, keepdims=True)
        p = jnp.exp(s - m)
        sums.append(jnp.sum(p, axis=-1, keepdims=True))
        probs.append(p.astype(BF16))
    outs = [jnp.dot(probs[idx], vw[r * Q_SUB:r * Q_SUB + K_WIN, hcol[hh]], preferred_element_type=F32)
            for idx, (hh, r) in enumerate(probs_)]
    for idx, (hh, r) in enumerate(probs_):
        o_ref[r * Q_SUB:(r + 1) * Q_SUB, hcol[hh]] = (outs[idx] / sums[idx]).astype(o_ref.dtype)


ATT_HEADS = 4


def _attn_prompt(z, bias, batch, seq):
    n = batch * seq
    tiles = seq // Q_TILE
    width = ATT_HEADS * DH_A

    def cur(col):
        return pl.BlockSpec((Q_TILE, width), lambda h, b, i: (b * tiles + i, col + h))

    def prev(col):
        return pl.BlockSpec((Q_TILE, width),
                            lambda h, b, i: (b * tiles + jnp.maximum(i - 1, 0), col + h))

    kcol, vcol = Z_KA // width, Z_VA // width
    return pl.pallas_call(
        _attn_prompt_kernel,
        grid=(H_A // ATT_HEADS, batch, tiles),
        in_specs=[cur(0), prev(kcol), cur(kcol), prev(vcol), cur(vcol),
                  pl.BlockSpec((ATT_HEADS, Q_SUB, K_WIN), lambda h, b, i: (h, 0, 0))],
        out_specs=pl.BlockSpec((Q_TILE, width), lambda h, b, i: (b * tiles + i, h)),
        out_shape=jax.ShapeDtypeStruct((n, W_A), BF16),
        scratch_shapes=[pltpu.VMEM((2 * Q_TILE, width), BF16), pltpu.VMEM((2 * Q_TILE, width), BF16)],
        compiler_params=_cparams(("arbitrary", "arbitrary", "arbitrary")),
        name="attn_prompt",
    )(z, z, z, z, z, bias)


def _attn_sample_kernel(q_ref, kn_ref, vn_ref, ck_ref, cv_ref, bc_ref, bn_ref, o_ref):
    scale = DH_A ** -0.5
    nt = (((1,), (1,)), ((), ()))
    lc = ck_ref.shape[0] // H_A
    heads = range(H_A)
    cols = [slice(h * DH_A, (h + 1) * DH_A) for h in heads]
    crow = [pl.ds(h, lc, stride=H_A) for h in heads]
    qs = [q_ref[:, cols[h]].astype(BF16) for h in heads]
    s_c = [lax.dot_general(qs[h], ck_ref[crow[h], :].astype(BF16), nt, preferred_element_type=F32) for h in heads]
    s_n = [lax.dot_general(qs[h], kn_ref[:, cols[h]].astype(BF16), nt, preferred_element_type=F32) for h in heads]
    p_c, p_n, sums = [], [], []
    for h in heads:
        sc = s_c[h] * scale + bc_ref[h]
        sn = s_n[h] * scale + bn_ref[h]
        m = jnp.maximum(jnp.max(sc, axis=-1, keepdims=True), jnp.max(sn, axis=-1, keepdims=True))
        pc = jnp.exp(sc - m)
        pn = jnp.exp(sn - m)
        sums.append(jnp.sum(pc, axis=-1, keepdims=True) + jnp.sum(pn, axis=-1, keepdims=True))
        p_c.append(pc.astype(BF16))
        p_n.append(pn.astype(BF16))
    outs = [jnp.dot(p_c[h], cv_ref[crow[h], :].astype(BF16), preferred_element_type=F32)
            + jnp.dot(p_n[h], vn_ref[:, cols[h]].astype(BF16), preferred_element_type=F32) for h in heads]
    for h in heads:
        o_ref[:, cols[h]] = (outs[h] / sums[h]).astype(o_ref.dtype)


def _attn_sample(z, cache_k, cache_v, bias_c, bias_n, batch, seq):
    lc = cache_k.shape[1] // H_A
    return pl.pallas_call(
        _attn_sample_kernel,
        grid=(batch,),
        in_specs=[
            pl.BlockSpec((seq, W_A), lambda b: (b, Z_QA // W_A)),
            pl.BlockSpec((seq, W_A), lambda b: (b, Z_KA // W_A)),
            pl.BlockSpec((seq, W_A), lambda b: (b, Z_VA // W_A)),
            pl.BlockSpec((None, lc * H_A, DH_A), lambda b: (b, 0, 0)),
            pl.BlockSpec((None, lc * H_A, DH_A), lambda b: (b, 0, 0)),
            pl.BlockSpec((H_A, seq, lc), lambda b: (0, 0, 0)),
            pl.BlockSpec((H_A, seq, seq), lambda b: (0, 0, 0)),
        ],
        out_specs=pl.BlockSpec((seq, W_A), lambda b: (b, 0)),
        out_shape=jax.ShapeDtypeStruct((batch * seq, W_A), BF16),
        compiler_params=_cparams(("arbitrary",), VMEM_LIMIT_MID),
        name="attn_sample",
    )(z, z, z, cache_k, cache_v, bias_c, bias_n)


def _split_bf16(x):
    hi = x.astype(BF16)
    lo = (x - hi.astype(F32)).astype(BF16)
    return hi, lo


_TN = (((0,), (0,)), ((), ()))
_NT = (((1,), (1,)), ((), ()))


def _gla_state_free(units, tri, causal, c):
    scale = DK_B ** -0.5
    ones = jnp.ones((c, DK_B), BF16)
    split = [_split_bf16(g) for (_, _, _, g) in units]
    cum = [jnp.dot(tri, hi, preferred_element_type=F32) + jnp.dot(tri, lo, preferred_element_type=F32)
           for (hi, lo) in split]
    tot = [lax.dot_general(hi, ones, _TN, preferred_element_type=F32)
           + lax.dot_general(lo, ones, _TN, preferred_element_type=F32) for (hi, lo) in split]
    qd, kd, vb, att = [], [], [], []
    for (q, k, v, _), b in zip(units, cum):
        mid = b[c // 2:c // 2 + 1, :]
        blast = b[c - 1:c, :]
        qs = q * scale
        ks = k * scale
        att.append(lax.dot_general((qs * jnp.exp(b - mid)).astype(BF16), (ks * jnp.exp(mid - b)).astype(BF16),
                                   _NT, preferred_element_type=F32))
        qd.append((qs * jnp.exp(b)).astype(BF16))
        kd.append((ks * jnp.exp(blast - b)).astype(BF16))
        vb.append(v.astype(BF16))
    intra = [jnp.dot(jnp.where(causal, a, 0.0).astype(BF16), v, preferred_element_type=F32)
             for a, v in zip(att, vb)]
    out = []
    for i in range(len(units)):
        dec = jnp.exp(tot[i])
        out.append((intra[i], qd[i], kd[i], vb[i], jnp.concatenate([dec, dec], axis=1)))
    return out


def _gla_read_state(pre, s_prev):
    o_intra, qd, _, _, _ = pre
    return o_intra + jnp.dot(qd, s_prev.astype(BF16), preferred_element_type=F32)


def _gla_next_state(pre, s_prev):
    _, _, kd, vb, dec = pre
    return dec * s_prev + lax.dot_general(kd, vb, _TN, preferred_element_type=F32)


def _log_decay(alr, wup, balpha):
    x = jnp.dot(alr.astype(BF16), wup, preferred_element_type=F32) + balpha
    return (jnp.minimum(x, 0.0) - jnp.log1p(jnp.exp(-jnp.abs(x)))) * (1.0 / GATE_TAU)


def _gla_epilogue(o, rb, gnorm):
    y = _rms(o) * gnorm
    return y * (rb * _sigmoid(rb))


def _tri_and_causal(c):
    row = lax.broadcasted_iota(jnp.int32, (c, c), 0)
    col = lax.broadcasted_iota(jnp.int32, (c, c), 1)
    causal = col <= row
    return jnp.where(causal, 1.0, 0.0).astype(BF16), causal


def _gla_prompt_kernel(q_ref, k_ref, v_ref, rb_ref, alr_ref, wup_ref, balpha_ref, gnorm_ref,
                       ob_ref, st_ref, s_scr, *, batch, blocks):
    step = pl.program_id(0)

    @pl.when(step == 0)
    def _():
        s_scr[...] = jnp.zeros_like(s_scr)

    tri, causal = _tri_and_causal(CHUNK)
    wup = wup_ref[...]
    balpha = balpha_ref[...]
    gnorm = gnorm_ref[...]
    ksl = [slice(h * DK_B, (h + 1) * DK_B) for h in range(H_B)]
    vsl = [slice(h * DV_B, (h + 1) * DV_B) for h in range(H_B)]
    streams = [(bi, h) for bi in range(batch) for h in range(H_B)]
    rows = [slice(blk * CHUNK, (blk + 1) * CHUNK) for blk in range(blocks)]
    logg = {(blk, bi): _log_decay(alr_ref[bi, rows[blk], :], wup, balpha)
            for blk in range(blocks) for bi in range(batch)}
    units = [(q_ref[bi, rows[blk], ksl[h]], k_ref[bi, rows[blk], ksl[h]], v_ref[bi, rows[blk], vsl[h]],
              logg[(blk, bi)][:, ksl[h]]) for blk in range(blocks) for (bi, h) in streams]
    pre = _gla_state_free(units, tri, causal, CHUNK)
    for blk in range(blocks):
        mine = pre[blk * len(streams):(blk + 1) * len(streams)]
        outs = [_gla_read_state(p, s_scr[si]) for si, p in enumerate(mine)]
        nxt = [_gla_next_state(p, s_scr[si]) for si, p in enumerate(mine)]
        for si, (bi, h) in enumerate(streams):
            s_scr[si] = nxt[si]
            ob_ref[bi, rows[blk], vsl[h]] = _gla_epilogue(outs[si], rb_ref[bi, rows[blk], vsl[h]],
                                                          gnorm).astype(ob_ref.dtype)

    @pl.when(step == pl.num_programs(0) - 1)
    def _():
        st_ref[...] = s_scr[...]


def _gla_prompt(z, alr, wup, balpha, gnorm, batch, seq):
    blocks = 4
    rows = blocks * CHUNK
    z3 = z.reshape(batch, seq, Z_WIDTH)
    alr3 = alr.reshape(batch, seq, ALR_PAD)
    ob, st = pl.pallas_call(
        functools.partial(_gla_prompt_kernel, batch=batch, blocks=blocks),
        grid=(seq // rows,),
        in_specs=[
            pl.BlockSpec((batch, rows, KW_B), lambda j: (0, j, Z_QB // KW_B)),
            pl.BlockSpec((batch, rows, KW_B), lambda j: (0, j, Z_KB // KW_B)),
            pl.BlockSpec((batch, rows, VW_B), lambda j: (0, j, Z_VB // VW_B)),
            pl.BlockSpec((batch, rows, VW_B), lambda j: (0, j, Z_RB // VW_B)),
            pl.BlockSpec((batch, rows, ALR_PAD), lambda j: (0, j, 0)),
            pl.BlockSpec((ALR_PAD, KW_B), lambda j: (0, 0)),
            pl.BlockSpec((1, KW_B), lambda j: (0, 0)),
            pl.BlockSpec((1, DV_B), lambda j: (0, 0)),
        ],
        out_specs=[
            pl.BlockSpec((batch, rows, VW_B), lambda j: (0, j, 0)),
            pl.BlockSpec((batch * H_B, DK_B, DV_B), lambda j: (0, 0, 0)),
        ],
        out_shape=[
            jax.ShapeDtypeStruct((batch, seq, VW_B), BF16),
            jax.ShapeDtypeStruct((batch * H_B, DK_B, DV_B), F32),
        ],
        scratch_shapes=[pltpu.VMEM((batch * H_B, DK_B, DV_B), F32)],
        compiler_params=_cparams(("arbitrary",), VMEM_LIMIT_MID),
        name="gla_prompt",
    )(z3, z3, z3, z3, alr3, wup, balpha, gnorm)
    return ob.reshape(batch * seq, VW_B), st.reshape(batch, H_B, DK_B, DV_B)


def _gla_sample_kernel(q_ref, k_ref, v_ref, rb_ref, alr_ref, s0_ref, wup_ref, balpha_ref, gnorm_ref,
                       ob_ref, st_ref, *, seq):
    tri, causal = _tri_and_causal(seq)
    logg = _log_decay(alr_ref[...], wup_ref[...], balpha_ref[...])
    gnorm = gnorm_ref[...]
    ksl = [slice(h * DK_B, (h + 1) * DK_B) for h in range(H_B)]
    vsl = [slice(h * DV_B, (h + 1) * DV_B) for h in range(H_B)]
    pre = _gla_state_free([(q_ref[:, ksl[h]], k_ref[:, ksl[h]], v_ref[:, vsl[h]], logg[:, ksl[h]])
                           for h in range(H_B)], tri, causal, seq)
    outs = [_gla_read_state(pre[h], s0_ref[h]) for h in range(H_B)]
    nxt = [_gla_next_state(pre[h], s0_ref[h]) for h in range(H_B)]
    for h in range(H_B):
        st_ref[h] = nxt[h]
        ob_ref[:, vsl[h]] = _gla_epilogue(outs[h], rb_ref[:, vsl[h]], gnorm).astype(ob_ref.dtype)


def _gla_sample(z, alr, state, wup, balpha, gnorm, batch, seq):
    return pl.pallas_call(
        functools.partial(_gla_sample_kernel, seq=seq),
        grid=(batch,),
        in_specs=[
            pl.BlockSpec((seq, KW_B), lambda b: (b, Z_QB // KW_B)),
            pl.BlockSpec((seq, KW_B), lambda b: (b, Z_KB // KW_B)),
            pl.BlockSpec((seq, VW_B), lambda b: (b, Z_VB // VW_B)),
            pl.BlockSpec((seq, VW_B), lambda b: (b, Z_RB // VW_B)),
            pl.BlockSpec((seq, ALR_PAD), lambda b: (b, 0)),
            pl.BlockSpec((None, H_B, DK_B, DV_B), lambda b: (b, 0, 0, 0)),
            pl.BlockSpec((ALR_PAD, KW_B), lambda b: (0, 0)),
            pl.BlockSpec((1, KW_B), lambda b: (0, 0)),
            pl.BlockSpec((1, DV_B), lambda b: (0, 0)),
        ],
        out_specs=[
            pl.BlockSpec((seq, VW_B), lambda b: (b, 0)),
            pl.BlockSpec((None, H_B, DK_B, DV_B), lambda b: (b, 0, 0, 0)),
        ],
        out_shape=[
            jax.ShapeDtypeStruct((batch * seq, VW_B), BF16),
            jax.ShapeDtypeStruct((batch, H_B, DK_B, DV_B), F32),
        ],
        compiler_params=_cparams(("arbitrary",)),
        name="gla_sample",
    )(z, z, z, z, alr, state, wup, balpha, gnorm)


def _merge_kernel(oa_ref, ob_ref, gla_ref, glb_ref, x_ref, gt1_ref, sc2_ref, sh2_ref, g2_ref,
                  wa_ref, wb_ref, wo_ref, x1_ref, h2t_ref):
    ya = jnp.dot(oa_ref[...], wa_ref[...], preferred_element_type=F32)
    yb = jnp.dot(ob_ref[...], wb_ref[...], preferred_element_type=F32)
    merged = _sigmoid(gla_ref[...]) * ya + _sigmoid(glb_ref[...]) * yb
    x1 = x_ref[...] + gt1_ref[...] * jnp.dot(merged.astype(BF16), wo_ref[...], preferred_element_type=F32)
    x1_ref[...] = x1
    h2 = _rms(x1) * g2_ref[...]
    h2 = h2 * (1.0 + sc2_ref[...]) + sh2_ref[...]
    h2t_ref[...] = jnp.transpose(h2).astype(BF16)


def _merge(oa, ob, z, x, gt1, sc2, sh2, g2, wa, wb, wo, tm, per_token, rows_per_batch):
    n = x.shape[0]
    tpb = max(rows_per_batch // tm, 1)
    const = lambda shape: pl.BlockSpec(shape, lambda m: (0, 0))
    mod = _mod_spec(per_token, tm, tpb, 1)
    return pl.pallas_call(
        _merge_kernel,
        grid=(n // tm,),
        in_specs=[
            pl.BlockSpec((tm, W_A), lambda m: (m, 0)),
            pl.BlockSpec((tm, VW_B), lambda m: (m, 0)),
            pl.BlockSpec((tm, D_MODEL), lambda m: (m, Z_GL // D_MODEL)),
            pl.BlockSpec((tm, D_MODEL), lambda m: (m, Z_GL // D_MODEL + 1)),
            pl.BlockSpec((tm, D_MODEL), lambda m: (m, 0)),
            mod, mod, mod,
            const((1, D_MODEL)),
            const((W_A, D_MODEL)), const((VW_B, D_MODEL)), const((D_MODEL, D_MODEL)),
        ],
        out_specs=[
            pl.BlockSpec((tm, D_MODEL), lambda m: (m, 0)),
            pl.BlockSpec((D_MODEL, tm), lambda m: (0, m)),
        ],
        out_shape=[
            jax.ShapeDtypeStruct((n, D_MODEL), F32),
            jax.ShapeDtypeStruct((D_MODEL, n), BF16),
        ],
        compiler_params=_cparams(("arbitrary",), VMEM_LIMIT_BIG),
        name="merge",
    )(oa, ob, z, z, x, gt1, sc2, sh2, g2, wa, wb, wo)


def _qkfold_kernel(wq_ref, sk_ref, o_ref):
    o_ref[...] = lax.dot_general(wq_ref[...], sk_ref[...], (((1,), (1,)), ((), ())),
                                 preferred_element_type=F32)


def _qkfold(w_q, subkeys):
    half = N_KEYS
    return pl.pallas_call(
        _qkfold_kernel,
        grid=(2 * PEER_HEADS,),
        in_specs=[
            pl.BlockSpec((D_MODEL, half), lambda c: (0, c)),
            pl.BlockSpec((None, N_KEYS, half), lambda c: (c % 2, 0, 0)),
        ],
        out_specs=pl.BlockSpec((D_MODEL, N_KEYS), lambda c: (0, c)),
        out_shape=jax.ShapeDtypeStruct((D_MODEL, 2 * PEER_HEADS * N_KEYS), F32),
        compiler_params=_cparams(("arbitrary",)),
        name="qkfold",
    )(w_q, subkeys)


ROUTE_T = 256
LANES = 128
_SET_ROWS = N_KEYS * PEER_HEADS
_PAIRS = [(r, q) for r in range(PEER_TOPK) for q in range(PEER_TOPK) if (r + 1) * (q + 1) <= PEER_TOPK]


def _sort_desc(x):
    x = list(x)
    n = len(x)
    k = 2
    while k <= n:
        j = k // 2
        while j >= 1:
            for i in range(n):
                l = i ^ j
                if l > i:
                    hi, lo = jnp.maximum(x[i], x[l]), jnp.minimum(x[i], x[l])
                    x[i], x[l] = (hi, lo) if (i & k) == 0 else (lo, hi)
            j //= 2
        k *= 2
    return x


def _merge_top(a, b):
    n = len(a)
    x = [jnp.maximum(a[i], b[n - 1 - i]) for i in range(n)]
    j = n // 2
    while j >= 1:
        for i in range(n):
            l = i ^ j
            if l > i:
                x[i], x[l] = jnp.maximum(x[i], x[l]), jnp.minimum(x[i], x[l])
        j //= 2
    return x


def _top_sorted(vals, top):
    groups = [_sort_desc(vals[g:g + top]) for g in range(0, len(vals), top)]
    while len(groups) > 1:
        groups = [_merge_top(groups[g], groups[g + 1]) for g in range(0, len(groups), 2)]
    return groups[0]


def _route_fast(s0, s1):
    shp = s0[0].shape
    one = jnp.ones(shp, F32)
    zero = jnp.zeros(shp, F32)
    top0 = _top_sorted(s0, PEER_TOPK)
    top1 = _top_sorted(s1, PEER_TOPK)
    cand = [top0[r] + top1[q] for (r, q) in _PAIRS]
    pad = [jnp.full(shp, -jnp.inf, F32)] * (-len(cand) % PEER_TOPK)
    topc = _top_sorted(cand + pad, PEER_TOPK)
    tau = topc[PEER_TOPK - 1]
    cnt = [zero] * PEER_TOPK
    for c, (r, q) in enumerate(_PAIRS):
        cnt[r] = cnt[r] + jnp.where(cand[c] >= tau, one, zero)
    picked = cnt[0]
    for r in range(1, PEER_TOPK):
        picked = picked + cnt[r]
    zsum = zero
    for r in range(PEER_TOPK):
        zsum = zsum + jnp.exp(topc[r] - topc[0])
    zinv = 1.0 / zsum
    tie = jnp.where(picked != float(PEER_TOPK), one, zero)
    for t in (top0, top1):
        for r in range(PEER_TOPK - 1):
            tie = jnp.where(t[r] == t[r + 1], one, tie)
    in0, in1 = zero, zero
    rank1, b1, cnt0, a0 = [], [], [], []
    for k in range(N_KEYS):
        v0, v1 = s0[k], s1[k]
        c0 = zero
        rk = jnp.full(shp, float(PEER_TOPK), F32)
        for r in range(PEER_TOPK - 1, -1, -1):
            c0 = jnp.where(v0 >= top0[r], cnt[r], c0)
            rk = jnp.where(v1 >= top1[r], float(r), rk)
        in0 = in0 + jnp.where(v0 >= top0[PEER_TOPK - 1], one, zero)
        in1 = in1 + jnp.where(v1 >= top1[PEER_TOPK - 1], one, zero)
        cnt0.append(c0)
        rank1.append(rk)
        a0.append(jnp.exp(v0 - top0[0]) * zinv)
        b1.append(jnp.exp(v1 - top1[0]))
    tie = jnp.where(in0 != float(PEER_TOPK), one, tie)
    tie = jnp.where(in1 != float(PEER_TOPK), one, tie)
    return rank1, b1, cnt0, a0, tie


def _route_kernel(h2t_ref, wqk_ref, rank1_ref, b1_ref, cnt0_ref, a0_ref,
                  s_scr, sw_scr, rk_scr, val_scr, idx_scr, tmp_scr):
    halves = h2t_ref.shape[1] // LANES
    shp = (PEER_HEADS, LANES)
    s = jnp.dot(wqk_ref[...], h2t_ref[...], preferred_element_type=F32)
    for hf in range(halves):
        s_scr[hf] = s[:, hf * LANES:(hf + 1) * LANES]
    neg = jnp.full(shp, -jnp.inf, F32)

    def rows(p, k):
        return pl.ds(p * _SET_ROWS + k * PEER_HEADS, PEER_HEADS)

    def one_half(hf, carry0):
        f_rank1, f_b1, f_cnt0, f_a0, tie = _route_fast([s_scr[hf, rows(0, k), :] for k in range(N_KEYS)],
                                                       [s_scr[hf, rows(1, k), :] for k in range(N_KEYS)])
        for k in range(N_KEYS):
            kr = pl.ds(k * PEER_HEADS, PEER_HEADS)
            tmp_scr[hf, 0, kr, :] = f_cnt0[k]
            tmp_scr[hf, 1, kr, :] = f_a0[k]
            tmp_scr[hf, 2, kr, :] = f_b1[k]
            rk_scr[hf, rows(1, k), :] = f_rank1[k]

        @pl.when(jnp.max(tie) > 0.0)
        def _():
            exact_half(hf)

        return carry0

    def exact_half(hf):
        sw_scr[hf] = s_scr[hf]
        rk_scr[hf] = jnp.full(rk_scr.shape[1:], float(PEER_TOPK), F32)

        def extract(r, carry):
            rf = jnp.full(shp, r, jnp.int32).astype(F32)
            for p in range(2):
                best, bidx = None, None
                for part in range(4):
                    m = neg
                    ix = jnp.zeros(shp, jnp.int32)
                    for k in range(part * 32, part * 32 + 32):
                        v = sw_scr[hf, rows(p, k), :]
                        gt = v > m
                        m = jnp.where(gt, v, m)
                        ix = jnp.where(gt, k, ix)
                    if best is None:
                        best, bidx = m, ix
                    else:
                        gt = m > best
                        best = jnp.where(gt, m, best)
                        bidx = jnp.where(gt, ix, bidx)
                val_scr[p, r] = best
                idx_scr[p, r] = bidx
                for k in range(N_KEYS):
                    hit = bidx == k
                    sw_scr[hf, rows(p, k), :] = jnp.where(hit, -jnp.inf, sw_scr[hf, rows(p, k), :])
                    rk_scr[hf, rows(p, k), :] = jnp.where(hit, rf, rk_scr[hf, rows(p, k), :])
            return carry

        lax.fori_loop(0, PEER_TOPK, extract, 0)

        v0 = [val_scr[0, r] for r in range(PEER_TOPK)]
        v1 = [val_scr[1, q] for q in range(PEER_TOPK)]
        cand = [v0[r] + v1[q] for (r, q) in _PAIRS]
        top = v0[0] + v1[0]
        cnt = [jnp.zeros(shp, F32) for _ in range(PEER_TOPK)]
        zsum = jnp.zeros(shp, F32)
        for _ in range(PEER_TOPK):
            m = neg
            ix = jnp.zeros(shp, jnp.int32)
            for c, cv in enumerate(cand):
                gt = cv > m
                m = jnp.where(gt, cv, m)
                ix = jnp.where(gt, c, ix)
            zsum = zsum + jnp.exp(m - top)
            for c, (r, q) in enumerate(_PAIRS):
                hit = ix == c
                cand[c] = jnp.where(hit, -jnp.inf, cand[c])
                cnt[r] = cnt[r] + jnp.where(hit, 1.0, 0.0)
        zinv = 1.0 / zsum

        idx0 = [idx_scr[0, r] for r in range(PEER_TOPK)]
        for k in range(N_KEYS):
            c0 = jnp.zeros(shp, F32)
            for r in range(PEER_TOPK):
                c0 = jnp.where(idx0[r] == k, cnt[r], c0)
            kr = pl.ds(k * PEER_HEADS, PEER_HEADS)
            tmp_scr[hf, 0, kr, :] = c0
            tmp_scr[hf, 1, kr, :] = jnp.exp(s_scr[hf, rows(0, k), :] - v0[0]) * zinv
            tmp_scr[hf, 2, kr, :] = jnp.exp(s_scr[hf, rows(1, k), :] - v1[0])

    lax.fori_loop(0, halves, one_half, 0)

    for hf in range(halves):
        ls = slice(hf * LANES, (hf + 1) * LANES)
        for h in range(PEER_HEADS):
            dst = slice(h * N_KEYS, (h + 1) * N_KEYS)
            src = pl.ds(h, N_KEYS, stride=PEER_HEADS)
            cnt0_ref[dst, ls] = _pack_pair(tmp_scr[hf, 0, src, :])
            a0_ref[dst, ls] = _pack_pair(tmp_scr[hf, 1, src, :])
            b1_ref[dst, ls] = tmp_scr[hf, 2, src, :].astype(BF16)
            rank1_ref[dst, ls] = rk_scr[hf, pl.ds(_SET_ROWS + h, N_KEYS, stride=PEER_HEADS), :].astype(BF16)


def _pack_pair(x):
    bits = pltpu.bitcast(x.astype(BF16).astype(F32), jnp.uint32)
    return bits | (bits >> 16)


def _route(h2t, wqk_t):
    n = h2t.shape[1]
    t = ROUTE_T
    rows = PEER_HEADS * N_KEYS
    ospec = pl.BlockSpec((rows, t), lambda m: (0, m))
    out_bf = jax.ShapeDtypeStruct((rows, n), BF16)
    out_pk = jax.ShapeDtypeStruct((rows, n), jnp.uint32)
    return pl.pallas_call(
        _route_kernel,
        grid=(n // t,),
        in_specs=[
            pl.BlockSpec((D_MODEL, t), lambda m: (0, m)),
            pl.BlockSpec((2 * rows, D_MODEL), lambda m: (0, 0)),
        ],
        out_specs=[ospec, ospec, ospec, ospec],
        out_shape=[out_bf, out_bf, out_pk, out_pk],
        scratch_shapes=[
            pltpu.VMEM((t // LANES, 2 * rows, LANES), F32),
            pltpu.VMEM((t // LANES, 2 * rows, LANES), F32),
            pltpu.VMEM((t // LANES, 2 * rows, LANES), F32),
            pltpu.VMEM((2, PEER_TOPK, PEER_HEADS, LANES), F32),
            pltpu.VMEM((2, PEER_TOPK, PEER_HEADS, LANES), jnp.int32),
            pltpu.VMEM((t // LANES, 3, rows, LANES), F32),
        ],
        compiler_params=_cparams(("arbitrary",), VMEM_LIMIT_MID),
        name="route",
    )(h2t, wqk_t)


PEER_T = 512
PEER_E = 512


def _gelu(x):
    return 0.5 * x * (1.0 + lax.erf(x * (2.0 ** -0.5)))


def _bcast_pair_rows(ref, row, t):
    word = jnp.broadcast_to(ref[pl.ds(row, 1), :], (8, t))
    pair = pltpu.bitcast(word, BF16)
    return jnp.broadcast_to(pair[None], (N_KEYS // 16, 16, t)).reshape(N_KEYS, t)


def _peer_coef(tile, u_ref, row0, h2t_ref, rank1_ref, b1_ref, cnt0_ref, a0_ref, coef_ref):
    t = h2t_ref.shape[1]
    act = jnp.dot(u_ref[row0:row0 + PEER_E, :], h2t_ref[...], preferred_element_type=F32)
    per = PEER_E // N_KEYS
    for ii in range(per):
        i = tile * per + ii
        w = None
        for h in range(PEER_HEADS):
            hs = slice(h * N_KEYS, (h + 1) * N_KEYS)
            cnt = _bcast_pair_rows(cnt0_ref, h * N_KEYS + i, t)
            a = _bcast_pair_rows(a0_ref, h * N_KEYS + i, t)
            term = jnp.where(rank1_ref[hs, :] < cnt, b1_ref[hs, :] * a, jnp.zeros((), BF16))
            w = term if w is None else w + term
        rs = slice(ii * N_KEYS, (ii + 1) * N_KEYS)
        coef_ref[rs, :] = _gelu(act[rs, :]).astype(BF16) * w


def _peer_kernel(h2t_ref, u_ref, vtp_ref, vtc_ref, rank1_ref, b1_ref, cnt0_ref, a0_ref, x1_ref, gt2_ref, gf_ref,
                 y_ref, acc, coef_a, coef_b):
    k = pl.program_id(1)
    last = pl.num_programs(1) - 1
    route = (rank1_ref, b1_ref, cnt0_ref, a0_ref)

    def two_tiles():
        _peer_coef(2 * k, u_ref, 0, h2t_ref, *route, coef_a)
        _peer_coef(2 * k + 1, u_ref, PEER_E, h2t_ref, *route, coef_b)
        return jnp.dot(vtc_ref[...], coef_a[...], preferred_element_type=F32)

    @pl.when(k == 0)
    def _():
        acc[...] = two_tiles()

    @pl.when(jnp.logical_and(k > 0, k < last))
    def _():
        acc[...] += jnp.dot(vtp_ref[...], coef_b[...], preferred_element_type=F32)
        acc[...] += two_tiles()

    @pl.when(k == last)
    def _():
        tot = acc[...] + jnp.dot(vtp_ref[...], coef_b[...], preferred_element_type=F32)
        x2 = x1_ref[...] + gt2_ref[...] * jnp.transpose(tot)
        y_ref[...] = _rms(x2) * gf_ref[...]


def _peer(h2t, u_bf, vt_bf, rank1, b1, cnt0, a0, x1, gt2, gfin, per_token, rows_per_batch):
    n = x1.shape[0]
    t = PEER_T
    tpb = max(rows_per_batch // t, 1)
    rows = PEER_HEADS * N_KEYS
    steps = N_EXPERTS // (2 * PEER_E)
    once = pl.Buffered(1)
    tok = lambda r: pl.BlockSpec((r, t), lambda m, k: (0, m))
    if per_token:
        gt_spec = pl.BlockSpec((t, D_MODEL), lambda m, k: (m, 0), pipeline_mode=once)
    else:
        gt_spec = pl.BlockSpec((None, 1, D_MODEL), lambda m, k: (m // tpb, 0, 0))
    return pl.pallas_call(
        _peer_kernel,
        grid=(n // t, steps + 1),
        in_specs=[
            pl.BlockSpec((D_MODEL, t), lambda m, k: (0, m)),
            pl.BlockSpec((2 * PEER_E, D_MODEL), lambda m, k: (jnp.minimum(k, steps - 1), 0)),
            pl.BlockSpec((None, D_MODEL, PEER_E), lambda m, k: (jnp.maximum(2 * k - 1, 0), 0, 0)),
            pl.BlockSpec((None, D_MODEL, PEER_E), lambda m, k: (2 * jnp.minimum(k, steps - 1), 0, 0)),
            tok(rows), tok(rows), tok(rows), tok(rows),
            pl.BlockSpec((t, D_MODEL), lambda m, k: (m, 0), pipeline_mode=once),
            gt_spec,
            pl.BlockSpec((1, D_MODEL), lambda m, k: (0, 0)),
        ],
        out_specs=pl.BlockSpec((t, D_MODEL), lambda m, k: (m, 0)),
        out_shape=jax.ShapeDtypeStruct((n, D_MODEL), F32),
        scratch_shapes=[pltpu.VMEM((D_MODEL, t), F32), pltpu.VMEM((PEER_E, t), BF16), pltpu.VMEM((PEER_E, t), BF16)],
        compiler_params=_cparams(("parallel", "arbitrary"), VMEM_LIMIT_BIG),
        name="peer",
    )(h2t, u_bf, vt_bf, vt_bf, rank1, b1, cnt0, a0, x1, gt2, gfin)


def _rel_bias_tile(rel_bias, rows, cols, offset):
    rb = rel_bias.astype(F32)
    heads = rb.shape[0]
    rel_max = offset + rows - 1
    rel_min = offset - (cols - 1)
    lo, hi = max(rel_min, -REL_CLIP), min(rel_max, REL_CLIP)
    parts = []
    if rel_min < -REL_CLIP:
        parts.append(jnp.broadcast_to(rb[:, :1], (heads, -REL_CLIP - rel_min)))
    parts.append(rb[:, lo + REL_CLIP:hi + REL_CLIP + 1])
    if rel_max > REL_CLIP:
        parts.append(jnp.broadcast_to(rb[:, -1:], (heads, rel_max - REL_CLIP)))
    g = jnp.flip(jnp.concatenate(parts, axis=1), axis=1)
    period = rows + cols
    gp = jnp.pad(g, ((0, 0), (0, 1)))
    shifted = jnp.tile(gp, (1, rows))[:, :rows * (period - 1)].reshape(heads, rows, period - 1)
    return shifted[:, :, rows - 1:rows - 1 + cols]


def _prompt_bias(rel_bias):
    a = jnp.arange(Q_SUB, dtype=jnp.int32)[:, None]
    c = jnp.arange(K_WIN, dtype=jnp.int32)[None, :]
    cq = a // CHUNK
    ck = c // CHUNK
    ok = (ck >= cq) & (ck <= cq + N_LEFT_CHUNKS)
    return jnp.where(ok[None], _rel_bias_tile(rel_bias, Q_SUB, K_WIN, BAND_LEFT), NEG_INF)


def _sample_bias(rel_bias, seq, lc):
    return _rel_bias_tile(rel_bias, seq, lc, lc), _rel_bias_tile(rel_bias, seq, seq, 0)


def _layer(x, mod, cache_k, cache_v, state, is_sample, wts):
    (g1, w_main, w_alr, rel_bias, wup, balpha, gnorm, wa, wb, wo, g2, wqk_t, u_bf, vt_bf, gfin) = wts
    batch, seq, _ = x.shape
    n = batch * seq
    xf = x.reshape(n, D_MODEL)
    parts = [mod[:, i * D_MODEL:(i + 1) * D_MODEL] for i in range(6)]
    if is_sample:
        sh1, sc1, gt1, sh2, sc2, gt2 = [jnp.repeat(p, seq, axis=0) for p in parts]
        tm_in, tm_mg = n, 256
    else:
        sh1, sc1, gt1, sh2, sc2, gt2 = [p.reshape(batch, 1, D_MODEL) for p in parts]
        tm_in, tm_mg = 1024, 256

    z, alr = _inproj(xf, sh1, sc1, g1, w_main, w_alr, tm_in, is_sample, seq)

    if is_sample:
        lc = cache_k.shape[1]
        bias_c, bias_n = _sample_bias(rel_bias, seq, lc)
        oa = _attn_sample(z, cache_k.reshape(batch, lc * H_A, DH_A), cache_v.reshape(batch, lc * H_A, DH_A),
                          bias_c, bias_n, batch, seq)
        ob, s_new = _gla_sample(z, alr, state, wup, balpha, gnorm, batch, seq)
        k_rows = z[:, Z_KA:Z_KA + W_A].reshape(batch, seq, H_A, DH_A)
        v_rows = z[:, Z_VA:Z_VA + W_A].reshape(batch, seq, H_A, DH_A)
    else:
        oa = _attn_prompt(z, _prompt_bias(rel_bias), batch, seq)
        ob, s_new = _gla_prompt(z, alr, wup, balpha, gnorm, batch, seq)
        keep = min(BAND_LEFT, seq)
        zk = z.reshape(batch, seq, Z_WIDTH)[:, seq - keep:]
        k_rows = zk[:, :, Z_KA:Z_KA + W_A].reshape(batch, keep, H_A, DH_A)
        v_rows = zk[:, :, Z_VA:Z_VA + W_A].reshape(batch, keep, H_A, DH_A)

    x1, h2t = _merge(oa, ob, z, xf, gt1, sc2, sh2, g2, wa, wb, wo, tm_mg, is_sample, seq)
    rank1, b1, cnt0, a0 = _route(h2t, wqk_t)
    y = _peer(h2t, u_bf, vt_bf, rank1, b1, cnt0, a0, x1, gt2, gfin, is_sample, seq)
    return y.reshape(batch, seq, D_MODEL), k_rows, v_rows, s_new


def kernel(x_prompt, x_sample, cache_a_k, cache_a_v, state_gla, c_prompt, c_sample, w_ada, b_ada, g_norm1, w_in, rel_bias, w_alpha_up, b_alpha, g_gla_norm, w_branch_a, w_branch_b, w_out, g_norm2, w_peer_q, peer_subkeys, peer_u, peer_v, g_final):
    depth = w_in.shape[0]
    assert depth == 1, "the final rmsnorm is fused into the PEER kernel of the only layer"
    nbp = c_prompt.shape[0]
    l = 0
    nbs = c_sample.shape[0]
    pad = -(nbp + nbs) % 16
    c_all = jnp.concatenate([c_prompt, c_sample, jnp.zeros((pad, D_MODEL), F32)], axis=0)
    def only(w):
        return w.reshape(w.shape[1:])

    mod = _ada(c_all, only(w_ada), b_ada[l])

    alr_lo = Z_GL
    w_in0 = only(w_in)
    w_main = jnp.concatenate([w_in0[:, :alr_lo].astype(BF16), w_in0[:, alr_lo + ALPHA_RANK:].astype(BF16)], axis=1)
    w_alr = jnp.pad(w_in0[:, alr_lo:alr_lo + ALPHA_RANK], ((0, 0), (0, ALR_PAD - ALPHA_RANK))).astype(BF16)
    wup = jnp.pad(w_alpha_up[l], ((0, ALR_PAD - ALPHA_RANK), (0, 0))).astype(BF16)
    wqk = _qkfold(only(w_peer_q).astype(BF16), peer_subkeys[l].astype(BF16))
    wqk_t = wqk.reshape(D_MODEL, PEER_HEADS, 2, N_KEYS).transpose(2, 3, 1, 0).reshape(2 * PEER_HEADS * N_KEYS, D_MODEL)
    wts = (
        g_norm1[l].reshape(1, D_MODEL), w_main, w_alr, rel_bias[l], wup,
        b_alpha[l].reshape(1, KW_B), g_gla_norm[l].reshape(1, DV_B),
        only(w_branch_a).astype(BF16), only(w_branch_b).astype(BF16), only(w_out).astype(BF16),
        g_norm2[l].reshape(1, D_MODEL), wqk_t.astype(BF16),
        only(peer_u).astype(BF16),
        jnp.transpose(peer_v.astype(BF16).reshape(N_EXPERTS // PEER_E, PEER_E, D_MODEL), (0, 2, 1)),
        g_final.reshape(1, D_MODEL),
    )
    yp, kp, vp, sp = _layer(x_prompt, mod[:nbp], None, None, None, False, wts)
    ys, ks, vs, ss = _layer(x_sample, mod[nbp:nbp + nbs], only(cache_a_k), only(cache_a_v), only(state_gla), True, wts)
    return (yp, ys, kp[None], vp[None], sp[None], ks[None], vs[None], ss[None])
```

```python
import functools
import math

import jax
import jax.numpy as jnp
from jax import lax
from jax.experimental import pallas as pl
from jax.experimental.pallas import tpu as pltpu

F32 = jnp.float32
BF16 = jnp.bfloat16

D_MODEL = 2048
CHUNK = 64
N_LEFT_CHUNKS = 8
BAND_LEFT = N_LEFT_CHUNKS * CHUNK
H_A = 8
DH_A = 128
W_A = H_A * DH_A
REL_CLIP = 128
H_B = 4
DK_B = 128
DV_B = 256
KW_B = H_B * DK_B
VW_B = H_B * DV_B
ALPHA_RANK = 16
GATE_TAU = 16.0
PEER_HEADS = 8
N_KEYS = 128
N_EXPERTS = N_KEYS * N_KEYS
PEER_TOPK = 16
EPS = 1e-6
NEG_INF = -1e30

Z_QA, Z_KA, Z_VA = 0, W_A, 2 * W_A
Z_QB = 3 * W_A
Z_KB = Z_QB + KW_B
Z_VB = Z_KB + KW_B
Z_RB = Z_VB + VW_B
Z_GL = Z_RB + VW_B
Z_WIDTH = Z_GL + 2 * D_MODEL
ALR_PAD = 128

VMEM_LIMIT_BIG = 60 * 1024 * 1024
VMEM_LIMIT_MID = 40 * 1024 * 1024

Q_TILE = 512
Q_SUB = 128
K_WIN = Q_SUB + BAND_LEFT


def _cparams(sem, vmem=None):
    return pltpu.CompilerParams(dimension_semantics=sem, vmem_limit_bytes=vmem)


def _rms(xf):
    return xf * lax.rsqrt(jnp.mean(xf * xf, axis=-1, keepdims=True) + EPS)


def _sigmoid(x):
    return 1.0 / (1.0 + jnp.exp(-x))


def _ada_kernel(c_ref, w_ref, b_ref, o_ref):
    c = c_ref[...]
    a = (c * _sigmoid(c)).astype(BF16)
    o_ref[...] = jnp.dot(a, w_ref[...].astype(BF16), preferred_element_type=F32) + b_ref[...]


def _ada(c_all, w_ada, b_ada):
    nb = c_all.shape[0]
    n_out = w_ada.shape[1]
    tn = 1024
    return pl.pallas_call(
        _ada_kernel,
        grid=(n_out // tn,),
        in_specs=[
            pl.BlockSpec((nb, D_MODEL), lambda j: (0, 0)),
            pl.BlockSpec((D_MODEL, tn), lambda j: (0, j)),
            pl.BlockSpec((1, tn), lambda j: (0, j)),
        ],
        out_specs=pl.BlockSpec((nb, tn), lambda j: (0, j)),
        out_shape=jax.ShapeDtypeStruct((nb, n_out), F32),
        compiler_params=_cparams(("arbitrary",), VMEM_LIMIT_MID),
        name="ada",
    )(c_all, w_ada, b_ada.reshape(1, n_out))


def _mod_spec(per_token, tm, tiles_per_batch, grid_rank):
    if per_token:
        if grid_rank == 1:
            return pl.BlockSpec((tm, D_MODEL), lambda m: (m, 0))
        return pl.BlockSpec((tm, D_MODEL), lambda m, n: (m, 0))
    if grid_rank == 1:
        return pl.BlockSpec((None, 1, D_MODEL), lambda m: (m // tiles_per_batch, 0, 0))
    return pl.BlockSpec((None, 1, D_MODEL), lambda m, n: (m // tiles_per_batch, 0, 0))


def _inproj_kernel(x_ref, sh_ref, sc_ref, g_ref, w_ref, walr_ref, z_ref, alr_ref, h_scr):
    @pl.when(pl.program_id(1) == 0)
    def _():
        h = _rms(x_ref[...]) * g_ref[...]
        h = h * (1.0 + sc_ref[...]) + sh_ref[...]
        hb = h.astype(BF16)
        h_scr[...] = hb
        alr_ref[...] = jnp.dot(hb, walr_ref[...], preferred_element_type=F32)

    z_ref[...] = jnp.dot(h_scr[...], w_ref[...], preferred_element_type=F32)


def _inproj(x, sh, sc, g, w_main, w_alr, tm, per_token, rows_per_batch):
    n = x.shape[0]
    tn = 1024
    tpb = max(rows_per_batch // tm, 1)
    return pl.pallas_call(
        _inproj_kernel,
        grid=(n // tm, Z_WIDTH // tn),
        in_specs=[
            pl.BlockSpec((tm, D_MODEL), lambda m, j: (m, 0)),
            _mod_spec(per_token, tm, tpb, 2),
            _mod_spec(per_token, tm, tpb, 2),
            pl.BlockSpec((1, D_MODEL), lambda m, j: (0, 0)),
            pl.BlockSpec((D_MODEL, tn), lambda m, j: (0, j)),
            pl.BlockSpec((D_MODEL, ALR_PAD), lambda m, j: (0, 0)),
        ],
        out_specs=[
            pl.BlockSpec((tm, tn), lambda m, j: (m, j)),
            pl.BlockSpec((tm, ALR_PAD), lambda m, j: (m, 0)),
        ],
        out_shape=[
            jax.ShapeDtypeStruct((n, Z_WIDTH), F32),
            jax.ShapeDtypeStruct((n, ALR_PAD), F32),
        ],
        scratch_shapes=[pltpu.VMEM((tm, D_MODEL), BF16)],
        compiler_params=_cparams(("parallel", "arbitrary"), VMEM_LIMIT_BIG),
        name="inproj",
    )(x, sh, sc, g, w_main, w_alr)


def _attn_prompt_kernel(q_ref, kp_ref, kc_ref, vp_ref, vc_ref, bias_ref, o_ref, kw, vw):
    first = pl.program_id(2) == 0
    kw[0:Q_TILE, :] = kp_ref[...].astype(BF16)
    kw[Q_TILE:2 * Q_TILE, :] = kc_ref[...].astype(BF16)
    vw[0:Q_TILE, :] = vp_ref[...].astype(BF16)
    vw[Q_TILE:2 * Q_TILE, :] = vc_ref[...].astype(BF16)
    scale = DH_A ** -0.5
    probs_ = [(hh, r) for hh in range(ATT_HEADS) for r in range(Q_TILE // Q_SUB)]
    hcol = [slice(hh * DH_A, (hh + 1) * DH_A) for hh in range(ATT_HEADS)]
    scores = [lax.dot_general((q_ref[r * Q_SUB:(r + 1) * Q_SUB, hcol[hh]] * scale).astype(BF16),
                              kw[r * Q_SUB:r * Q_SUB + K_WIN, hcol[hh]],
                              (((1,), (1,)), ((), ())), preferred_element_type=F32) for (hh, r) in probs_]
    probs, sums = [], []
    col = lax.broadcasted_iota(jnp.int32, (1, K_WIN), 1)
    for idx, (hh, r) in enumerate(probs_):
        before = jnp.where(jnp.logical_and(first, col + r * Q_SUB < Q_TILE), NEG_INF, 0.0)
        s = scores[idx] + bias_ref[hh] + before
        m = jnp.max(s, axis=-1, keepdims=True)
        p = jnp.exp(s - m)
        sums.append(jnp.sum(p, axis=-1, keepdims=True))
        probs.append(p.astype(BF16))
    outs = [jnp.dot(probs[idx], vw[r * Q_SUB:r * Q_SUB + K_WIN, hcol[hh]], preferred_element_type=F32)
            for idx, (hh, r) in enumerate(probs_)]
    for idx, (hh, r) in enumerate(probs_):
        o_ref[r * Q_SUB:(r + 1) * Q_SUB, hcol[hh]] = (outs[idx] / sums[idx]).astype(o_ref.dtype)


ATT_HEADS = 4


def _attn_prompt(z, bias, batch, seq):
    n = batch * seq
    tiles = seq // Q_TILE
    width = ATT_HEADS * DH_A

    def cur(col):
        return pl.BlockSpec((Q_TILE, width), lambda h, b, i: (b * tiles + i, col + h))

    def prev(col):
        return pl.BlockSpec((Q_TILE, width),
                            lambda h, b, i: (b * tiles + jnp.maximum(i - 1, 0), col + h))

    kcol, vcol = Z_KA // width, Z_VA // width
    return pl.pallas_call(
        _attn_prompt_kernel,
        grid=(H_A // ATT_HEADS, batch, tiles),
        in_specs=[cur(0), prev(kcol), cur(kcol), prev(vcol), cur(vcol),
                  pl.BlockSpec((ATT_HEADS, Q_SUB, K_WIN), lambda h, b, i: (h, 0, 0))],
        out_specs=pl.BlockSpec((Q_TILE, width), lambda h, b, i: (b * tiles + i, h)),
        out_shape=jax.ShapeDtypeStruct((n, W_A), BF16),
        scratch_shapes=[pltpu.VMEM((2 * Q_TILE, width), BF16), pltpu.VMEM((2 * Q_TILE, width), BF16)],
        compiler_params=_cparams(("arbitrary", "arbitrary", "arbitrary")),
        name="attn_prompt",
    )(z, z, z, z, z, bias)


def _attn_sample_kernel(q_ref, kn_ref, vn_ref, ck_ref, cv_ref, bc_ref, bn_ref, o_ref):
    scale = DH_A ** -0.5
    nt = (((1,), (1,)), ((), ()))
    lc = ck_ref.shape[0] // H_A
    heads = range(H_A)
    cols = [slice(h * DH_A, (h + 1) * DH_A) for h in heads]
    crow = [pl.ds(h, lc, stride=H_A) for h in heads]
    qs = [q_ref[:, cols[h]].astype(BF16) for h in heads]
    s_c = [lax.dot_general(qs[h], ck_ref[crow[h], :].astype(BF16), nt, preferred_element_type=F32) for h in heads]
    s_n = [lax.dot_general(qs[h], kn_ref[:, cols[h]].astype(BF16), nt, preferred_element_type=F32) for h in heads]
    p_c, p_n, sums = [], [], []
    for h in heads:
        sc = s_c[h] * scale + bc_ref[h]
        sn = s_n[h] * scale + bn_ref[h]
        m = jnp.maximum(jnp.max(sc, axis=-1, keepdims=True), jnp.max(sn, axis=-1, keepdims=True))
        pc = jnp.exp(sc - m)
        pn = jnp.exp(sn - m)
        sums.append(jnp.sum(pc, axis=-1, keepdims=True) + jnp.sum(pn, axis=-1, keepdims=True))
        p_c.append(pc.astype(BF16))
        p_n.append(pn.astype(BF16))
    outs = [jnp.dot(p_c[h], cv_ref[crow[h], :].astype(BF16), preferred_element_type=F32)
            + jnp.dot(p_n[h], vn_ref[:, cols[h]].astype(BF16), preferred_element_type=F32) for h in heads]
    for h in heads:
        o_ref[:, cols[h]] = (outs[h] / sums[h]).astype(o_ref.dtype)


def _attn_sample(z, cache_k, cache_v, bias_c, bias_n, batch, seq):
    lc = cache_k.shape[1] // H_A
    return pl.pallas_call(
        _attn_sample_kernel,
        grid=(batch,),
        in_specs=[
            pl.BlockSpec((seq, W_A), lambda b: (b, Z_QA // W_A)),
            pl.BlockSpec((seq, W_A), lambda b: (b, Z_KA // W_A)),
            pl.BlockSpec((seq, W_A), lambda b: (b, Z_VA // W_A)),
            pl.BlockSpec((None, lc * H_A, DH_A), lambda b: (b, 0, 0)),
            pl.BlockSpec((None, lc * H_A, DH_A), lambda b: (b, 0, 0)),
            pl.BlockSpec((H_A, seq, lc), lambda b: (0, 0, 0)),
            pl.BlockSpec((H_A, seq, seq), lambda b: (0, 0, 0)),
        ],
        out_specs=pl.BlockSpec((seq, W_A), lambda b: (b, 0)),
        out_shape=jax.ShapeDtypeStruct((batch * seq, W_A), BF16),
        compiler_params=_cparams(("arbitrary",), VMEM_LIMIT_MID),
        name="attn_sample",
    )(z, z, z, cache_k, cache_v, bias_c, bias_n)


def _split_bf16(x):
    hi = x.astype(BF16)
    lo = (x - hi.astype(F32)).astype(BF16)
    return hi, lo


_TN = (((0,), (0,)), ((), ()))
_NT = (((1,), (1,)), ((), ()))


def _gla_state_free(units, tri, causal, c):
    scale = DK_B ** -0.5
    ones = jnp.ones((c, DK_B), BF16)
    split = [_split_bf16(g) for (_, _, _, g) in units]
    cum = [jnp.dot(tri, hi, preferred_element_type=F32) + jnp.dot(tri, lo, preferred_element_type=F32)
           for (hi, lo) in split]
    tot = [lax.dot_general(hi, ones, _TN, preferred_element_type=F32)
           + lax.dot_general(lo, ones, _TN, preferred_element_type=F32) for (hi, lo) in split]
    qd, kd, vb, att = [], [], [], []
    for (q, k, v, _), b in zip(units, cum):
        mid = b[c // 2:c // 2 + 1, :]
        blast = b[c - 1:c, :]
        qs = q * scale
        ks = k * scale
        att.append(lax.dot_general((qs * jnp.exp(b - mid)).astype(BF16), (ks * jnp.exp(mid - b)).astype(BF16),
                                   _NT, preferred_element_type=F32))
        qd.append((qs * jnp.exp(b)).astype(BF16))
        kd.append((ks * jnp.exp(blast - b)).astype(BF16))
        vb.append(v.astype(BF16))
    intra = [jnp.dot(jnp.where(causal, a, 0.0).astype(BF16), v, preferred_element_type=F32)
             for a, v in zip(att, vb)]
    out = []
    for i in range(len(units)):
        dec = jnp.exp(tot[i])
        out.append((intra[i], qd[i], kd[i], vb[i], jnp.concatenate([dec, dec], axis=1)))
    return out


def _gla_read_state(pre, s_prev):
    o_intra, qd, _, _, _ = pre
    return o_intra + jnp.dot(qd, s_prev.astype(BF16), preferred_element_type=F32)


def _gla_next_state(pre, s_prev):
    _, _, kd, vb, dec = pre
    return dec * s_prev + lax.dot_general(kd, vb, _TN, preferred_element_type=F32)


def _log_decay(alr, wup, balpha):
    x = jnp.dot(alr.astype(BF16), wup, preferred_element_type=F32) + balpha
    return (jnp.minimum(x, 0.0) - jnp.log1p(jnp.exp(-jnp.abs(x)))) * (1.0 / GATE_TAU)


def _gla_epilogue(o, rb, gnorm):
    y = _rms(o) * gnorm
    return y * (rb * _sigmoid(rb))


def _tri_and_causal(c):
    row = lax.broadcasted_iota(jnp.int32, (c, c), 0)
    col = lax.broadcasted_iota(jnp.int32, (c, c), 1)
    causal = col <= row
    return jnp.where(causal, 1.0, 0.0).astype(BF16), causal


def _gla_prompt_kernel(q_ref, k_ref, v_ref, rb_ref, alr_ref, wup_ref, balpha_ref, gnorm_ref,
                       ob_ref, st_ref, s_scr, *, batch, blocks):
    step = pl.program_id(0)

    @pl.when(step == 0)
    def _():
        s_scr[...] = jnp.zeros_like(s_scr)

    tri, causal = _tri_and_causal(CHUNK)
    wup = wup_ref[...]
    balpha = balpha_ref[...]
    gnorm = gnorm_ref[...]
    ksl = [slice(h * DK_B, (h + 1) * DK_B) for h in range(H_B)]
    vsl = [slice(h * DV_B, (h + 1) * DV_B) for h in range(H_B)]
    streams = [(bi, h) for bi in range(batch) for h in range(H_B)]
    rows = [slice(blk * CHUNK, (blk + 1) * CHUNK) for blk in range(blocks)]
    logg = {(blk, bi): _log_decay(alr_ref[bi, rows[blk], :], wup, balpha)
            for blk in range(blocks) for bi in range(batch)}
    units = [(q_ref[bi, rows[blk], ksl[h]], k_ref[bi, rows[blk], ksl[h]], v_ref[bi, rows[blk], vsl[h]],
              logg[(blk, bi)][:, ksl[h]]) for blk in range(blocks) for (bi, h) in streams]
    pre = _gla_state_free(units, tri, causal, CHUNK)
    for blk in range(blocks):
        mine = pre[blk * len(streams):(blk + 1) * len(streams)]
        outs = [_gla_read_state(p, s_scr[si]) for si, p in enumerate(mine)]
        nxt = [_gla_next_state(p, s_scr[si]) for si, p in enumerate(mine)]
        for si, (bi, h) in enumerate(streams):
            s_scr[si] = nxt[si]
            ob_ref[bi, rows[blk], vsl[h]] = _gla_epilogue(outs[si], rb_ref[bi, rows[blk], vsl[h]],
                                                          gnorm).astype(ob_ref.dtype)

    @pl.when(step == pl.num_programs(0) - 1)
    def _():
        st_ref[...] = s_scr[...]


def _gla_prompt(z, alr, wup, balpha, gnorm, batch, seq):
    blocks = 4
    rows = blocks * CHUNK
    z3 = z.reshape(batch, seq, Z_WIDTH)
    alr3 = alr.reshape(batch, seq, ALR_PAD)
    ob, st = pl.pallas_call(
        functools.partial(_gla_prompt_kernel, batch=batch, blocks=blocks),
        grid=(seq // rows,),
        in_specs=[
            pl.BlockSpec((batch, rows, KW_B), lambda j: (0, j, Z_QB // KW_B)),
            pl.BlockSpec((batch, rows, KW_B), lambda j: (0, j, Z_KB // KW_B)),
            pl.BlockSpec((batch, rows, VW_B), lambda j: (0, j, Z_VB // VW_B)),
            pl.BlockSpec((batch, rows, VW_B), lambda j: (0, j, Z_RB // VW_B)),
            pl.BlockSpec((batch, rows, ALR_PAD), lambda j: (0, j, 0)),
            pl.BlockSpec((ALR_PAD, KW_B), lambda j: (0, 0)),
            pl.BlockSpec((1, KW_B), lambda j: (0, 0)),
            pl.BlockSpec((1, DV_B), lambda j: (0, 0)),
        ],
        out_specs=[
            pl.BlockSpec((batch, rows, VW_B), lambda j: (0, j, 0)),
            pl.BlockSpec((batch * H_B, DK_B, DV_B), lambda j: (0, 0, 0)),
        ],
        out_shape=[
            jax.ShapeDtypeStruct((batch, seq, VW_B), BF16),
            jax.ShapeDtypeStruct((batch * H_B, DK_B, DV_B), F32),
        ],
        scratch_shapes=[pltpu.VMEM((batch * H_B, DK_B, DV_B), F32)],
        compiler_params=_cparams(("arbitrary",), VMEM_LIMIT_MID),
        name="gla_prompt",
    )(z3, z3, z3, z3, alr3, wup, balpha, gnorm)
    return ob.reshape(batch * seq, VW_B), st.reshape(batch, H_B, DK_B, DV_B)


def _gla_sample_kernel(q_ref, k_ref, v_ref, rb_ref, alr_ref, s0_ref, wup_ref, balpha_ref, gnorm_ref,
                       ob_ref, st_ref, *, seq):
    tri, causal = _tri_and_causal(seq)
    logg = _log_decay(alr_ref[...], wup_ref[...], balpha_ref[...])
    gnorm = gnorm_ref[...]
    ksl = [slice(h * DK_B, (h + 1) * DK_B) for h in range(H_B)]
    vsl = [slice(h * DV_B, (h + 1) * DV_B) for h in range(H_B)]
    pre = _gla_state_free([(q_ref[:, ksl[h]], k_ref[:, ksl[h]], v_ref[:, vsl[h]], logg[:, ksl[h]])
                           for h in range(H_B)], tri, causal, seq)
    outs = [_gla_read_state(pre[h], s0_ref[h]) for h in range(H_B)]
    nxt = [_gla_next_state(pre[h], s0_ref[h]) for h in range(H_B)]
    for h in range(H_B):
        st_ref[h] = nxt[h]
        ob_ref[:, vsl[h]] = _gla_epilogue(outs[h], rb_ref[:, vsl[h]], gnorm).astype(ob_ref.dtype)


def _gla_sample(z, alr, state, wup, balpha, gnorm, batch, seq):
    return pl.pallas_call(
        functools.partial(_gla_sample_kernel, seq=seq),
        grid=(batch,),
        in_specs=[
            pl.BlockSpec((seq, KW_B), lambda b: (b, Z_QB // KW_B)),
            pl.BlockSpec((seq, KW_B), lambda b: (b, Z_KB // KW_B)),
            pl.BlockSpec((seq, VW_B), lambda b: (b, Z_VB // VW_B)),
            pl.BlockSpec((seq, VW_B), lambda b: (b, Z_RB // VW_B)),
            pl.BlockSpec((seq, ALR_PAD), lambda b: (b, 0)),
            pl.BlockSpec((None, H_B, DK_B, DV_B), lambda b: (b, 0, 0, 0)),
            pl.BlockSpec((ALR_PAD, KW_B), lambda b: (0, 0)),
            pl.BlockSpec((1, KW_B), lambda b: (0, 0)),
            pl.BlockSpec((1, DV_B), lambda b: (0, 0)),
        ],
        out_specs=[
            pl.BlockSpec((seq, VW_B), lambda b: (b, 0)),
            pl.BlockSpec((None, H_B, DK_B, DV_B), lambda b: (b, 0, 0, 0)),
        ],
        out_shape=[
            jax.ShapeDtypeStruct((batch * seq, VW_B), BF16),
            jax.ShapeDtypeStruct((batch, H_B, DK_B, DV_B), F32),
        ],
        compiler_params=_cparams(("arbitrary",)),
        name="gla_sample",
    )(z, z, z, z, alr, state, wup, balpha, gnorm)


def _merge_kernel(oa_ref, ob_ref, gla_ref, glb_ref, x_ref, gt1_ref, sc2_ref, sh2_ref, g2_ref,
                  wa_ref, wb_ref, wo_ref, x1_ref, h2t_ref):
    ya = jnp.dot(oa_ref[...], wa_ref[...], preferred_element_type=F32)
    yb = jnp.dot(ob_ref[...], wb_ref[...], preferred_element_type=F32)
    merged = _sigmoid(gla_ref[...]) * ya + _sigmoid(glb_ref[...]) * yb
    x1 = x_ref[...] + gt1_ref[...] * jnp.dot(merged.astype(BF16), wo_ref[...], preferred_element_type=F32)
    x1_ref[...] = x1
    h2 = _rms(x1) * g2_ref[...]
    h2 = h2 * (1.0 + sc2_ref[...]) + sh2_ref[...]
    h2t_ref[...] = jnp.transpose(h2).astype(BF16)


def _merge(oa, ob, z, x, gt1, sc2, sh2, g2, wa, wb, wo, tm, per_token, rows_per_batch):
    n = x.shape[0]
    tpb = max(rows_per_batch // tm, 1)
    const = lambda shape: pl.BlockSpec(shape, lambda m: (0, 0))
    mod = _mod_spec(per_token, tm, tpb, 1)
    return pl.pallas_call(
        _merge_kernel,
        grid=(n // tm,),
        in_specs=[
            pl.BlockSpec((tm, W_A), lambda m: (m, 0)),
            pl.BlockSpec((tm, VW_B), lambda m: (m, 0)),
            pl.BlockSpec((tm, D_MODEL), lambda m: (m, Z_GL // D_MODEL)),
            pl.BlockSpec((tm, D_MODEL), lambda m: (m, Z_GL // D_MODEL + 1)),
            pl.BlockSpec((tm, D_MODEL), lambda m: (m, 0)),
            mod, mod, mod,
            const((1, D_MODEL)),
            const((W_A, D_MODEL)), const((VW_B, D_MODEL)), const((D_MODEL, D_MODEL)),
        ],
        out_specs=[
            pl.BlockSpec((tm, D_MODEL), lambda m: (m, 0)),
            pl.BlockSpec((D_MODEL, tm), lambda m: (0, m)),
        ],
        out_shape=[
            jax.ShapeDtypeStruct((n, D_MODEL), F32),
            jax.ShapeDtypeStruct((D_MODEL, n), BF16),
        ],
        compiler_params=_cparams(("arbitrary",), VMEM_LIMIT_BIG),
        name="merge",
    )(oa, ob, z, z, x, gt1, sc2, sh2, g2, wa, wb, wo)


def _qkfold_kernel(wq_ref, sk_ref, o_ref):
    o_ref[...] = lax.dot_general(wq_ref[...], sk_ref[...], (((1,), (1,)), ((), ())),
                                 preferred_element_type=F32)


def _qkfold(w_q, subkeys):
    half = N_KEYS
    return pl.pallas_call(
        _qkfold_kernel,
        grid=(2 * PEER_HEADS,),
        in_specs=[
            pl.BlockSpec((D_MODEL, half), lambda c: (0, c)),
            pl.BlockSpec((None, N_KEYS, half), lambda c: (c % 2, 0, 0)),
        ],
        out_specs=pl.BlockSpec((D_MODEL, N_KEYS), lambda c: (0, c)),
        out_shape=jax.ShapeDtypeStruct((D_MODEL, 2 * PEER_HEADS * N_KEYS), F32),
        compiler_params=_cparams(("arbitrary",)),
        name="qkfold",
    )(w_q, subkeys)


ROUTE_T = 256
LANES = 128
_SET_ROWS = N_KEYS * PEER_HEADS
_PAIRS = [(r, q) for r in range(PEER_TOPK) for q in range(PEER_TOPK) if (r + 1) * (q + 1) <= PEER_TOPK]


def _sort_desc(x):
    x = list(x)
    n = len(x)
    k = 2
    while k <= n:
        j = k // 2
        while j >= 1:
            for i in range(n):
                l = i ^ j
                if l > i:
                    hi, lo = jnp.maximum(x[i], x[l]), jnp.minimum(x[i], x[l])
                    x[i], x[l] = (hi, lo) if (i & k) == 0 else (lo, hi)
            j //= 2
        k *= 2
    return x


def _merge_top(a, b):
    n = len(a)
    x = [jnp.maximum(a[i], b[n - 1 - i]) for i in range(n)]
    j = n // 2
    while j >= 1:
        for i in range(n):
            l = i ^ j
            if l > i:
                x[i], x[l] = jnp.maximum(x[i], x[l]), jnp.minimum(x[i], x[l])
        j //= 2
    return x


def _top_sorted(vals, top):
    groups = [_sort_desc(vals[g:g + top]) for g in range(0, len(vals), top)]
    while len(groups) > 1:
        groups = [_merge_top(groups[g], groups[g + 1]) for g in range(0, len(groups), 2)]
    return groups[0]


def _route_fast(s0, s1):
    shp = s0[0].shape
    one = jnp.ones(shp, F32)
    zero = jnp.zeros(shp, F32)
    top0 = _top_sorted(s0, PEER_TOPK)
    top1 = _top_sorted(s1, PEER_TOPK)
    cand = [top0[r] + top1[q] for (r, q) in _PAIRS]
    pad = [jnp.full(shp, -jnp.inf, F32)] * (-len(cand) % PEER_TOPK)
    topc = _top_sorted(cand + pad, PEER_TOPK)
    tau = topc[PEER_TOPK - 1]
    cnt = [zero] * PEER_TOPK
    for c, (r, q) in enumerate(_PAIRS):
        cnt[r] = cnt[r] + jnp.where(cand[c] >= tau, one, zero)
    picked = cnt[0]
    for r in range(1, PEER_TOPK):
        picked = picked + cnt[r]
    zsum = zero
    for r in range(PEER_TOPK):
        zsum = zsum + jnp.exp(topc[r] - topc[0])
    zinv = 1.0 / zsum
    tie = jnp.where(picked != float(PEER_TOPK), one, zero)
    for t in (top0, top1):
        for r in range(PEER_TOPK - 1):
            tie = jnp.where(t[r] == t[r + 1], one, tie)
    in0, in1 = zero, zero
    rank1, b1, cnt0, a0 = [], [], [], []
    for k in range(N_KEYS):
        v0, v1 = s0[k], s1[k]
        c0 = zero
        rk = jnp.full(shp, float(PEER_TOPK), F32)
        for r in range(PEER_TOPK - 1, -1, -1):
            c0 = jnp.where(v0 >= top0[r], cnt[r], c0)
            rk = jnp.where(v1 >= top1[r], float(r), rk)
        in0 = in0 + jnp.where(v0 >= top0[PEER_TOPK - 1], one, zero)
        in1 = in1 + jnp.where(v1 >= top1[PEER_TOPK - 1], one, zero)
        cnt0.append(c0)
        rank1.append(rk)
        a0.append(jnp.exp(v0 - top0[0]) * zinv)
        b1.append(jnp.exp(v1 - top1[0]))
    tie = jnp.where(in0 != float(PEER_TOPK), one, tie)
    tie = jnp.where(in1 != float(PEER_TOPK), one, tie)
    return rank1, b1, cnt0, a0, tie


def _route_kernel(h2t_ref, wqk_ref, rank1_ref, b1_ref, cnt0_ref, a0_ref,
                  s_scr, sw_scr, rk_scr, val_scr, idx_scr, tmp_scr):
    halves = h2t_ref.shape[1] // LANES
    shp = (PEER_HEADS, LANES)
    s = jnp.dot(wqk_ref[...], h2t_ref[...], preferred_element_type=F32)
    for hf in range(halves):
        s_scr[hf] = s[:, hf * LANES:(hf + 1) * LANES]
    neg = jnp.full(shp, -jnp.inf, F32)

    def rows(p, k):
        return pl.ds(p * _SET_ROWS + k * PEER_HEADS, PEER_HEADS)

    def one_half(hf, carry0):
        f_rank1, f_b1, f_cnt0, f_a0, tie = _route_fast([s_scr[hf, rows(0, k), :] for k in range(N_KEYS)],
                                                       [s_scr[hf, rows(1, k), :] for k in range(N_KEYS)])
        for k in range(N_KEYS):
            kr = pl.ds(k * PEER_HEADS, PEER_HEADS)
            tmp_scr[hf, 0, kr, :] = f_cnt0[k]
            tmp_scr[hf, 1, kr, :] = f_a0[k]
            tmp_scr[hf, 2, kr, :] = f_b1[k]
            rk_scr[hf, rows(1, k), :] = f_rank1[k]

        @pl.when(jnp.max(tie) > 0.0)
        def _():
            exact_half(hf)

        return carry0

    def exact_half(hf):
        sw_scr[hf] = s_scr[hf]
        rk_scr[hf] = jnp.full(rk_scr.shape[1:], float(PEER_TOPK), F32)

        def extract(r, carry):
            rf = jnp.full(shp, r, jnp.int32).astype(F32)
            for p in range(2):
                best, bidx = None, None
                for part in range(4):
                    m = neg
                    ix = jnp.zeros(shp, jnp.int32)
                    for k in range(part * 32, part * 32 + 32):
                        v = sw_scr[hf, rows(p, k), :]
                        gt = v > m
                        m = jnp.where(gt, v, m)
                        ix = jnp.where(gt, k, ix)
                    if best is None:
                        best, bidx = m, ix
                    else:
                        gt = m > best
                        best = jnp.where(gt, m, best)
                        bidx = jnp.where(gt, ix, bidx)
                val_scr[p, r] = best
                idx_scr[p, r] = bidx
                for k in range(N_KEYS):
                    hit = bidx == k
                    sw_scr[hf, rows(p, k), :] = jnp.where(hit, -jnp.inf, sw_scr[hf, rows(p, k), :])
                    rk_scr[hf, rows(p, k), :] = jnp.where(hit, rf, rk_scr[hf, rows(p, k), :])
            return carry

        lax.fori_loop(0, PEER_TOPK, extract, 0)

        v0 = [val_scr[0, r] for r in range(PEER_TOPK)]
        v1 = [val_scr[1, q] for q in range(PEER_TOPK)]
        cand = [v0[r] + v1[q] for (r, q) in _PAIRS]
        top = v0[0] + v1[0]
        cnt = [jnp.zeros(shp, F32) for _ in range(PEER_TOPK)]
        zsum = jnp.zeros(shp, F32)
        for _ in range(PEER_TOPK):
            m = neg
            ix = jnp.zeros(shp, jnp.int32)
            for c, cv in enumerate(cand):
                gt = cv > m
                m = jnp.where(gt, cv, m)
                ix = jnp.where(gt, c, ix)
            zsum = zsum + jnp.exp(m - top)
            for c, (r, q) in enumerate(_PAIRS):
                hit = ix == c
                cand[c] = jnp.where(hit, -jnp.inf, cand[c])
                cnt[r] = cnt[r] + jnp.where(hit, 1.0, 0.0)
        zinv = 1.0 / zsum

        idx0 = [idx_scr[0, r] for r in range(PEER_TOPK)]
        for k in range(N_KEYS):
            c0 = jnp.zeros(shp, F32)
            for r in range(PEER_TOPK):
                c0 = jnp.where(idx0[r] == k, cnt[r], c0)
            kr = pl.ds(k * PEER_HEADS, PEER_HEADS)
            tmp_scr[hf, 0, kr, :] = c0
            tmp_scr[hf, 1, kr, :] = jnp.exp(s_scr[hf, rows(0, k), :] - v0[0]) * zinv
            tmp_scr[hf, 2, kr, :] = jnp.exp(s_scr[hf, rows(1, k), :] - v1[0])

    lax.fori_loop(0, halves, one_half, 0)

    for hf in range(halves):
        ls = slice(hf * LANES, (hf + 1) * LANES)
        for h in range(PEER_HEADS):
            dst = slice(h * N_KEYS, (h + 1) * N_KEYS)
            src = pl.ds(h, N_KEYS, stride=PEER_HEADS)
            cnt0_ref[dst, ls] = _pack_pair(tmp_scr[hf, 0, src, :])
            a0_ref[dst, ls] = _pack_pair(tmp_scr[hf, 1, src, :])
            b1_ref[dst, ls] = tmp_scr[hf, 2, src, :].astype(BF16)
            rank1_ref[dst, ls] = rk_scr[hf, pl.ds(_SET_ROWS + h, N_KEYS, stride=PEER_HEADS), :].astype(BF16)


def _pack_pair(x):
    bits = pltpu.bitcast(x.astype(BF16).astype(F32), jnp.uint32)
    return bits | (bits >> 16)


def _route(h2t, wqk_t):
    n = h2t.shape[1]
    t = ROUTE_T
    rows = PEER_HEADS * N_KEYS
    ospec = pl.BlockSpec((rows, t), lambda m: (0, m))
    out_bf = jax.ShapeDtypeStruct((rows, n), BF16)
    out_pk = jax.ShapeDtypeStruct((rows, n), jnp.uint32)
    return pl.pallas_call(
        _route_kernel,
        grid=(n // t,),
        in_specs=[
            pl.BlockSpec((D_MODEL, t), lambda m: (0, m)),
            pl.BlockSpec((2 * rows, D_MODEL), lambda m: (0, 0)),
        ],
        out_specs=[ospec, ospec, ospec, ospec],
        out_shape=[out_bf, out_bf, out_pk, out_pk],
        scratch_shapes=[
            pltpu.VMEM((t // LANES, 2 * rows, LANES), F32),
            pltpu.VMEM((t // LANES, 2 * rows, LANES), F32),
            pltpu.VMEM((t // LANES, 2 * rows, LANES), F32),
            pltpu.VMEM((2, PEER_TOPK, PEER_HEADS, LANES), F32),
            pltpu.VMEM((2, PEER_TOPK, PEER_HEADS, LANES), jnp.int32),
            pltpu.VMEM((t // LANES, 3, rows, LANES), F32),
        ],
        compiler_params=_cparams(("arbitrary",), VMEM_LIMIT_MID),
        name="route",
    )(h2t, wqk_t)


PEER_T = 512
PEER_E = 512


def _gelu(x):
    return 0.5 * x * (1.0 + lax.erf(x * (2.0 ** -0.5)))


def _bcast_pair_rows(ref, row, t):
    word = jnp.broadcast_to(ref[pl.ds(row, 1), :], (8, t))
    pair = pltpu.bitcast(word, BF16)
    return jnp.broadcast_to(pair[None], (N_KEYS // 16, 16, t)).reshape(N_KEYS, t)


def _peer_coef(tile, u_ref, row0, h2t_ref, rank1_ref, b1_ref, cnt0_ref, a0_ref, coef_ref):
    t = h2t_ref.shape[1]
    act = jnp.dot(u_ref[row0:row0 + PEER_E, :], h2t_ref[...], preferred_element_type=F32)
    per = PEER_E // N_KEYS
    for ii in range(per):
        i = tile * per + ii
        w = None
        for h in range(PEER_HEADS):
            hs = slice(h * N_KEYS, (h + 1) * N_KEYS)
            cnt = _bcast_pair_rows(cnt0_ref, h * N_KEYS + i, t)
            a = _bcast_pair_rows(a0_ref, h * N_KEYS + i, t)
            term = jnp.where(rank1_ref[hs, :] < cnt, b1_ref[hs, :] * a, jnp.zeros((), BF16))
            w = term if w is None else w + term
        rs = slice(ii * N_KEYS, (ii + 1) * N_KEYS)
        coef_ref[rs, :] = _gelu(act[rs, :]).astype(BF16) * w


def _peer_kernel(h2t_ref, u_ref, vtp_ref, vtc_ref, rank1_ref, b1_ref, cnt0_ref, a0_ref, x1_ref, gt2_ref, gf_ref,
                 y_ref, acc, coef_a, coef_b):
    k = pl.program_id(1)
    last = pl.num_programs(1) - 1
    route = (rank1_ref, b1_ref, cnt0_ref, a0_ref)

    def two_tiles():
        _peer_coef(2 * k, u_ref, 0, h2t_ref, *route, coef_a)
        _peer_coef(2 * k + 1, u_ref, PEER_E, h2t_ref, *route, coef_b)
        return jnp.dot(vtc_ref[...], coef_a[...], preferred_element_type=F32)

    @pl.when(k == 0)
    def _():
        acc[...] = two_tiles()

    @pl.when(jnp.logical_and(k > 0, k < last))
    def _():
        acc[...] += jnp.dot(vtp_ref[...], coef_b[...], preferred_element_type=F32)
        acc[...] += two_tiles()

    @pl.when(k == last)
    def _():
        tot = acc[...] + jnp.dot(vtp_ref[...], coef_b[...], preferred_element_type=F32)
        x2 = x1_ref[...] + gt2_ref[...] * jnp.transpose(tot)
        y_ref[...] = _rms(x2) * gf_ref[...]


def _peer(h2t, u_bf, vt_bf, rank1, b1, cnt0, a0, x1, gt2, gfin, per_token, rows_per_batch):
    n = x1.shape[0]
    t = PEER_T
    tpb = max(rows_per_batch // t, 1)
    rows = PEER_HEADS * N_KEYS
    steps = N_EXPERTS // (2 * PEER_E)
    once = pl.Buffered(1)
    tok = lambda r: pl.BlockSpec((r, t), lambda m, k: (0, m))
    if per_token:
        gt_spec = pl.BlockSpec((t, D_MODEL), lambda m, k: (m, 0), pipeline_mode=once)
    else:
        gt_spec = pl.BlockSpec((None, 1, D_MODEL), lambda m, k: (m // tpb, 0, 0))
    return pl.pallas_call(
        _peer_kernel,
        grid=(n // t, steps + 1),
        in_specs=[
            pl.BlockSpec((D_MODEL, t), lambda m, k: (0, m)),
            pl.BlockSpec((2 * PEER_E, D_MODEL), lambda m, k: (jnp.minimum(k, steps - 1), 0)),
            pl.BlockSpec((None, D_MODEL, PEER_E), lambda m, k: (jnp.maximum(2 * k - 1, 0), 0, 0)),
            pl.BlockSpec((None, D_MODEL, PEER_E), lambda m, k: (2 * jnp.minimum(k, steps - 1), 0, 0)),
            tok(rows), tok(rows), tok(rows), tok(rows),
            pl.BlockSpec((t, D_MODEL), lambda m, k: (m, 0)),
            gt_spec,
            pl.BlockSpec((1, D_MODEL), lambda m, k: (0, 0)),
        ],
        out_specs=pl.BlockSpec((t, D_MODEL), lambda m, k: (m, 0)),
        out_shape=jax.ShapeDtypeStruct((n, D_MODEL), F32),
        scratch_shapes=[pltpu.VMEM((D_MODEL, t), F32), pltpu.VMEM((PEER_E, t), BF16), pltpu.VMEM((PEER_E, t), BF16)],
        compiler_params=_cparams(("parallel", "arbitrary"), VMEM_LIMIT_BIG),
        name="peer",
    )(h2t, u_bf, vt_bf, vt_bf, rank1, b1, cnt0, a0, x1, gt2, gfin)


def _rel_bias_tile(rel_bias, rows, cols, offset):
    rb = rel_bias.astype(F32)
    heads = rb.shape[0]
    rel_max = offset + rows - 1
    rel_min = offset - (cols - 1)
    lo, hi = max(rel_min, -REL_CLIP), min(rel_max, REL_CLIP)
    parts = []
    if rel_min < -REL_CLIP:
        parts.append(jnp.broadcast_to(rb[:, :1], (heads, -REL_CLIP - rel_min)))
    parts.append(rb[:, lo + REL_CLIP:hi + REL_CLIP + 1])
    if rel_max > REL_CLIP:
        parts.append(jnp.broadcast_to(rb[:, -1:], (heads, rel_max - REL_CLIP)))
    g = jnp.flip(jnp.concatenate(parts, axis=1), axis=1)
    period = rows + cols
    gp = jnp.pad(g, ((0, 0), (0, 1)))
    shifted = jnp.tile(gp, (1, rows))[:, :rows * (period - 1)].reshape(heads, rows, period - 1)
    return shifted[:, :, rows - 1:rows - 1 + cols]


def _prompt_bias(rel_bias):
    a = jnp.arange(Q_SUB, dtype=jnp.int32)[:, None]
    c = jnp.arange(K_WIN, dtype=jnp.int32)[None, :]
    cq = a // CHUNK
    ck = c // CHUNK
    ok = (ck >= cq) & (ck <= cq + N_LEFT_CHUNKS)
    return jnp.where(ok[None], _rel_bias_tile(rel_bias, Q_SUB, K_WIN, BAND_LEFT), NEG_INF)


def _sample_bias(rel_bias, seq, lc):
    return _rel_bias_tile(rel_bias, seq, lc, lc), _rel_bias_tile(rel_bias, seq, seq, 0)


def _layer(x, mod, cache_k, cache_v, state, is_sample, wts):
    (g1, w_main, w_alr, rel_bias, wup, balpha, gnorm, wa, wb, wo, g2, wqk_t, u_bf, vt_bf, gfin) = wts
    batch, seq, _ = x.shape
    n = batch * seq
    xf = x.reshape(n, D_MODEL)
    parts = [mod[:, i * D_MODEL:(i + 1) * D_MODEL] for i in range(6)]
    if is_sample:
        sh1, sc1, gt1, sh2, sc2, gt2 = [jnp.repeat(p, seq, axis=0) for p in parts]
        tm_in, tm_mg = n, 256
    else:
        sh1, sc1, gt1, sh2, sc2, gt2 = [p.reshape(batch, 1, D_MODEL) for p in parts]
        tm_in, tm_mg = 1024, 256

    z, alr = _inproj(xf, sh1, sc1, g1, w_main, w_alr, tm_in, is_sample, seq)

    if is_sample:
        lc = cache_k.shape[1]
        bias_c, bias_n = _sample_bias(rel_bias, seq, lc)
        oa = _attn_sample(z, cache_k.reshape(batch, lc * H_A, DH_A), cache_v.reshape(batch, lc * H_A, DH_A),
                          bias_c, bias_n, batch, seq)
        ob, s_new = _gla_sample(z, alr, state, wup, balpha, gnorm, batch, seq)
        k_rows = z[:, Z_KA:Z_KA + W_A].reshape(batch, seq, H_A, DH_A)
        v_rows = z[:, Z_VA:Z_VA + W_A].reshape(batch, seq, H_A, DH_A)
    else:
        oa = _attn_prompt(z, _prompt_bias(rel_bias), batch, seq)
        ob, s_new = _gla_prompt(z, alr, wup, balpha, gnorm, batch, seq)
        keep = min(BAND_LEFT, seq)
        zk = z.reshape(batch, seq, Z_WIDTH)[:, seq - keep:]
        k_rows = zk[:, :, Z_KA:Z_KA + W_A].reshape(batch, keep, H_A, DH_A)
        v_rows = zk[:, :, Z_VA:Z_VA + W_A].reshape(batch, keep, H_A, DH_A)

    x1, h2t = _merge(oa, ob, z, xf, gt1, sc2, sh2, g2, wa, wb, wo, tm_mg, is_sample, seq)
    rank1, b1, cnt0, a0 = _route(h2t, wqk_t)
    y = _peer(h2t, u_bf, vt_bf, rank1, b1, cnt0, a0, x1, gt2, gfin, is_sample, seq)
    return y.reshape(batch, seq, D_MODEL), k_rows, v_rows, s_new


def kernel(x_prompt, x_sample, cache_a_k, cache_a_v, state_gla, c_prompt, c_sample, w_ada, b_ada, g_norm1, w_in, rel_bias, w_alpha_up, b_alpha, g_gla_norm, w_branch_a, w_branch_b, w_out, g_norm2, w_peer_q, peer_subkeys, peer_u, peer_v, g_final):
    depth = w_in.shape[0]
    assert depth == 1, "the final rmsnorm is fused into the PEER kernel of the only layer"
    nbp = c_prompt.shape[0]
    l = 0
    nbs = c_sample.shape[0]
    pad = -(nbp + nbs) % 16
    c_all = jnp.concatenate([c_prompt, c_sample, jnp.zeros((pad, D_MODEL), F32)], axis=0)
    def only(w):
        return w.reshape(w.shape[1:])

    mod = _ada(c_all, only(w_ada), b_ada[l])

    alr_lo = Z_GL
    w_in0 = only(w_in)
    w_main = jnp.concatenate([w_in0[:, :alr_lo].astype(BF16), w_in0[:, alr_lo + ALPHA_RANK:].astype(BF16)], axis=1)
    w_alr = jnp.pad(w_in0[:, alr_lo:alr_lo + ALPHA_RANK], ((0, 0), (0, ALR_PAD - ALPHA_RANK))).astype(BF16)
    wup = jnp.pad(w_alpha_up[l], ((0, ALR_PAD - ALPHA_RANK), (0, 0))).astype(BF16)
    wqk = _qkfold(only(w_peer_q).astype(BF16), peer_subkeys[l].astype(BF16))
    wqk_t = wqk.reshape(D_MODEL, PEER_HEADS, 2, N_KEYS).transpose(2, 3, 1, 0).reshape(2 * PEER_HEADS * N_KEYS, D_MODEL)
    wts = (
        g_norm1[l].reshape(1, D_MODEL), w_main, w_alr, rel_bias[l], wup,
        b_alpha[l].reshape(1, KW_B), g_gla_norm[l].reshape(1, DV_B),
        only(w_branch_a).astype(BF16), only(w_branch_b).astype(BF16), only(w_out).astype(BF16),
        g_norm2[l].reshape(1, D_MODEL), wqk_t.astype(BF16),
        only(peer_u).astype(BF16),
        jnp.transpose(peer_v.astype(BF16).reshape(N_EXPERTS // PEER_E, PEER_E, D_MODEL), (0, 2, 1)),
        g_final.reshape(1, D_MODEL),
    )
    yp, kp, vp, sp = _layer(x_prompt, mod[:nbp], None, None, None, False, wts)
    ys, ks, vs, ss = _layer(x_sample, mod[nbp:nbp + nbs], only(cache_a_k), only(cache_a_v), only(state_gla), True, wts)
    return (yp, ys, kp[None], vp[None], sp[None], ks[None], vs[None], ss[None])
```
